```python
import jax, jax.numpy as jnp
from jax import lax
import numpy as np

D_MODEL = 1024
BATCH = 8
SEQ = 4096
DEPTH = 4

HEAD_DIM = 64
N_NSA_HEADS = 8
N_NSA_KV = 2
NSA_GROUP = N_NSA_HEADS // N_NSA_KV
N_FOX_HEADS = 8
MIX_WIDTH = (N_NSA_HEADS + N_FOX_HEADS) * HEAD_DIM
N_BRANCH = 3
CMP_LEN = 32
CMP_STRIDE = 16
CMP_HIDDEN = 256
SEL_BLOCK = 64
SEL_TOPK = 16
WINDOW = 512
Q_BLOCK = 128
SEL_Q_BLOCK = 32
ROPE_THETA = 500000.0
ROPE_DIMS = HEAD_DIM // 4
D_FF = 4 * D_MODEL
NORM_EPS = 1e-6
NEG_INF = -1e30
FORCED_SCORE = 1e4
NSA_Q_COLS = N_NSA_HEADS * HEAD_DIM
KV_COLS = N_NSA_KV * HEAD_DIM
NSA_GATE_COLS = N_NSA_HEADS * N_BRANCH
FOX_COLS = N_FOX_HEADS * HEAD_DIM
FOX_F_COLS = N_FOX_HEADS
COL_SIZES = (NSA_Q_COLS, KV_COLS, KV_COLS, KV_COLS, KV_COLS, KV_COLS, KV_COLS, NSA_GATE_COLS, FOX_COLS, FOX_COLS, FOX_COLS, FOX_F_COLS)
IN_COLS = NSA_Q_COLS + 6 * KV_COLS + NSA_GATE_COLS + 3 * FOX_COLS + FOX_F_COLS

kernel_name = "nsa_fox_hymba_sandwich_trunk"


def _rms_norm(x, g):
    xf = x.astype(jnp.float32)
    y = xf * lax.rsqrt(jnp.mean(xf * xf, axis=-1, keepdims=True) + NORM_EPS)
    return (y * g.astype(jnp.float32)).astype(x.dtype)


def _partial_rope(x, pos):
    half = ROPE_DIMS // 2
    inv_freq = ROPE_THETA ** (-jnp.arange(half, dtype=jnp.float32) * 2.0 / ROPE_DIMS)
    ang = pos.astype(jnp.float32)[:, None] * inv_freq[None, :]
    cos, sin = jnp.cos(ang), jnp.sin(ang)
    xf = x.astype(jnp.float32)
    x1, x2 = xf[..., :half], xf[..., half:ROPE_DIMS]
    out = jnp.concatenate([x1 * cos - x2 * sin, x1 * sin + x2 * cos, xf[..., ROPE_DIMS:]], axis=-1)
    return out.astype(x.dtype)


def _masked_softmax(s, mask):
    return jax.nn.softmax(jnp.where(mask, s, NEG_INF), axis=-1)


def _unblock(t, axis):
    t = jnp.moveaxis(t, 0, axis)
    shp = t.shape
    return t.reshape(shp[:axis] + (shp[axis] * shp[axis + 1],) + shp[axis + 2:])


def _importance_map(seq):
    n_cmp = (seq - CMP_LEN) // CMP_STRIDE + 1
    n_sel = seq // SEL_BLOCK
    cs = np.arange(n_cmp)[:, None] * CMP_STRIDE
    ss = np.arange(n_sel)[None, :] * SEL_BLOCK
    overlap = np.clip(np.minimum(cs + CMP_LEN, ss + SEL_BLOCK) - np.maximum(cs, ss), 0, None)
    return jnp.asarray(overlap / CMP_LEN, dtype=jnp.float32)


def _compress(kv, pos_emb, w1, b1, w2, b2):
    b, g, s, d = kv.shape
    r = CMP_LEN // CMP_STRIDE
    n_chunk = s // CMP_STRIDE
    n_cmp = n_chunk - r + 1
    chunks = kv.reshape(b, g, n_chunk, CMP_STRIDE, d)
    blocks = jnp.concatenate([chunks[:, :, j:j + n_cmp] for j in range(r)], axis=3)
    blocks = (blocks + pos_emb).reshape(b, g, n_cmp, CMP_LEN * d)
    hid = jax.nn.gelu(blocks @ w1 + b1)
    return hid @ w2 + b2


def _nsa_mixer(q, k_c, v_c, k_s, v_s, k_w, v_w, gate_logits, gate_bias,
               pos_k, w1_k, b1_k, w2_k, b2_k, pos_v, w1_v, b1_v, w2_v, b2_v):
    b, s, _, d = q.shape
    pos = jnp.arange(s)
    scale = HEAD_DIM ** -0.5
    q = q.reshape(b, s, N_NSA_KV, NSA_GROUP, d).transpose(0, 2, 3, 1, 4)
    q_rot = _partial_rope(q, pos)
    k_c, v_c, k_s, v_s, k_w, v_w = (t.transpose(0, 2, 1, 3) for t in (k_c, v_c, k_s, v_s, k_w, v_w))
    kc = _compress(k_c, pos_k, w1_k, b1_k, w2_k, b2_k)
    vc = _compress(v_c, pos_v, w1_v, b1_v, w2_v, b2_v)
    n_cmp = kc.shape[2]
    cmp_end = jnp.arange(n_cmp) * CMP_STRIDE + CMP_LEN - 1
    imp_map = _importance_map(s)
    n_sel = s // SEL_BLOCK
    top_k = min(SEL_TOPK, n_sel)
    k_s_rot = _partial_rope(k_s, pos)
    pad = ((0, 0), (0, 0), (WINDOW, 0), (0, 0))
    k_w_pad = jnp.pad(_partial_rope(k_w, pos), pad)
    v_w_pad = jnp.pad(v_w, pad)
    blk = jnp.arange(n_sel)

    def block_fn(i):
        q0 = i * Q_BLOCK
        qpos = q0 + jnp.arange(Q_BLOCK)
        qp = lax.dynamic_slice_in_dim(q, q0, Q_BLOCK, axis=3)
        qr = lax.dynamic_slice_in_dim(q_rot, q0, Q_BLOCK, axis=3)
        sc = jnp.einsum('bghqd,bgnd->bghqn', qp, kc).astype(jnp.float32) * scale
        mc = cmp_end[None, :] <= qpos[:, None]
        pc = _masked_softmax(sc, mc) * mc
        o_cmp = jnp.einsum('bghqn,bgnd->bghqd', pc.astype(vc.dtype), vc)
        imp = jnp.einsum('bghqn,nj->bgqj', pc, imp_map)
        cur = qpos // SEL_BLOCK
        valid = blk[None, :] * SEL_BLOCK <= qpos[:, None]
        forced = (blk[None, :] == 0) | (blk[None, :] == cur[:, None]) | (blk[None, :] == cur[:, None] - 1)
        score = jnp.where(forced, FORCED_SCORE, jnp.where(valid, imp, -1.0))
        _, idx = lax.top_k(score, top_k)
        kw = lax.dynamic_slice_in_dim(k_w_pad, q0, WINDOW + Q_BLOCK, axis=2)
        vw = lax.dynamic_slice_in_dim(v_w_pad, q0, WINDOW + Q_BLOCK, axis=2)
        kpos = q0 - WINDOW + jnp.arange(WINDOW + Q_BLOCK)
        mw = (kpos[None, :] <= qpos[:, None]) & (kpos[None, :] > qpos[:, None] - WINDOW) & (kpos[None, :] >= 0)
        sw = jnp.einsum('bghqd,bgkd->bghqk', qr, kw).astype(jnp.float32) * scale
        pw = _masked_softmax(sw, mw)
        o_win = jnp.einsum('bghqk,bgkd->bghqd', pw.astype(vw.dtype), vw)
        return o_cmp, o_win, idx

    o_cmp, o_win, sel_idx = lax.map(block_fn, jnp.arange(s // Q_BLOCK))
    o_cmp = _unblock(o_cmp, 3)
    o_win = _unblock(o_win, 3)
    sel_idx = _unblock(sel_idx, 2)

    kb = k_s_rot.reshape(b, N_NSA_KV, n_sel, SEL_BLOCK, d)
    vb = v_s.reshape(b, N_NSA_KV, n_sel, SEL_BLOCK, d)
    gather = jax.vmap(jax.vmap(lambda blocks, ix: blocks[ix]))
    n_keys = top_k * SEL_BLOCK

    def sel_fn(c):
        q0 = c * SEL_Q_BLOCK
        qpos = q0 + jnp.arange(SEL_Q_BLOCK)
        qr = lax.dynamic_slice_in_dim(q_rot, q0, SEL_Q_BLOCK, axis=3)
        ix = lax.dynamic_slice_in_dim(sel_idx, q0, SEL_Q_BLOCK, axis=2)
        kg = gather(kb, ix)
        vg = gather(vb, ix).reshape(b, N_NSA_KV, SEL_Q_BLOCK, n_keys, d)
        kpos = ix[..., None] * SEL_BLOCK + jnp.arange(SEL_BLOCK)
        m = (kpos <= qpos[None, None, :, None, None]).reshape(b, N_NSA_KV, 1, SEL_Q_BLOCK, n_keys)
        ss = jnp.einsum('bghqd,bgqnkd->bghqnk', qr, kg).astype(jnp.float32) * scale
        ps = _masked_softmax(ss.reshape(b, N_NSA_KV, NSA_GROUP, SEL_Q_BLOCK, n_keys), m)
        return jnp.einsum('bghqm,bgqmd->bghqd', ps.astype(vg.dtype), vg)

    o_sel = _unblock(lax.map(sel_fn, jnp.arange(s // SEL_Q_BLOCK)), 3)

    g = jax.nn.sigmoid((gate_logits + gate_bias).astype(jnp.float32)).astype(q.dtype)
    g = g.reshape(b, s, N_NSA_KV, NSA_GROUP, N_BRANCH).transpose(0, 2, 3, 1, 4)
    o = g[..., 0:1] * o_cmp + g[..., 1:2] * o_sel + g[..., 2:3] * o_win
    return o.transpose(0, 3, 1, 2, 4).reshape(b, s, N_NSA_HEADS * d)


def _fox_mixer(q, k, v, f_logits, f_bias):
    b, s, h, d = q.shape
    scale = HEAD_DIM ** -0.5
    q, k, v = (t.transpose(0, 2, 1, 3) for t in (q, k, v))
    log_f = jax.nn.log_sigmoid(f_logits.astype(jnp.float32) + f_bias.astype(jnp.float32))
    cum = jnp.cumsum(log_f, axis=1).transpose(0, 2, 1)
    outs = []
    for i in range(s // Q_BLOCK):
        q0, q1 = i * Q_BLOCK, (i + 1) * Q_BLOCK
        logits = (jnp.einsum('bhqd,bhkd->bhqk', q[:, :, q0:q1], k[:, :, :q1]).astype(jnp.float32) * scale
                  + cum[:, :, q0:q1, None] - cum[:, :, None, :q1])
        mask = jnp.arange(q1)[None, :] <= jnp.arange(q0, q1)[:, None]
        p = _masked_softmax(logits, mask)
        outs.append(jnp.einsum('bhqk,bhkd->bhqd', p.astype(v.dtype), v[:, :, :q1]))
    o = jnp.concatenate(outs, axis=2)
    return o.transpose(0, 2, 1, 3).reshape(b, s, h * d)


def setup_inputs(seed: int = 0) -> dict:
    key = jax.random.key(seed)
    ks = jax.random.split(key, 22)
    f32 = jnp.float32
    nrm = lambda k, shape, fan_in: jax.random.normal(k, shape, f32) * (fan_in ** -0.5)
    small = lambda k, shape, sc: sc * jax.random.normal(k, shape, f32)
    gain = lambda k: 1.0 + 0.05 * jax.random.normal(k, (DEPTH, D_MODEL), f32)
    cmp_in = CMP_LEN * HEAD_DIM
    return {
        "x": jax.random.normal(ks[0], (BATCH, SEQ, D_MODEL), f32),
        "w_in": nrm(ks[1], (DEPTH, D_MODEL, IN_COLS), D_MODEL),
        "b_nsa_gate": small(ks[2], (DEPTH, NSA_GATE_COLS), 0.01),
        "b_forget": jax.random.uniform(ks[3], (DEPTH, N_FOX_HEADS), f32, minval=1.0, maxval=6.0),
        "cmp_pos_k": small(ks[4], (DEPTH, CMP_LEN, HEAD_DIM), 0.1),
        "cmp_w1_k": nrm(ks[5], (DEPTH, cmp_in, CMP_HIDDEN), cmp_in),
        "cmp_b1_k": small(ks[6], (DEPTH, CMP_HIDDEN), 0.01),
        "cmp_w2_k": nrm(ks[7], (DEPTH, CMP_HIDDEN, HEAD_DIM), CMP_HIDDEN),
        "cmp_b2_k": small(ks[8], (DEPTH, HEAD_DIM), 0.01),
        "cmp_pos_v": small(ks[9], (DEPTH, CMP_LEN, HEAD_DIM), 0.1),
        "cmp_w1_v": nrm(ks[10], (DEPTH, cmp_in, CMP_HIDDEN), cmp_in),
        "cmp_b1_v": small(ks[11], (DEPTH, CMP_HIDDEN), 0.01),
        "cmp_w2_v": nrm(ks[12], (DEPTH, CMP_HIDDEN, HEAD_DIM), CMP_HIDDEN),
        "cmp_b2_v": small(ks[13], (DEPTH, HEAD_DIM), 0.01),
        "w_out": nrm(ks[14], (DEPTH, MIX_WIDTH, D_MODEL), MIX_WIDTH),
        "w_up": nrm(ks[15], (DEPTH, D_MODEL, D_FF), D_MODEL),
        "w_down": nrm(ks[16], (DEPTH, D_FF, D_MODEL), D_FF),
        "g_pre_mix": gain(ks[17]),
        "g_post_mix": gain(ks[18]),
        "g_pre_mlp": gain(ks[19]),
        "g_post_mlp": gain(ks[20]),
    }


def reference(x, w_in, b_nsa_gate, b_forget, cmp_pos_k, cmp_w1_k, cmp_b1_k, cmp_w2_k, cmp_b2_k,
              cmp_pos_v, cmp_w1_v, cmp_b1_v, cmp_w2_v, cmp_b2_v, w_out, w_up, w_down,
              g_pre_mix, g_post_mix, g_pre_mlp, g_post_mlp):
    b, s, _ = x.shape
    split_at = [int(v) for v in np.cumsum(COL_SIZES)[:-1]]
    for l in range(DEPTH):
        h = _rms_norm(x, g_pre_mix[l])
        parts = jnp.split(h @ w_in[l], split_at, axis=-1)
        nq, kc, vc, ksel, vsel, kwin, vwin, gates, fq, fk, fv, ff = parts
        heads = lambda t, n: t.reshape(b, s, n, HEAD_DIM)
        o_nsa = _nsa_mixer(heads(nq, N_NSA_HEADS),
                           heads(kc, N_NSA_KV), heads(vc, N_NSA_KV),
                           heads(ksel, N_NSA_KV), heads(vsel, N_NSA_KV),
                           heads(kwin, N_NSA_KV), heads(vwin, N_NSA_KV),
                           gates, b_nsa_gate[l],
                           cmp_pos_k[l], cmp_w1_k[l], cmp_b1_k[l], cmp_w2_k[l], cmp_b2_k[l],
                           cmp_pos_v[l], cmp_w1_v[l], cmp_b1_v[l], cmp_w2_v[l], cmp_b2_v[l])
        o_fox = _fox_mixer(heads(fq, N_FOX_HEADS), heads(fk, N_FOX_HEADS), heads(fv, N_FOX_HEADS),
                           ff, b_forget[l])
        mix = jnp.concatenate([o_nsa, o_fox], axis=-1) @ w_out[l]
        x = x + _rms_norm(mix, g_post_mix[l])
        h = _rms_norm(x, g_pre_mlp[l])
        y = jnp.square(jax.nn.relu(h @ w_up[l])) @ w_down[l]
        x = x + _rms_norm(y, g_post_mlp[l])
    return x
```

```python
import functools

import numpy as np
import jax
import jax.numpy as jnp
from jax import lax
from jax.experimental import pallas as pl
from jax.experimental.pallas import tpu as pltpu

HEAD_DIM = 64
N_NSA_HEADS = 8
N_NSA_KV = 2
NSA_GROUP = N_NSA_HEADS // N_NSA_KV
N_FOX_HEADS = 8
N_BRANCH = 3
CMP_LEN = 32
CMP_STRIDE = 16
CMP_HIDDEN = 256
SEL_BLOCK = 64
SEL_SHIFT = 6
SEL_TOPK = 16
WINDOW = 512
ROPE_THETA = 500000.0
ROPE_DIMS = HEAD_DIM // 4
NORM_EPS = 1e-6
NEG_INF = -1e30
FORCED_SCORE = 1e4
SCALE = HEAD_DIM ** -0.5

LANE = 128
MXU_DTYPE = jnp.bfloat16
F32 = jnp.float32
VMEM_LIMIT = 56 * 1024 * 1024

_SRC_NQ, _SRC_KC, _SRC_VC, _SRC_KSEL, _SRC_VSEL, _SRC_KWIN, _SRC_VWIN = 0, 512, 640, 768, 896, 1024, 1152
_SRC_GATE, _SRC_FQ, _SRC_FK, _SRC_FV, _SRC_FF = 1280, 1304, 1816, 2328, 2840
_Q0, _KVC0, _KS0, _VS0, _KW0, _VW0, _FQ0, _FK0, _FV0, _SM0, _NP = (
    0, 1024, 1536, 1792, 2048, 2304, 2560, 3072, 3584, 4096, 4352)
_FF_LANE = 16


def _projection_columns():
    src = np.full((_NP,), -1, np.int32)
    d = np.arange(HEAD_DIM)
    for h in range(N_NSA_HEADS):
        src[_Q0 + LANE * h + d] = _SRC_NQ + HEAD_DIM * h + d
    for i in range(4):
        src[_KVC0 + LANE * i + d] = _SRC_KC + HEAD_DIM * i + d
    for g in range(N_NSA_KV):
        src[_KS0 + LANE * g + d] = _SRC_KSEL + HEAD_DIM * g + d
        src[_KW0 + LANE * g + d] = _SRC_KWIN + HEAD_DIM * g + d
        for half in range(2):
            src[_VS0 + LANE * g + HEAD_DIM * half + d] = _SRC_VSEL + HEAD_DIM * g + d
            src[_VW0 + LANE * g + HEAD_DIM * half + d] = _SRC_VWIN + HEAD_DIM * g + d
        gates = np.arange(NSA_GROUP * N_BRANCH)
        src[_SM0 + LANE * g + gates] = _SRC_GATE + NSA_GROUP * N_BRANCH * g + gates
        src[_SM0 + LANE * g + _FF_LANE + np.arange(N_FOX_HEADS)] = _SRC_FF + np.arange(N_FOX_HEADS)
    fox = np.arange(N_FOX_HEADS * HEAD_DIM)
    src[_FQ0 + fox] = _SRC_FQ + fox
    src[_FK0 + fox] = _SRC_FK + fox
    src[_FV0 + fox] = _SRC_FV + fox
    return src


def _rms(x, g):
    ms = jnp.mean(x * x, axis=-1, keepdims=True)
    return x * lax.rsqrt(ms + NORM_EPS) * g


def _dot(a, b):
    return jnp.dot(a, b, preferred_element_type=F32)


def _dot_nt(a, b):
    return lax.dot_general(a, b, (((1,), (1,)), ((), ())), preferred_element_type=F32)


def _split3(x):
    hi = x.astype(MXU_DTYPE)
    r1 = x - hi.astype(F32)
    mid = r1.astype(MXU_DTYPE)
    lo = (r1 - mid.astype(F32)).astype(MXU_DTYPE)
    return hi, mid, lo


def _rows(shape):
    return lax.broadcasted_iota(jnp.int32, shape, 0)


def _lanes(shape):
    return lax.broadcasted_iota(jnp.int32, shape, 1)


def _inproj_body(x_ref, g_ref, w_ref, rc_ref, ra_ref, rb_ref,
                 qn_ref, qr_ref, kvc_ref, ksa_ref, vs_ref, kw_ref, vw_ref,
                 fq_ref, fk_ref, fv_ref, small_ref, *, tm, n_s_tiles):
    h = _rms(x_ref[...], g_ref[...]).astype(MXU_DTYPE)
    rc, ra, rb = rc_ref[...], ra_ref[...], rb_ref[...]

    def mm(lo, n):
        return _dot(h, w_ref[:, lo:lo + n])

    def rope(v):
        return v * rc + pltpu.roll(v, LANE - ROPE_DIMS // 2, 1) * ra + pltpu.roll(v, ROPE_DIMS // 2, 1) * rb

    q = mm(_Q0, N_NSA_HEADS * LANE) * SCALE
    qn_ref[...] = q.astype(MXU_DTYPE)
    for hh in range(N_NSA_HEADS):
        sl = slice(LANE * hh, LANE * (hh + 1))
        qr_ref[:, sl] = rope(q[:, sl]).astype(MXU_DTYPE)

    kvc = mm(_KVC0, 4 * LANE)
    for i in range(4):
        kvc_ref[i] = kvc[:, LANE * i:LANE * i + HEAD_DIM]

    pos = (pl.program_id(0) % n_s_tiles) * tm + _rows((tm, LANE))
    onehot = ((_lanes((tm, LANE)) - HEAD_DIM) == (pos >> SEL_SHIFT)).astype(F32)
    ks = mm(_KS0, N_NSA_KV * LANE)
    kw = mm(_KW0, N_NSA_KV * LANE)
    for g in range(N_NSA_KV):
        sl = slice(LANE * g, LANE * (g + 1))
        ksa_ref[:, sl] = (rope(ks[:, sl]) + onehot).astype(MXU_DTYPE)
        kw_ref[:, sl] = rope(kw[:, sl]).astype(MXU_DTYPE)
    vs_ref[...] = mm(_VS0, N_NSA_KV * LANE).astype(MXU_DTYPE)
    vw_ref[...] = mm(_VW0, N_NSA_KV * LANE).astype(MXU_DTYPE)

    nf = N_FOX_HEADS * HEAD_DIM
    fq_ref[...] = (mm(_FQ0, nf) * SCALE).astype(MXU_DTYPE)
    fk_ref[...] = mm(_FK0, nf).astype(MXU_DTYPE)
    fv_ref[...] = mm(_FV0, nf).astype(MXU_DTYPE)
    small_ref[...] = mm(_SM0, N_NSA_KV * LANE)


def _inproj(xf, g, wp, rope_c, rope_a, rope_b, *, seq, tm):
    m, d = xf.shape
    n_s_tiles = seq // tm
    nf = N_FOX_HEADS * HEAD_DIM
    row = lambda i: (i, 0)
    fixed = lambda i: (0, 0)
    tab = lambda i: (i % n_s_tiles, 0)
    out_shape = (
        jax.ShapeDtypeStruct((m, N_NSA_HEADS * LANE), MXU_DTYPE),
        jax.ShapeDtypeStruct((m, N_NSA_HEADS * LANE), MXU_DTYPE),
        jax.ShapeDtypeStruct((4, m, HEAD_DIM), F32),
        jax.ShapeDtypeStruct((m, N_NSA_KV * LANE), MXU_DTYPE),
        jax.ShapeDtypeStruct((m, N_NSA_KV * LANE), MXU_DTYPE),
        jax.ShapeDtypeStruct((m, N_NSA_KV * LANE), MXU_DTYPE),
        jax.ShapeDtypeStruct((m, N_NSA_KV * LANE), MXU_DTYPE),
        jax.ShapeDtypeStruct((m, nf), MXU_DTYPE),
        jax.ShapeDtypeStruct((m, nf), MXU_DTYPE),
        jax.ShapeDtypeStruct((m, nf), MXU_DTYPE),
        jax.ShapeDtypeStruct((m, N_NSA_KV * LANE), F32),
    )
    out_specs = (
        pl.BlockSpec((tm, N_NSA_HEADS * LANE), row),
        pl.BlockSpec((tm, N_NSA_HEADS * LANE), row),
        pl.BlockSpec((4, tm, HEAD_DIM), lambda i: (0, i, 0)),
        pl.BlockSpec((tm, N_NSA_KV * LANE), row),
        pl.BlockSpec((tm, N_NSA_KV * LANE), row),
        pl.BlockSpec((tm, N_NSA_KV * LANE), row),
        pl.BlockSpec((tm, N_NSA_KV * LANE), row),
        pl.BlockSpec((tm, nf), row),
        pl.BlockSpec((tm, nf), row),
        pl.BlockSpec((tm, nf), row),
        pl.BlockSpec((tm, N_NSA_KV * LANE), row),
    )
    return pl.pallas_call(
        functools.partial(_inproj_body, tm=tm, n_s_tiles=n_s_tiles),
        grid=(m // tm,),
        in_specs=[
            pl.BlockSpec((tm, d), row),
            pl.BlockSpec((1, d), fixed),
            pl.BlockSpec((d, _NP), fixed),
            pl.BlockSpec((tm, LANE), tab),
            pl.BlockSpec((tm, LANE), tab),
            pl.BlockSpec((tm, LANE), tab),
        ],
        out_specs=out_specs,
        out_shape=out_shape,
        compiler_params=pltpu.CompilerParams(
            dimension_semantics=("arbitrary",), vmem_limit_bytes=VMEM_LIMIT),
        name="inproj",
    )(xf, g, wp, rope_c, rope_a, rope_b)


def _cumsum_body(sm_ref, bias_ref, tril_ref, out_ref, *, seq, chunk):
    carry = jnp.zeros((1, LANE), F32)
    tril = tril_ref[...]
    for c in range(seq // chunk):
        sl = slice(c * chunk, (c + 1) * chunk)
        z = sm_ref[sl, :] + bias_ref[...]
        logf = jnp.minimum(z, 0.0) - jnp.log1p(jnp.exp(-jnp.abs(z)))
        hi, mid, lo = _split3(logf)
        cs = (_dot(tril, hi) + _dot(tril, mid)) + _dot(tril, lo) + carry
        out_ref[sl, :] = cs
        carry = cs[chunk - 1:chunk, :]


def _forget_cumsum(small, fbias, tril, *, batch, seq):
    chunk = tril.shape[0]
    return pl.pallas_call(
        functools.partial(_cumsum_body, seq=seq, chunk=chunk),
        grid=(batch,),
        in_specs=[
            pl.BlockSpec((seq, LANE), lambda b: (b, 0)),
            pl.BlockSpec((1, LANE), lambda b: (0, 0)),
            pl.BlockSpec((chunk, chunk), lambda b: (0, 0)),
        ],
        out_specs=pl.BlockSpec((seq, LANE), lambda b: (b, 0)),
        out_shape=jax.ShapeDtypeStruct((batch * seq, LANE), F32),
        compiler_params=pltpu.CompilerParams(
            dimension_semantics=("arbitrary",), vmem_limit_bytes=VMEM_LIMIT),
        name="forget_cumsum",
    )(small, fbias, tril)


def _compress_body(x_ref, pos_ref, w1_ref, b1_ref, w2_ref, b2_ref, o_ref, *, n_chunk):
    half = CMP_STRIDE * HEAD_DIM
    x = x_ref[0, 0]
    first = _dot((x + pos_ref[0, 0:1, :]).astype(MXU_DTYPE), w1_ref[0, 0:half, :])
    second = _dot((x + pos_ref[0, 1:2, :]).astype(MXU_DTYPE), w1_ref[0, half:2 * half, :])
    hid = first + pltpu.roll(second, n_chunk - 1, 0) + b1_ref[0]
    act = jax.nn.gelu(hid)
    o_ref[0, 0] = (_dot(act.astype(MXU_DTYPE), w2_ref[0]) + b2_ref[0]).astype(MXU_DTYPE)


def _compress(kvc, pos, w1, b1, w2p, b2p, *, batch, n_chunk):
    half = CMP_STRIDE * HEAD_DIM
    kv = lambda i, b: (i // N_NSA_KV, 0, 0)
    return pl.pallas_call(
        functools.partial(_compress_body, n_chunk=n_chunk),
        grid=(2 * N_NSA_KV, batch),
        in_specs=[
            pl.BlockSpec((1, 1, n_chunk, half), lambda i, b: (i, b, 0, 0)),
            pl.BlockSpec((1, 2, half), kv),
            pl.BlockSpec((1, 2 * half, CMP_HIDDEN), kv),
            pl.BlockSpec((1, 1, CMP_HIDDEN), kv),
            pl.BlockSpec((1, CMP_HIDDEN, LANE), kv),
            pl.BlockSpec((1, 1, LANE), kv),
        ],
        out_specs=pl.BlockSpec((1, 1, n_chunk, LANE), lambda i, b: (i, b, 0, 0)),
        out_shape=jax.ShapeDtypeStruct((2 * N_NSA_KV, batch, n_chunk, LANE), MXU_DTYPE),
        compiler_params=pltpu.CompilerParams(
            dimension_semantics=("arbitrary", "arbitrary"), vmem_limit_bytes=VMEM_LIMIT),
        name="compress",
    )(kvc, pos, w1, b1, w2p, b2p)


def _softmax_update(s, v, m_ref, l_ref, acc_ref):
    m_prev = m_ref[...]
    m_new = jnp.maximum(m_prev, jnp.max(s, axis=-1, keepdims=True))
    alpha = jnp.exp(m_prev - m_new)
    p = jnp.exp(s - m_new)
    l_ref[...] = alpha * l_ref[...] + jnp.sum(p, axis=-1, keepdims=True)
    acc_ref[...] = alpha * acc_ref[...] + _dot(p.astype(MXU_DTYPE), v)
    m_ref[...] = m_new


def _nsa_body(qn_ref, qr_ref, kc_ref, vc_ref, ksa_ref, vs_ref, kw_ref, vw_ref, sm_ref, gb_ref, impt_ref,
              o_ref, m_ref, l_ref, acc_ref, *, tq, tk, n_sel, top_k):
    q0 = pl.program_id(2) * tq
    r = NSA_GROUP * tq
    n_cmp = kc_ref.shape[2]

    q4n = jnp.concatenate([qn_ref[:, LANE * h:LANE * (h + 1)] for h in range(NSA_GROUP)], axis=0)
    sc = _dot_nt(q4n, kc_ref[0, 0])
    rowq = q0 + (_rows((r, n_cmp)) & (tq - 1))
    mc = _lanes((r, n_cmp)) * CMP_STRIDE + (CMP_LEN - 1) <= rowq
    scm = jnp.where(mc, sc, NEG_INF)
    e = jnp.exp(scm - jnp.max(scm, axis=-1, keepdims=True))
    pc = jnp.where(mc, e / jnp.sum(e, axis=-1, keepdims=True), 0.0)
    o_cmp = _dot(pc.astype(MXU_DTYPE), vc_ref[0, 0])

    pcs = (pc[0:tq] + pc[tq:2 * tq]) + (pc[2 * tq:3 * tq] + pc[3 * tq:4 * tq])
    impt_w = impt_ref[...]
    hi, mid, lo = _split3(pcs)
    imp_t = (_dot_nt(impt_w, hi) + _dot_nt(impt_w, mid)) + _dot_nt(impt_w, lo)
    jrow = _rows((n_sel, tq))
    qpos = q0 + _lanes((n_sel, tq))
    cur = qpos >> SEL_SHIFT
    valid = jrow * SEL_BLOCK <= qpos
    forced = (jrow == 0) | (jrow == cur) | (jrow == cur - 1)
    score = jnp.where(forced, FORCED_SCORE, jnp.where(valid, imp_t, -1.0))

    sub = 8
    chunks = [score[sub * c:sub * (c + 1), :] for c in range(n_sel // sub)]
    ranks = [jnp.zeros((sub, tq), F32) for _ in chunks]
    later = _rows((sub, tq))
    for jp in range(n_sel):
        cj, rj = divmod(jp, sub)
        rowv = chunks[cj][rj:rj + 1, :]
        for c, sc_c in enumerate(chunks):
            if c > cj:
                beats = rowv >= sc_c
            elif c < cj:
                beats = rowv > sc_c
            else:
                beats = (rowv > sc_c) | ((rowv == sc_c) & (later > rj))
            ranks[c] = ranks[c] + jnp.where(beats, 1.0, 0.0)
    sel_t = jnp.where(jnp.concatenate(ranks, axis=0) < float(top_k), 1.0, 0.0)
    sel_pad = jnp.concatenate([sel_t, jnp.zeros((LANE - n_sel, tq), F32)], axis=0)
    sel_q = pltpu.roll(sel_pad.T, HEAD_DIM, 1)
    sel_bias = jnp.where(_lanes((tq, LANE)) >= HEAD_DIM, (sel_q - 1.0) * (-NEG_INF), 0.0)

    q4r = jnp.concatenate(
        [(qr_ref[:, LANE * h:LANE * (h + 1)].astype(F32) + sel_bias).astype(MXU_DTYPE)
         for h in range(NSA_GROUP)], axis=0)

    m_ref[...] = jnp.full((r, 1), NEG_INF, F32)
    l_ref[...] = jnp.zeros((r, 1), F32)
    acc_ref[...] = jnp.zeros((r, LANE), F32)
    rowq_k = q0 + (_rows((r, tk)) & (tq - 1))

    def sel_step(kb, masked):
        k0 = pl.multiple_of(kb * tk, tk)
        s = _dot_nt(q4r, ksa_ref[pl.ds(k0, tk), :])
        if masked:
            s = jnp.where(k0 + _lanes((r, tk)) <= rowq_k, s, NEG_INF)
        _softmax_update(s, vs_ref[pl.ds(k0, tk), :], m_ref, l_ref, acc_ref)

    n_full = lax.div(q0, tk)

    def full_step(kb, carry):
        sel_step(kb, False)
        return carry

    lax.fori_loop(0, n_full, full_step, 0)
    sel_step(n_full, True)
    o_sel = acc_ref[...] / l_ref[...]

    span = WINDOW + tq
    ks0 = pl.multiple_of(jnp.maximum(q0 - WINDOW, 0), tq)
    sw = _dot_nt(q4r, kw_ref[pl.ds(ks0, span), :])
    rowq_w = q0 + (_rows((r, span)) & (tq - 1))
    kpos = ks0 + _lanes((r, span))
    mw = (kpos <= rowq_w) & (kpos > rowq_w - WINDOW)
    swm = jnp.where(mw, sw, NEG_INF)
    ew = jnp.exp(swm - jnp.max(swm, axis=-1, keepdims=True))
    pw = ew / jnp.sum(ew, axis=-1, keepdims=True)
    o_win = _dot(pw.astype(MXU_DTYPE), vw_ref[pl.ds(ks0, span), :])

    gate = jax.nn.sigmoid(sm_ref[...] + gb_ref[...])
    heads = []
    for h in range(NSA_GROUP):
        rs = slice(h * tq, (h + 1) * tq)
        c = N_BRANCH * h
        heads.append(gate[:, c:c + 1] * o_cmp[rs] + gate[:, c + 1:c + 2] * o_sel[rs]
                     + gate[:, c + 2:c + 3] * o_win[rs])
    low = _lanes((tq, LANE)) < HEAD_DIM
    for p in range(NSA_GROUP // 2):
        o_ref[:, LANE * p:LANE * (p + 1)] = jnp.where(low, heads[2 * p], heads[2 * p + 1]).astype(MXU_DTYPE)


def _nsa(qn, qr, cmp_kv, ksa, vs, kw, vw, small, gbias, imp_t, *, batch, seq, tq, tk):
    n_q = seq // tq
    n_cmp = cmp_kv.shape[2]
    n_sel = seq // SEL_BLOCK
    top_k = min(SEL_TOPK, n_sel)
    gw = NSA_GROUP * LANE
    qmap = lambda b, g, i: (b * n_q + i, g)
    kvmap = lambda b, g, i: (b, g)
    r = NSA_GROUP * tq
    return pl.pallas_call(
        functools.partial(_nsa_body, tq=tq, tk=tk, n_sel=n_sel, top_k=top_k),
        grid=(batch, N_NSA_KV, n_q),
        in_specs=[
            pl.BlockSpec((tq, gw), qmap),
            pl.BlockSpec((tq, gw), qmap),
            pl.BlockSpec((1, 1, n_cmp, LANE), lambda b, g, i: (g, b, 0, 0)),
            pl.BlockSpec((1, 1, n_cmp, LANE), lambda b, g, i: (N_NSA_KV + g, b, 0, 0)),
            pl.BlockSpec((seq, LANE), kvmap),
            pl.BlockSpec((seq, LANE), kvmap),
            pl.BlockSpec((seq, LANE), kvmap),
            pl.BlockSpec((seq, LANE), kvmap),
            pl.BlockSpec((tq, LANE), qmap),
            pl.BlockSpec((1, LANE), lambda b, g, i: (0, g)),
            pl.BlockSpec((n_sel, n_cmp), lambda b, g, i: (0, 0)),
        ],
        out_specs=pl.BlockSpec((tq, NSA_GROUP * HEAD_DIM), qmap),
        out_shape=jax.ShapeDtypeStruct((batch * seq, N_NSA_HEADS * HEAD_DIM), MXU_DTYPE),
        scratch_shapes=[
            pltpu.VMEM((r, 1), F32),
            pltpu.VMEM((r, 1), F32),
            pltpu.VMEM((r, LANE), F32),
        ],
        compiler_params=pltpu.CompilerParams(
            dimension_semantics=("arbitrary", "arbitrary", "arbitrary"), vmem_limit_bytes=VMEM_LIMIT),
        name="nsa_attention",
    )(qn, qr, cmp_kv, cmp_kv, ksa, vs, kw, vw, small, gbias, imp_t)


def _fox_body(q_ref, k_ref, v_ref, cq_ref, ck_ref, o_ref, m_ref, l_ref, acc_ref, *, tq, tk):
    i = pl.program_id(2)
    q0 = i * tq
    qp = q_ref[...]
    low = _lanes((tq, LANE)) < HEAD_DIM
    causal_rows = q0 + _rows((tq, tk))
    n_full = lax.div(q0, tk)
    outs = []
    for e in range(2):
        qe = jnp.where(low if e == 0 else jnp.logical_not(low), qp, jnp.zeros_like(qp))
        cq = cq_ref[0, 0, :, e:e + 1]
        m_ref[...] = jnp.full((tq, 1), NEG_INF, F32)
        l_ref[...] = jnp.zeros((tq, 1), F32)
        acc_ref[...] = jnp.zeros((tq, LANE), F32)

        def step(kb, masked, qe=qe, cq=cq, e=e):
            k0 = pl.multiple_of(kb * tk, tk)
            s = _dot_nt(qe, k_ref[pl.ds(k0, tk), :]) + cq - ck_ref[0, 0, e:e + 1, pl.ds(k0, tk)]
            if masked:
                s = jnp.where(k0 + _lanes((tq, tk)) <= causal_rows, s, NEG_INF)
            _softmax_update(s, v_ref[pl.ds(k0, tk), :], m_ref, l_ref, acc_ref)

        def full_step(kb, carry, step=step):
            step(kb, False)
            return carry

        lax.fori_loop(0, n_full, full_step, 0)
        step(n_full, True)
        outs.append(acc_ref[...] / l_ref[...])
    o_ref[...] = jnp.where(low, outs[0], outs[1]).astype(MXU_DTYPE)


def _fox(fq, fk, fv, cq, ck, *, batch, seq, tq, tk):
    n_q = seq // tq
    n_pairs = N_FOX_HEADS // 2
    qmap = lambda b, p, i: (b * n_q + i, p)
    kvmap = lambda b, p, i: (b, p)
    return pl.pallas_call(
        functools.partial(_fox_body, tq=tq, tk=tk),
        grid=(batch, n_pairs, n_q),
        in_specs=[
            pl.BlockSpec((tq, LANE), qmap),
            pl.BlockSpec((seq, LANE), kvmap),
            pl.BlockSpec((seq, LANE), kvmap),
            pl.BlockSpec((1, 1, tq, 2), lambda b, p, i: (b, p, i, 0)),
            pl.BlockSpec((1, 1, 2, seq), lambda b, p, i: (b, p, 0, 0)),
        ],
        out_specs=pl.BlockSpec((tq, LANE), qmap),
        out_shape=jax.ShapeDtypeStruct((batch * seq, N_FOX_HEADS * HEAD_DIM), MXU_DTYPE),
        scratch_shapes=[
            pltpu.VMEM((tq, 1), F32),
            pltpu.VMEM((tq, 1), F32),
            pltpu.VMEM((tq, LANE), F32),
        ],
        compiler_params=pltpu.CompilerParams(
            dimension_semantics=("arbitrary", "arbitrary", "arbitrary"), vmem_limit_bytes=VMEM_LIMIT),
        name="fox_attention",
    )(fq, fk, fv, cq, ck)


def _post_body(on_ref, of_ref, x_ref, wo_ref, gpm_ref, gpl_ref, wu_ref, wd_ref, gpo_ref, o_ref, *, ff_chunk):
    n_nsa = on_ref.shape[1]
    mix = _dot(on_ref[...], wo_ref[0:n_nsa, :]) + _dot(of_ref[...], wo_ref[n_nsa:, :])
    x1 = x_ref[...] + _rms(mix, gpm_ref[...])
    h = _rms(x1, gpl_ref[...]).astype(MXU_DTYPE)
    d_ff = wu_ref.shape[1]
    acc = jnp.zeros(x1.shape, F32)
    for c in range(d_ff // ff_chunk):
        sl = slice(c * ff_chunk, (c + 1) * ff_chunk)
        a = jnp.square(jnp.maximum(_dot(h, wu_ref[:, sl]), 0.0))
        acc = acc + _dot(a.astype(MXU_DTYPE), wd_ref[sl, :])
    o_ref[...] = x1 + _rms(acc, gpo_ref[...])


def _post(o_nsa, o_fox, xf, wo, g_post_mix, g_pre_mlp, wu, wd, g_post_mlp, *, tm, ff_chunk):
    m, d = xf.shape
    d_ff = wu.shape[1]
    row = lambda i: (i, 0)
    fixed = lambda i: (0, 0)
    resident = functools.partial(pl.BlockSpec, index_map=fixed, pipeline_mode=pl.Buffered(1))
    return pl.pallas_call(
        functools.partial(_post_body, ff_chunk=ff_chunk),
        grid=(m // tm,),
        in_specs=[
            pl.BlockSpec((tm, o_nsa.shape[1]), row),
            pl.BlockSpec((tm, o_fox.shape[1]), row),
            pl.BlockSpec((tm, d), row),
            resident((wo.shape[0], d)),
            pl.BlockSpec((1, d), fixed),
            pl.BlockSpec((1, d), fixed),
            resident((d, d_ff)),
            resident((d_ff, d)),
            pl.BlockSpec((1, d), fixed),
        ],
        out_specs=pl.BlockSpec((tm, d), row),
        out_shape=jax.ShapeDtypeStruct((m, d), F32),
        compiler_params=pltpu.CompilerParams(
            dimension_semantics=("arbitrary",), vmem_limit_bytes=VMEM_LIMIT),
        name="outproj_mlp",
    )(o_nsa, o_fox, xf, wo, g_post_mix, g_pre_mlp, wu, wd, g_post_mlp)


def _rope_tables(seq):
    half = ROPE_DIMS // 2
    inv_freq = ROPE_THETA ** (-jnp.arange(half, dtype=F32) * 2.0 / ROPE_DIMS)
    ang = jnp.arange(seq).astype(F32)[:, None] * inv_freq[None, :]
    cos, sin = jnp.cos(ang), jnp.sin(ang)
    pad = jnp.zeros((seq, LANE - ROPE_DIMS), F32)
    zero = jnp.zeros((seq, half), F32)
    c = jnp.concatenate([cos, cos, pad + 1.0], axis=-1)
    a = jnp.concatenate([-sin, zero, pad], axis=-1)
    b = jnp.concatenate([zero, sin, pad], axis=-1)
    return c, a, b


def _importance_map_t(seq, n_chunk):
    n_cmp = (seq - CMP_LEN) // CMP_STRIDE + 1
    n_sel = seq // SEL_BLOCK
    cs = np.arange(n_cmp)[None, :] * CMP_STRIDE
    ss = np.arange(n_sel)[:, None] * SEL_BLOCK
    overlap = np.clip(np.minimum(cs + CMP_LEN, ss + SEL_BLOCK) - np.maximum(cs, ss), 0, None) / CMP_LEN
    out = np.zeros((n_sel, n_chunk), np.float32)
    out[:, :n_cmp] = overlap
    return jnp.asarray(out, dtype=MXU_DTYPE)


def kernel(x, w_in, b_nsa_gate, b_forget, cmp_pos_k, cmp_w1_k, cmp_b1_k, cmp_w2_k, cmp_b2_k, cmp_pos_v, cmp_w1_v, cmp_b1_v, cmp_w2_v, cmp_b2_v, w_out, w_up, w_down, g_pre_mix, g_post_mix, g_pre_mlp, g_post_mlp):
    batch, seq, d_model = x.shape
    depth = w_in.shape[0]
    m = batch * seq
    n_chunk = seq // CMP_STRIDE
    n_sel = seq // SEL_BLOCK
    tm = min(512, seq)
    nsa_tq, nsa_tk = LANE, min(256, seq)
    fox_t = min(512, seq)
    assert seq % tm == 0 and seq % fox_t == 0 and seq % nsa_tk == 0
    assert n_sel <= HEAD_DIM and n_sel % 8 == 0 and seq >= WINDOW + nsa_tq
    assert x.dtype == F32

    rope_c, rope_a, rope_b = _rope_tables(seq)
    imp_t = _importance_map_t(seq, n_chunk)
    cs_chunk = min(256, seq)
    tril = jnp.asarray(np.tril(np.ones((cs_chunk, cs_chunk), np.float32)), dtype=MXU_DTYPE)

    src = _projection_columns()
    keep = jnp.asarray(src >= 0)
    wp = jnp.where(keep[None, None, :], jnp.take(w_in, jnp.asarray(np.maximum(src, 0)), axis=2), 0.0).astype(MXU_DTYPE)

    gate_bias = jnp.zeros((depth, 1, N_NSA_KV * LANE), F32)
    fbias = jnp.zeros((depth, 1, LANE), F32)
    n_gate = NSA_GROUP * N_BRANCH
    for g in range(N_NSA_KV):
        gate_bias = gate_bias.at[:, 0, LANE * g:LANE * g + n_gate].set(b_nsa_gate[:, n_gate * g:n_gate * (g + 1)])
    fbias = fbias.at[:, 0, _FF_LANE:_FF_LANE + N_FOX_HEADS].set(b_forget)

    half = CMP_STRIDE * HEAD_DIM
    cpos = jnp.stack([cmp_pos_k.reshape(depth, 2, half), cmp_pos_v.reshape(depth, 2, half)], axis=1)
    cw1 = jnp.stack([cmp_w1_k, cmp_w1_v], axis=1).astype(MXU_DTYPE)
    cb1 = jnp.stack([cmp_b1_k, cmp_b1_v], axis=1)[:, :, None, :]
    zw = jnp.zeros_like(cmp_w2_k)
    cw2 = jnp.stack([jnp.concatenate([cmp_w2_k, zw], axis=-1),
                     jnp.concatenate([cmp_w2_v, cmp_w2_v], axis=-1)], axis=1).astype(MXU_DTYPE)
    zb = jnp.zeros_like(cmp_b2_k)
    cb2 = jnp.stack([jnp.concatenate([cmp_b2_k, zb], axis=-1),
                     jnp.concatenate([cmp_b2_v, cmp_b2_v], axis=-1)], axis=1)[:, :, None, :]

    wo = w_out.astype(MXU_DTYPE)
    wu = w_up.astype(MXU_DTYPE)
    wd = w_down.astype(MXU_DTYPE)

    xf = x.reshape(m, d_model)
    n_pairs = N_FOX_HEADS // 2
    for l in range(depth):
        qn, qr, kvc, ksa, vs, kw, vw, fq, fk, fv, small = _inproj(
            xf, g_pre_mix[l][None, :], wp[l], rope_c, rope_a, rope_b, seq=seq, tm=tm)

        cum = _forget_cumsum(small, fbias[l], tril, batch=batch, seq=seq)
        cum = cum[:, _FF_LANE:_FF_LANE + N_FOX_HEADS].reshape(batch, seq, n_pairs, 2)
        cq = cum.transpose(0, 2, 1, 3)
        ck = cum.transpose(0, 2, 3, 1)

        cmp_kv = _compress(kvc.reshape(4, batch, n_chunk, half), cpos[l], cw1[l], cb1[l], cw2[l], cb2[l],
                           batch=batch, n_chunk=n_chunk)
        o_nsa = _nsa(qn, qr, cmp_kv, ksa, vs, kw, vw, small, gate_bias[l], imp_t,
                     batch=batch, seq=seq, tq=nsa_tq, tk=nsa_tk)
        o_fox = _fox(fq, fk, fv, cq, ck, batch=batch, seq=seq, tq=fox_t, tk=fox_t)
        xf = _post(o_nsa, o_fox, xf, wo[l], g_post_mix[l][None, :], g_pre_mlp[l][None, :],
                   wu[l], wd[l], g_post_mlp[l][None, :], tm=tm, ff_chunk=1024)
    return xf.reshape(batch, seq, d_model)
```

```python
import functools

import numpy as np
import jax
import jax.numpy as jnp
from jax import lax
from jax.experimental import pallas as pl
from jax.experimental.pallas import tpu as pltpu

HEAD_DIM = 64
N_NSA_HEADS = 8
N_NSA_KV = 2
NSA_GROUP = N_NSA_HEADS // N_NSA_KV
N_FOX_HEADS = 8
N_BRANCH = 3
CMP_LEN = 32
CMP_STRIDE = 16
CMP_HIDDEN = 256
SEL_BLOCK = 64
SEL_SHIFT = 6
SEL_TOPK = 16
WINDOW = 512
ROPE_THETA = 500000.0
ROPE_DIMS = HEAD_DIM // 4
NORM_EPS = 1e-6
NEG_INF = -1e30
FORCED_SCORE = 1e4
SCALE = HEAD_DIM ** -0.5

LANE = 128
MXU_DTYPE = jnp.bfloat16
F32 = jnp.float32
VMEM_LIMIT = 56 * 1024 * 1024

_SRC_NQ, _SRC_KC, _SRC_VC, _SRC_KSEL, _SRC_VSEL, _SRC_KWIN, _SRC_VWIN = 0, 512, 640, 768, 896, 1024, 1152
_SRC_GATE, _SRC_FQ, _SRC_FK, _SRC_FV, _SRC_FF = 1280, 1304, 1816, 2328, 2840
_Q0, _KVC0, _KS0, _VS0, _KW0, _VW0, _FQ0, _FK0, _FV0, _SM0, _NP = (
    0, 1024, 1536, 1792, 2048, 2304, 2560, 3072, 3584, 4096, 4352)
_FF_LANE = 16


def _projection_columns():
    src = np.full((_NP,), -1, np.int32)
    d = np.arange(HEAD_DIM)
    for h in range(N_NSA_HEADS):
        src[_Q0 + LANE * h + d] = _SRC_NQ + HEAD_DIM * h + d
    for i in range(4):
        src[_KVC0 + LANE * i + d] = _SRC_KC + HEAD_DIM * i + d
    for g in range(N_NSA_KV):
        src[_KS0 + LANE * g + d] = _SRC_KSEL + HEAD_DIM * g + d
        src[_KW0 + LANE * g + d] = _SRC_KWIN + HEAD_DIM * g + d
        for half in range(2):
            src[_VS0 + LANE * g + HEAD_DIM * half + d] = _SRC_VSEL + HEAD_DIM * g + d
            src[_VW0 + LANE * g + HEAD_DIM * half + d] = _SRC_VWIN + HEAD_DIM * g + d
        gates = np.arange(NSA_GROUP * N_BRANCH)
        src[_SM0 + LANE * g + gates] = _SRC_GATE + NSA_GROUP * N_BRANCH * g + gates
        src[_SM0 + LANE * g + _FF_LANE + np.arange(N_FOX_HEADS)] = _SRC_FF + np.arange(N_FOX_HEADS)
    fox = np.arange(N_FOX_HEADS * HEAD_DIM)
    src[_FQ0 + fox] = _SRC_FQ + fox
    src[_FK0 + fox] = _SRC_FK + fox
    src[_FV0 + fox] = _SRC_FV + fox
    return src


def _rms(x, g):
    ms = jnp.mean(x * x, axis=-1, keepdims=True)
    return x * lax.rsqrt(ms + NORM_EPS) * g


def _dot(a, b):
    return jnp.dot(a, b, preferred_element_type=F32)


def _dot_nt(a, b):
    return lax.dot_general(a, b, (((1,), (1,)), ((), ())), preferred_element_type=F32)


def _split3(x):
    hi = x.astype(MXU_DTYPE)
    r1 = x - hi.astype(F32)
    mid = r1.astype(MXU_DTYPE)
    lo = (r1 - mid.astype(F32)).astype(MXU_DTYPE)
    return hi, mid, lo


def _rows(shape):
    return lax.broadcasted_iota(jnp.int32, shape, 0)


def _lanes(shape):
    return lax.broadcasted_iota(jnp.int32, shape, 1)


def _inproj_body(x_ref, g_ref, w_ref, rc_ref, ra_ref, rb_ref,
                 qn_ref, qr_ref, kvc_ref, ksa_ref, vs_ref, kw_ref, vw_ref,
                 fq_ref, fk_ref, fv_ref, small_ref, *, tm, n_s_tiles):
    h = _rms(x_ref[...], g_ref[...]).astype(MXU_DTYPE)
    rc, ra, rb = rc_ref[...], ra_ref[...], rb_ref[...]

    def mm(lo, n):
        return _dot(h, w_ref[:, lo:lo + n])

    def rope(v):
        return v * rc + pltpu.roll(v, LANE - ROPE_DIMS // 2, 1) * ra + pltpu.roll(v, ROPE_DIMS // 2, 1) * rb

    q = mm(_Q0, N_NSA_HEADS * LANE) * SCALE
    qn_ref[...] = q.astype(MXU_DTYPE)
    for hh in range(N_NSA_HEADS):
        sl = slice(LANE * hh, LANE * (hh + 1))
        qr_ref[:, sl] = rope(q[:, sl]).astype(MXU_DTYPE)

    kvc = mm(_KVC0, 4 * LANE)
    for i in range(4):
        kvc_ref[i] = kvc[:, LANE * i:LANE * i + HEAD_DIM]

    pos = (pl.program_id(0) % n_s_tiles) * tm + _rows((tm, LANE))
    onehot = ((_lanes((tm, LANE)) - HEAD_DIM) == (pos >> SEL_SHIFT)).astype(F32)
    ks = mm(_KS0, N_NSA_KV * LANE)
    kw = mm(_KW0, N_NSA_KV * LANE)
    for g in range(N_NSA_KV):
        sl = slice(LANE * g, LANE * (g + 1))
        ksa_ref[:, sl] = (rope(ks[:, sl]) + onehot).astype(MXU_DTYPE)
        kw_ref[:, sl] = rope(kw[:, sl]).astype(MXU_DTYPE)
    vs_ref[...] = mm(_VS0, N_NSA_KV * LANE).astype(MXU_DTYPE)
    vw_ref[...] = mm(_VW0, N_NSA_KV * LANE).astype(MXU_DTYPE)

    nf = N_FOX_HEADS * HEAD_DIM
    fq_ref[...] = (mm(_FQ0, nf) * SCALE).astype(MXU_DTYPE)
    fk_ref[...] = mm(_FK0, nf).astype(MXU_DTYPE)
    fv_ref[...] = mm(_FV0, nf).astype(MXU_DTYPE)
    small_ref[...] = mm(_SM0, N_NSA_KV * LANE)


def _inproj(xf, g, wp, rope_c, rope_a, rope_b, *, seq, tm):
    m, d = xf.shape
    n_s_tiles = seq // tm
    nf = N_FOX_HEADS * HEAD_DIM
    row = lambda i: (i, 0)
    fixed = lambda i: (0, 0)
    tab = lambda i: (i % n_s_tiles, 0)
    out_shape = (
        jax.ShapeDtypeStruct((m, N_NSA_HEADS * LANE), MXU_DTYPE),
        jax.ShapeDtypeStruct((m, N_NSA_HEADS * LANE), MXU_DTYPE),
        jax.ShapeDtypeStruct((4, m, HEAD_DIM), F32),
        jax.ShapeDtypeStruct((m, N_NSA_KV * LANE), MXU_DTYPE),
        jax.ShapeDtypeStruct((m, N_NSA_KV * LANE), MXU_DTYPE),
        jax.ShapeDtypeStruct((m, N_NSA_KV * LANE), MXU_DTYPE),
        jax.ShapeDtypeStruct((m, N_NSA_KV * LANE), MXU_DTYPE),
        jax.ShapeDtypeStruct((m, nf), MXU_DTYPE),
        jax.ShapeDtypeStruct((m, nf), MXU_DTYPE),
        jax.ShapeDtypeStruct((m, nf), MXU_DTYPE),
        jax.ShapeDtypeStruct((m, N_NSA_KV * LANE), F32),
    )
    out_specs = (
        pl.BlockSpec((tm, N_NSA_HEADS * LANE), row),
        pl.BlockSpec((tm, N_NSA_HEADS * LANE), row),
        pl.BlockSpec((4, tm, HEAD_DIM), lambda i: (0, i, 0)),
        pl.BlockSpec((tm, N_NSA_KV * LANE), row),
        pl.BlockSpec((tm, N_NSA_KV * LANE), row),
        pl.BlockSpec((tm, N_NSA_KV * LANE), row),
        pl.BlockSpec((tm, N_NSA_KV * LANE), row),
        pl.BlockSpec((tm, nf), row),
        pl.BlockSpec((tm, nf), row),
        pl.BlockSpec((tm, nf), row),
        pl.BlockSpec((tm, N_NSA_KV * LANE), row),
    )
    return pl.pallas_call(
        functools.partial(_inproj_body, tm=tm, n_s_tiles=n_s_tiles),
        grid=(m // tm,),
        in_specs=[
            pl.BlockSpec((tm, d), row),
            pl.BlockSpec((1, d), fixed),
            pl.BlockSpec((d, _NP), fixed),
            pl.BlockSpec((tm, LANE), tab),
            pl.BlockSpec((tm, LANE), tab),
            pl.BlockSpec((tm, LANE), tab),
        ],
        out_specs=out_specs,
        out_shape=out_shape,
        compiler_params=pltpu.CompilerParams(
            dimension_semantics=("arbitrary",), vmem_limit_bytes=VMEM_LIMIT),
        name="inproj",
    )(xf, g, wp, rope_c, rope_a, rope_b)


def _cumsum_body(sm_ref, bias_ref, tril_ref, out_ref, *, seq, chunk):
    carry = jnp.zeros((1, LANE), F32)
    tril = tril_ref[...]
    for c in range(seq // chunk):
        sl = slice(c * chunk, (c + 1) * chunk)
        z = sm_ref[sl, :] + bias_ref[...]
        logf = jnp.minimum(z, 0.0) - jnp.log1p(jnp.exp(-jnp.abs(z)))
        hi, mid, lo = _split3(logf)
        cs = (_dot(tril, hi) + _dot(tril, mid)) + _dot(tril, lo) + carry
        out_ref[sl, :] = cs
        carry = cs[chunk - 1:chunk, :]


def _forget_cumsum(small, fbias, tril, *, batch, seq):
    chunk = tril.shape[0]
    return pl.pallas_call(
        functools.partial(_cumsum_body, seq=seq, chunk=chunk),
        grid=(batch,),
        in_specs=[
            pl.BlockSpec((seq, LANE), lambda b: (b, 0)),
            pl.BlockSpec((1, LANE), lambda b: (0, 0)),
            pl.BlockSpec((chunk, chunk), lambda b: (0, 0)),
        ],
        out_specs=pl.BlockSpec((seq, LANE), lambda b: (b, 0)),
        out_shape=jax.ShapeDtypeStruct((batch * seq, LANE), F32),
        compiler_params=pltpu.CompilerParams(
            dimension_semantics=("arbitrary",), vmem_limit_bytes=VMEM_LIMIT),
        name="forget_cumsum",
    )(small, fbias, tril)


def _compress_body(x_ref, pos_ref, w1_ref, b1_ref, w2_ref, b2_ref, o_ref, *, n_chunk):
    half = CMP_STRIDE * HEAD_DIM
    x = x_ref[0, 0]
    first = _dot((x + pos_ref[0, 0:1, :]).astype(MXU_DTYPE), w1_ref[0, 0:half, :])
    second = _dot((x + pos_ref[0, 1:2, :]).astype(MXU_DTYPE), w1_ref[0, half:2 * half, :])
    hid = first + pltpu.roll(second, n_chunk - 1, 0) + b1_ref[0]
    act = jax.nn.gelu(hid)
    o_ref[0, 0] = (_dot(act.astype(MXU_DTYPE), w2_ref[0]) + b2_ref[0]).astype(MXU_DTYPE)


def _compress(kvc, pos, w1, b1, w2p, b2p, *, batch, n_chunk):
    half = CMP_STRIDE * HEAD_DIM
    kv = lambda i, b: (i // N_NSA_KV, 0, 0)
    return pl.pallas_call(
        functools.partial(_compress_body, n_chunk=n_chunk),
        grid=(2 * N_NSA_KV, batch),
        in_specs=[
            pl.BlockSpec((1, 1, n_chunk, half), lambda i, b: (i, b, 0, 0)),
            pl.BlockSpec((1, 2, half), kv),
            pl.BlockSpec((1, 2 * half, CMP_HIDDEN), kv),
            pl.BlockSpec((1, 1, CMP_HIDDEN), kv),
            pl.BlockSpec((1, CMP_HIDDEN, LANE), kv),
            pl.BlockSpec((1, 1, LANE), kv),
        ],
        out_specs=pl.BlockSpec((1, 1, n_chunk, LANE), lambda i, b: (i, b, 0, 0)),
        out_shape=jax.ShapeDtypeStruct((2 * N_NSA_KV, batch, n_chunk, LANE), MXU_DTYPE),
        compiler_params=pltpu.CompilerParams(
            dimension_semantics=("arbitrary", "arbitrary"), vmem_limit_bytes=VMEM_LIMIT),
        name="compress",
    )(kvc, pos, w1, b1, w2p, b2p)


def _lane_max(s):
    out = s[:, 0:LANE]
    for c in range(1, s.shape[1] // LANE):
        out = jnp.maximum(out, s[:, LANE * c:LANE * (c + 1)])
    return out


def _row_max_lanes(mrun):
    return jnp.broadcast_to(jnp.max(mrun, axis=-1, keepdims=True), mrun.shape)


def _exp_tiles(s, mb):
    n = s.shape[1] // LANE
    return jnp.concatenate(
        [jnp.exp(s[:, LANE * c:LANE * (c + 1)] - mb).astype(MXU_DTYPE) for c in range(n)], axis=1)


def _values_with_ones(v, ones_low):
    low = _lanes(v.shape) < HEAD_DIM
    ones = jnp.ones_like(v)
    return jnp.where(low, ones, v) if ones_low else jnp.where(low, v, ones)


def _normalize(acc):
    return acc / pltpu.roll(acc, HEAD_DIM, 1)


def _nsa_body(qn_ref, qr_ref, kc_ref, vc_ref, ksa_ref, vs_ref, kw_ref, vw_ref, sm_ref, gb_ref, impt_ref,
              o_ref, s_ref, mrun_ref, acc_ref, *, tq, tk, n_sel, top_k):
    q0 = pl.program_id(2) * tq
    r = NSA_GROUP * tq
    n_cmp = kc_ref.shape[2]

    q4w = jnp.concatenate([qr_ref[:, LANE * h:LANE * (h + 1)] for h in range(NSA_GROUP)], axis=0)
    span = WINDOW + tq
    ks0 = pl.multiple_of(jnp.maximum(q0 - WINDOW, 0), tq)
    sw = _dot_nt(q4w, kw_ref[pl.ds(ks0, span), :])
    rowq_w = q0 + (_rows((r, span)) & (tq - 1))
    kpos = ks0 + _lanes((r, span))
    mw = (kpos <= rowq_w) & (kpos > rowq_w - WINDOW)
    swm = jnp.where(mw, sw, NEG_INF)
    pw = _exp_tiles(swm, _row_max_lanes(_lane_max(swm)))
    o_win = _normalize(_dot(pw, _values_with_ones(vw_ref[pl.ds(ks0, span), :], False)))

    q4n = jnp.concatenate([qn_ref[:, LANE * h:LANE * (h + 1)] for h in range(NSA_GROUP)], axis=0)
    sc = _dot_nt(q4n, kc_ref[0, 0])
    rowq = q0 + (_rows((r, n_cmp)) & (tq - 1))
    mc = _lanes((r, n_cmp)) * CMP_STRIDE + (CMP_LEN - 1) <= rowq
    scm = jnp.where(mc, sc, NEG_INF)
    e = jnp.exp(scm - jnp.max(scm, axis=-1, keepdims=True))
    pc = jnp.where(mc, e / jnp.sum(e, axis=-1, keepdims=True), 0.0)
    o_cmp = _dot(pc.astype(MXU_DTYPE), vc_ref[0, 0])

    pcs = (pc[0:tq] + pc[tq:2 * tq]) + (pc[2 * tq:3 * tq] + pc[3 * tq:4 * tq])
    impt_w = impt_ref[...]
    hi, mid, lo = _split3(pcs)
    imp_t = (_dot_nt(impt_w, hi) + _dot_nt(impt_w, mid)) + _dot_nt(impt_w, lo)
    jrow = _rows((n_sel, tq))
    qpos = q0 + _lanes((n_sel, tq))
    cur = qpos >> SEL_SHIFT
    valid = jrow * SEL_BLOCK <= qpos
    forced = (jrow == 0) | (jrow == cur) | (jrow == cur - 1)
    score = jnp.where(forced, FORCED_SCORE, jnp.where(valid, imp_t, -1.0))

    sub = 8
    chunks = [score[sub * c:sub * (c + 1), :] for c in range(n_sel // sub)]
    ranks = [jnp.zeros((sub, tq), F32) for _ in chunks]
    later = _rows((sub, tq))
    for jp in range(n_sel):
        cj, rj = divmod(jp, sub)
        rowv = chunks[cj][rj:rj + 1, :]
        for c, sc_c in enumerate(chunks):
            if c > cj:
                beats = rowv >= sc_c
            elif c < cj:
                beats = rowv > sc_c
            else:
                beats = (rowv > sc_c) | ((rowv == sc_c) & (later > rj))
            ranks[c] = ranks[c] + jnp.where(beats, 1.0, 0.0)
    sel_t = jnp.where(jnp.concatenate(ranks, axis=0) < float(top_k), 1.0, 0.0)
    sel_pad = jnp.concatenate([sel_t, jnp.zeros((LANE - n_sel, tq), F32)], axis=0)
    sel_q = pltpu.roll(sel_pad.T, HEAD_DIM, 1)
    sel_bias = jnp.where(_lanes((tq, LANE)) >= HEAD_DIM, (sel_q - 1.0) * (-NEG_INF), 0.0)

    q4r = jnp.concatenate(
        [(qr_ref[:, LANE * h:LANE * (h + 1)].astype(F32) + sel_bias).astype(MXU_DTYPE)
         for h in range(NSA_GROUP)], axis=0)

    rowq_k = q0 + (_rows((r, tk)) & (tq - 1))
    n_full = lax.div(q0, tk)

    def score_step(kb, masked):
        k0 = pl.multiple_of(kb * tk, tk)
        s = _dot_nt(q4r, ksa_ref[pl.ds(k0, tk), :])
        if masked:
            s = jnp.where(k0 + _lanes((r, tk)) <= rowq_k, s, NEG_INF)
        s_ref[kb] = s
        mrun_ref[...] = jnp.maximum(mrun_ref[...], _lane_max(s))

    def full_score_step(kb, carry):
        score_step(kb, False)
        return carry

    mrun_ref[...] = jnp.full((r, LANE), NEG_INF, F32)
    lax.fori_loop(0, n_full, full_score_step, 0)
    score_step(n_full, True)
    mrun_ref[...] = _row_max_lanes(mrun_ref[...])
    acc_ref[...] = jnp.zeros((r, LANE), F32)

    def value_step(kb, carry):
        k0 = pl.multiple_of(kb * tk, tk)
        p = _exp_tiles(s_ref[kb], mrun_ref[...])
        acc_ref[...] += _dot(p, _values_with_ones(vs_ref[pl.ds(k0, tk), :], False))
        return carry

    lax.fori_loop(0, n_full + 1, value_step, 0)
    o_sel = _normalize(acc_ref[...])

    gate = jax.nn.sigmoid(sm_ref[...] + gb_ref[...])
    heads = []
    for h in range(NSA_GROUP):
        rs = slice(h * tq, (h + 1) * tq)
        c = N_BRANCH * h
        heads.append(gate[:, c:c + 1] * o_cmp[rs] + gate[:, c + 1:c + 2] * o_sel[rs]
                     + gate[:, c + 2:c + 3] * o_win[rs])
    low = _lanes((tq, LANE)) < HEAD_DIM
    for p in range(NSA_GROUP // 2):
        pair = jnp.where(low, heads[2 * p], pltpu.roll(heads[2 * p + 1], HEAD_DIM, 1))
        o_ref[:, LANE * p:LANE * (p + 1)] = pair.astype(MXU_DTYPE)


def _nsa(qn, qr, cmp_kv, ksa, vs, kw, vw, small, gbias, imp_t, *, batch, seq, tq, tk):
    n_q = seq // tq
    n_cmp = cmp_kv.shape[2]
    n_sel = seq // SEL_BLOCK
    top_k = min(SEL_TOPK, n_sel)
    gw = NSA_GROUP * LANE
    qmap = lambda b, g, i: (b * n_q + i, g)
    kvmap = lambda b, g, i: (b, g)
    r = NSA_GROUP * tq
    return pl.pallas_call(
        functools.partial(_nsa_body, tq=tq, tk=tk, n_sel=n_sel, top_k=top_k),
        grid=(batch, N_NSA_KV, n_q),
        in_specs=[
            pl.BlockSpec((tq, gw), qmap),
            pl.BlockSpec((tq, gw), qmap),
            pl.BlockSpec((1, 1, n_cmp, LANE), lambda b, g, i: (g, b, 0, 0)),
            pl.BlockSpec((1, 1, n_cmp, LANE), lambda b, g, i: (N_NSA_KV + g, b, 0, 0)),
            pl.BlockSpec((seq, LANE), kvmap),
            pl.BlockSpec((seq, LANE), kvmap),
            pl.BlockSpec((seq, LANE), kvmap),
            pl.BlockSpec((seq, LANE), kvmap),
            pl.BlockSpec((tq, LANE), qmap),
            pl.BlockSpec((1, LANE), lambda b, g, i: (0, g)),
            pl.BlockSpec((n_sel, n_cmp), lambda b, g, i: (0, 0)),
        ],
        out_specs=pl.BlockSpec((tq, NSA_GROUP * HEAD_DIM), qmap),
        out_shape=jax.ShapeDtypeStruct((batch * seq, N_NSA_HEADS * HEAD_DIM), MXU_DTYPE),
        scratch_shapes=[
            pltpu.VMEM((seq // tk, r, tk), F32),
            pltpu.VMEM((r, LANE), F32),
            pltpu.VMEM((r, LANE), F32),
        ],
        compiler_params=pltpu.CompilerParams(
            dimension_semantics=("arbitrary", "arbitrary", "arbitrary"), vmem_limit_bytes=VMEM_LIMIT),
        name="nsa_attention",
    )(qn, qr, cmp_kv, cmp_kv, ksa, vs, kw, vw, small, gbias, imp_t)


def _fox_body(q_ref, k_ref, v_ref, cq_ref, ck_ref, o_ref, s_ref, mrun_ref, acc_ref, *, tq, tk):
    i = pl.program_id(2)
    q0 = i * tq
    qp = q_ref[...]
    low = _lanes((tq, LANE)) < HEAD_DIM
    causal_rows = q0 + _rows((tq, tk))
    n_full = lax.div(q0, tk)
    qs, cqs = [], []
    for e in range(2):
        qs.append(jnp.where(low if e == 0 else jnp.logical_not(low), qp, jnp.zeros_like(qp)))
        cq = jnp.broadcast_to(cq_ref[0, 0, :, e:e + 1], (tq, LANE))
        cqs.append(jnp.concatenate([cq] * (tk // LANE), axis=1))

    def score_step(kb, masked):
        k0 = pl.multiple_of(kb * tk, tk)
        k = k_ref[pl.ds(k0, tk), :]
        for e in range(2):
            s = _dot_nt(qs[e], k) + cqs[e] - ck_ref[0, 0, e:e + 1, pl.ds(k0, tk)]
            if masked:
                s = jnp.where(k0 + _lanes((tq, tk)) <= causal_rows, s, NEG_INF)
            s_ref[e, kb] = s
            mrun_ref[e] = jnp.maximum(mrun_ref[e], _lane_max(s))

    def full_score_step(kb, carry):
        score_step(kb, False)
        return carry

    mrun_ref[...] = jnp.full((2, tq, LANE), NEG_INF, F32)
    lax.fori_loop(0, n_full, full_score_step, 0)
    score_step(n_full, True)
    for e in range(2):
        mrun_ref[e] = _row_max_lanes(mrun_ref[e])
    acc_ref[...] = jnp.zeros((2, tq, LANE), F32)

    def value_step(kb, carry):
        k0 = pl.multiple_of(kb * tk, tk)
        v = v_ref[pl.ds(k0, tk), :]
        for e in range(2):
            p = _exp_tiles(s_ref[e, kb], mrun_ref[e])
            acc_ref[e] += _dot(p, _values_with_ones(v, e == 1))
        return carry

    lax.fori_loop(0, n_full + 1, value_step, 0)
    o_ref[...] = jnp.where(low, _normalize(acc_ref[0]), _normalize(acc_ref[1])).astype(MXU_DTYPE)


def _fox(fq, fk, fv, cq, ck, *, batch, seq, tq, tk):
    n_q = seq // tq
    n_pairs = N_FOX_HEADS // 2
    qmap = lambda b, p, i: (b * n_q + i, p)
    kvmap = lambda b, p, i: (b, p)
    return pl.pallas_call(
        functools.partial(_fox_body, tq=tq, tk=tk),
        grid=(batch, n_pairs, n_q),
        in_specs=[
            pl.BlockSpec((tq, LANE), qmap),
            pl.BlockSpec((seq, LANE), kvmap),
            pl.BlockSpec((seq, LANE), kvmap),
            pl.BlockSpec((1, 1, tq, 2), lambda b, p, i: (b, p, i, 0)),
            pl.BlockSpec((1, 1, 2, seq), lambda b, p, i: (b, p, 0, 0)),
        ],
        out_specs=pl.BlockSpec((tq, LANE), qmap),
        out_shape=jax.ShapeDtypeStruct((batch * seq, N_FOX_HEADS * HEAD_DIM), MXU_DTYPE),
        scratch_shapes=[
            pltpu.VMEM((2, seq // tk, tq, tk), F32),
            pltpu.VMEM((2, tq, LANE), F32),
            pltpu.VMEM((2, tq, LANE), F32),
        ],
        compiler_params=pltpu.CompilerParams(
            dimension_semantics=("arbitrary", "arbitrary", "arbitrary"), vmem_limit_bytes=VMEM_LIMIT),
        name="fox_attention",
    )(fq, fk, fv, cq, ck)


def _post_body(on_ref, of_ref, x_ref, wo_ref, gpm_ref, gpl_ref, wu_ref, wd_ref, gpo_ref, o_ref, *, ff_chunk):
    n_nsa = on_ref.shape[1]
    mix = _dot(on_ref[...], wo_ref[0:n_nsa, :]) + _dot(of_ref[...], wo_ref[n_nsa:, :])
    x1 = x_ref[...] + _rms(mix, gpm_ref[...])
    h = _rms(x1, gpl_ref[...]).astype(MXU_DTYPE)
    d_ff = wu_ref.shape[1]
    acc = jnp.zeros(x1.shape, F32)
    for c in range(d_ff // ff_chunk):
        sl = slice(c * ff_chunk, (c + 1) * ff_chunk)
        a = jnp.square(jnp.maximum(_dot(h, wu_ref[:, sl]), 0.0))
        acc = acc + _dot(a.astype(MXU_DTYPE), wd_ref[sl, :])
    o_ref[...] = x1 + _rms(acc, gpo_ref[...])


def _post(o_nsa, o_fox, xf, wo, g_post_mix, g_pre_mlp, wu, wd, g_post_mlp, *, tm, ff_chunk):
    m, d = xf.shape
    d_ff = wu.shape[1]
    row = lambda i: (i, 0)
    fixed = lambda i: (0, 0)
    resident = functools.partial(pl.BlockSpec, index_map=fixed, pipeline_mode=pl.Buffered(1))
    return pl.pallas_call(
        functools.partial(_post_body, ff_chunk=ff_chunk),
        grid=(m // tm,),
        in_specs=[
            pl.BlockSpec((tm, o_nsa.shape[1]), row),
            pl.BlockSpec((tm, o_fox.shape[1]), row),
            pl.BlockSpec((tm, d), row),
            resident((wo.shape[0], d)),
            pl.BlockSpec((1, d), fixed),
            pl.BlockSpec((1, d), fixed),
            resident((d, d_ff)),
            resident((d_ff, d)),
            pl.BlockSpec((1, d), fixed),
        ],
        out_specs=pl.BlockSpec((tm, d), row),
        out_shape=jax.ShapeDtypeStruct((m, d), F32),
        compiler_params=pltpu.CompilerParams(
            dimension_semantics=("arbitrary",), vmem_limit_bytes=VMEM_LIMIT),
        name="outproj_mlp",
    )(o_nsa, o_fox, xf, wo, g_post_mix, g_pre_mlp, wu, wd, g_post_mlp)


def _rope_tables(seq):
    half = ROPE_DIMS // 2
    inv_freq = ROPE_THETA ** (-jnp.arange(half, dtype=F32) * 2.0 / ROPE_DIMS)
    ang = jnp.arange(seq).astype(F32)[:, None] * inv_freq[None, :]
    cos, sin = jnp.cos(ang), jnp.sin(ang)
    pad = jnp.zeros((seq, LANE - ROPE_DIMS), F32)
    zero = jnp.zeros((seq, half), F32)
    c = jnp.concatenate([cos, cos, pad + 1.0], axis=-1)
    a = jnp.concatenate([-sin, zero, pad], axis=-1)
    b = jnp.concatenate([zero, sin, pad], axis=-1)
    return c, a, b


def _importance_map_t(seq, n_chunk):
    n_cmp = (seq - CMP_LEN) // CMP_STRIDE + 1
    n_sel = seq // SEL_BLOCK
    cs = np.arange(n_cmp)[None, :] * CMP_STRIDE
    ss = np.arange(n_sel)[:, None] * SEL_BLOCK
    overlap = np.clip(np.minimum(cs + CMP_LEN, ss + SEL_BLOCK) - np.maximum(cs, ss), 0, None) / CMP_LEN
    out = np.zeros((n_sel, n_chunk), np.float32)
    out[:, :n_cmp] = overlap
    return jnp.asarray(out, dtype=MXU_DTYPE)


def kernel(x, w_in, b_nsa_gate, b_forget, cmp_pos_k, cmp_w1_k, cmp_b1_k, cmp_w2_k, cmp_b2_k, cmp_pos_v, cmp_w1_v, cmp_b1_v, cmp_w2_v, cmp_b2_v, w_out, w_up, w_down, g_pre_mix, g_post_mix, g_pre_mlp, g_post_mlp):
    batch, seq, d_model = x.shape
    depth = w_in.shape[0]
    m = batch * seq
    n_chunk = seq // CMP_STRIDE
    n_sel = seq // SEL_BLOCK
    tm = min(512, seq)
    nsa_tq, nsa_tk = 2 * LANE, min(256, seq)
    fox_t = min(512, seq)
    assert seq % tm == 0 and seq % fox_t == 0 and seq % nsa_tk == 0
    assert n_sel <= HEAD_DIM and n_sel % 8 == 0 and seq >= WINDOW + nsa_tq
    assert x.dtype == F32

    rope_c, rope_a, rope_b = _rope_tables(seq)
    imp_t = _importance_map_t(seq, n_chunk)
    cs_chunk = min(256, seq)
    tril = jnp.asarray(np.tril(np.ones((cs_chunk, cs_chunk), np.float32)), dtype=MXU_DTYPE)

    src = _projection_columns()
    keep = jnp.asarray(src >= 0)
    wp = jnp.where(keep[None, None, :], jnp.take(w_in, jnp.asarray(np.maximum(src, 0)), axis=2), 0.0).astype(MXU_DTYPE)

    gate_bias = jnp.zeros((depth, 1, N_NSA_KV * LANE), F32)
    fbias = jnp.zeros((depth, 1, LANE), F32)
    n_gate = NSA_GROUP * N_BRANCH
    for g in range(N_NSA_KV):
        gate_bias = gate_bias.at[:, 0, LANE * g:LANE * g + n_gate].set(b_nsa_gate[:, n_gate * g:n_gate * (g + 1)])
    fbias = fbias.at[:, 0, _FF_LANE:_FF_LANE + N_FOX_HEADS].set(b_forget)

    half = CMP_STRIDE * HEAD_DIM
    cpos = jnp.stack([cmp_pos_k.reshape(depth, 2, half), cmp_pos_v.reshape(depth, 2, half)], axis=1)
    cw1 = jnp.stack([cmp_w1_k, cmp_w1_v], axis=1).astype(MXU_DTYPE)
    cb1 = jnp.stack([cmp_b1_k, cmp_b1_v], axis=1)[:, :, None, :]
    zw = jnp.zeros_like(cmp_w2_k)
    cw2 = jnp.stack([jnp.concatenate([cmp_w2_k, zw], axis=-1),
                     jnp.concatenate([cmp_w2_v, cmp_w2_v], axis=-1)], axis=1).astype(MXU_DTYPE)
    zb = jnp.zeros_like(cmp_b2_k)
    cb2 = jnp.stack([jnp.concatenate([cmp_b2_k, zb], axis=-1),
                     jnp.concatenate([cmp_b2_v, cmp_b2_v], axis=-1)], axis=1)[:, :, None, :]

    wo = w_out.astype(MXU_DTYPE)
    wu = w_up.astype(MXU_DTYPE)
    wd = w_down.astype(MXU_DTYPE)

    xf = x.reshape(m, d_model)
    n_pairs = N_FOX_HEADS // 2
    for l in range(depth):
        qn, qr, kvc, ksa, vs, kw, vw, fq, fk, fv, small = _inproj(
            xf, g_pre_mix[l][None, :], wp[l], rope_c, rope_a, rope_b, seq=seq, tm=tm)

        cum = _forget_cumsum(small, fbias[l], tril, batch=batch, seq=seq)
        cum = cum[:, _FF_LANE:_FF_LANE + N_FOX_HEADS].reshape(batch, seq, n_pairs, 2)
        cq = cum.transpose(0, 2, 1, 3)
        ck = cum.transpose(0, 2, 3, 1)

        cmp_kv = _compress(kvc.reshape(4, batch, n_chunk, half), cpos[l], cw1[l], cb1[l], cw2[l], cb2[l],
                           batch=batch, n_chunk=n_chunk)
        o_nsa = _nsa(qn, qr, cmp_kv, ksa, vs, kw, vw, small, gate_bias[l], imp_t,
                     batch=batch, seq=seq, tq=nsa_tq, tk=nsa_tk)
        o_fox = _fox(fq, fk, fv, cq, ck, batch=batch, seq=seq, tq=fox_t, tk=fox_t)
        xf = _post(o_nsa, o_fox, xf, wo[l], g_post_mix[l][None, :], g_pre_mlp[l][None, :],
                   wu[l], wd[l], g_post_mlp[l][None, :], tm=tm, ff_chunk=1024)
    return xf.reshape(batch, seq, d_model)
```

```python
import functools

import numpy as np
import jax
import jax.numpy as jnp
from jax import lax
from jax.experimental import pallas as pl
from jax.experimental.pallas import tpu as pltpu

HEAD_DIM = 64
N_NSA_HEADS = 8
N_NSA_KV = 2
NSA_GROUP = N_NSA_HEADS // N_NSA_KV
N_FOX_HEADS = 8
N_BRANCH = 3
CMP_LEN = 32
CMP_STRIDE = 16
CMP_SHIFT = 4
CMP_HIDDEN = 256
SEL_BLOCK = 64
SEL_SHIFT = 6
SEL_TOPK = 16
WINDOW = 512
ROPE_THETA = 500000.0
ROPE_DIMS = HEAD_DIM // 4
NORM_EPS = 1e-6
NEG_INF = -1e30
FORCED_SCORE = 1e4
SCALE = HEAD_DIM ** -0.5

LANE = 128
MXU_DTYPE = jnp.bfloat16
F32 = jnp.float32
VMEM_LIMIT = 56 * 1024 * 1024

_SRC_NQ, _SRC_KC, _SRC_VC, _SRC_KSEL, _SRC_VSEL, _SRC_KWIN, _SRC_VWIN = 0, 512, 640, 768, 896, 1024, 1152
_SRC_GATE, _SRC_FQ, _SRC_FK, _SRC_FV, _SRC_FF = 1280, 1304, 1816, 2328, 2840
_Q0, _KVC0, _KS0, _VS0, _KW0, _VW0, _FQ0, _FK0, _FV0, _SM0, _NP = (
    0, 1024, 1536, 1792, 2048, 2304, 2560, 3072, 3584, 4096, 4352)
_FF_LANE = 16


def _projection_columns():
    src = np.full((_NP,), -1, np.int32)
    d = np.arange(HEAD_DIM)
    for h in range(N_NSA_HEADS):
        src[_Q0 + LANE * h + d] = _SRC_NQ + HEAD_DIM * h + d
    for i in range(4):
        src[_KVC0 + LANE * i + d] = _SRC_KC + HEAD_DIM * i + d
    for g in range(N_NSA_KV):
        src[_KS0 + LANE * g + d] = _SRC_KSEL + HEAD_DIM * g + d
        src[_KW0 + LANE * g + d] = _SRC_KWIN + HEAD_DIM * g + d
        for half in range(2):
            src[_VS0 + LANE * g + HEAD_DIM * half + d] = _SRC_VSEL + HEAD_DIM * g + d
            src[_VW0 + LANE * g + HEAD_DIM * half + d] = _SRC_VWIN + HEAD_DIM * g + d
        gates = np.arange(NSA_GROUP * N_BRANCH)
        src[_SM0 + LANE * g + gates] = _SRC_GATE + NSA_GROUP * N_BRANCH * g + gates
        src[_SM0 + LANE * g + _FF_LANE + np.arange(N_FOX_HEADS)] = _SRC_FF + np.arange(N_FOX_HEADS)
    fox = np.arange(N_FOX_HEADS * HEAD_DIM)
    src[_FQ0 + fox] = _SRC_FQ + fox
    src[_FK0 + fox] = _SRC_FK + fox
    src[_FV0 + fox] = _SRC_FV + fox
    return src


def _rms(x, g):
    ms = jnp.mean(x * x, axis=-1, keepdims=True)
    return x * lax.rsqrt(ms + NORM_EPS) * g


def _dot(a, b):
    return jnp.dot(a, b, preferred_element_type=F32)


def _dot_nt(a, b):
    return lax.dot_general(a, b, (((1,), (1,)), ((), ())), preferred_element_type=F32)


def _split3(x):
    hi = x.astype(MXU_DTYPE)
    r1 = x - hi.astype(F32)
    mid = r1.astype(MXU_DTYPE)
    lo = (r1 - mid.astype(F32)).astype(MXU_DTYPE)
    return hi, mid, lo


def _rows(shape):
    return lax.broadcasted_iota(jnp.int32, shape, 0)


def _lanes(shape):
    return lax.broadcasted_iota(jnp.int32, shape, 1)


def _inproj_body(x_ref, g_ref, w_ref, rc_ref, ra_ref, rb_ref,
                 qn_ref, qr_ref, kvc_ref, ksa_ref, vs_ref, kw_ref, vw_ref,
                 fq_ref, fk_ref, fv_ref, small_ref, *, tm, n_s_tiles):
    h = _rms(x_ref[...], g_ref[...]).astype(MXU_DTYPE)
    rc, ra, rb = rc_ref[...], ra_ref[...], rb_ref[...]

    def mm(lo, n):
        return _dot(h, w_ref[:, lo:lo + n])

    def rope(v):
        return v * rc + pltpu.roll(v, LANE - ROPE_DIMS // 2, 1) * ra + pltpu.roll(v, ROPE_DIMS // 2, 1) * rb

    q = mm(_Q0, N_NSA_HEADS * LANE) * SCALE
    qn_ref[...] = q.astype(MXU_DTYPE)
    for hh in range(N_NSA_HEADS):
        sl = slice(LANE * hh, LANE * (hh + 1))
        qr_ref[:, sl] = rope(q[:, sl]).astype(MXU_DTYPE)

    kvc = mm(_KVC0, 4 * LANE)
    for i in range(4):
        kvc_ref[i] = kvc[:, LANE * i:LANE * i + HEAD_DIM]

    pos = (pl.program_id(0) % n_s_tiles) * tm + _rows((tm, LANE))
    onehot = ((_lanes((tm, LANE)) - HEAD_DIM) == (pos >> SEL_SHIFT)).astype(F32)
    ks = mm(_KS0, N_NSA_KV * LANE)
    kw = mm(_KW0, N_NSA_KV * LANE)
    for g in range(N_NSA_KV):
        sl = slice(LANE * g, LANE * (g + 1))
        ksa_ref[:, sl] = (rope(ks[:, sl]) + onehot).astype(MXU_DTYPE)
        kw_ref[:, sl] = rope(kw[:, sl]).astype(MXU_DTYPE)
    vs_ref[...] = mm(_VS0, N_NSA_KV * LANE).astype(MXU_DTYPE)
    vw_ref[...] = mm(_VW0, N_NSA_KV * LANE).astype(MXU_DTYPE)

    nf = N_FOX_HEADS * HEAD_DIM
    fq_ref[...] = (mm(_FQ0, nf) * SCALE).astype(MXU_DTYPE)
    fk_ref[...] = mm(_FK0, nf).astype(MXU_DTYPE)
    fv_ref[...] = mm(_FV0, nf).astype(MXU_DTYPE)
    small_ref[...] = mm(_SM0, N_NSA_KV * LANE)


def _inproj(xf, g, wp, rope_c, rope_a, rope_b, *, seq, tm):
    m, d = xf.shape
    n_s_tiles = seq // tm
    nf = N_FOX_HEADS * HEAD_DIM
    row = lambda i: (i, 0)
    fixed = lambda i: (0, 0)
    tab = lambda i: (i % n_s_tiles, 0)
    out_shape = (
        jax.ShapeDtypeStruct((m, N_NSA_HEADS * LANE), MXU_DTYPE),
        jax.ShapeDtypeStruct((m, N_NSA_HEADS * LANE), MXU_DTYPE),
        jax.ShapeDtypeStruct((4, m, HEAD_DIM), F32),
        jax.ShapeDtypeStruct((m, N_NSA_KV * LANE), MXU_DTYPE),
        jax.ShapeDtypeStruct((m, N_NSA_KV * LANE), MXU_DTYPE),
        jax.ShapeDtypeStruct((m, N_NSA_KV * LANE), MXU_DTYPE),
        jax.ShapeDtypeStruct((m, N_NSA_KV * LANE), MXU_DTYPE),
        jax.ShapeDtypeStruct((m, nf), MXU_DTYPE),
        jax.ShapeDtypeStruct((m, nf), MXU_DTYPE),
        jax.ShapeDtypeStruct((m, nf), MXU_DTYPE),
        jax.ShapeDtypeStruct((m, N_NSA_KV * LANE), F32),
    )
    out_specs = (
        pl.BlockSpec((tm, N_NSA_HEADS * LANE), row),
        pl.BlockSpec((tm, N_NSA_HEADS * LANE), row),
        pl.BlockSpec((4, tm, HEAD_DIM), lambda i: (0, i, 0)),
        pl.BlockSpec((tm, N_NSA_KV * LANE), row),
        pl.BlockSpec((tm, N_NSA_KV * LANE), row),
        pl.BlockSpec((tm, N_NSA_KV * LANE), row),
        pl.BlockSpec((tm, N_NSA_KV * LANE), row),
        pl.BlockSpec((tm, nf), row),
        pl.BlockSpec((tm, nf), row),
        pl.BlockSpec((tm, nf), row),
        pl.BlockSpec((tm, N_NSA_KV * LANE), row),
    )
    return pl.pallas_call(
        functools.partial(_inproj_body, tm=tm, n_s_tiles=n_s_tiles),
        grid=(m // tm,),
        in_specs=[
            pl.BlockSpec((tm, d), row),
            pl.BlockSpec((1, d), fixed),
            pl.BlockSpec((d, _NP), fixed),
            pl.BlockSpec((tm, LANE), tab),
            pl.BlockSpec((tm, LANE), tab),
            pl.BlockSpec((tm, LANE), tab),
        ],
        out_specs=out_specs,
        out_shape=out_shape,
        compiler_params=pltpu.CompilerParams(
            dimension_semantics=("arbitrary",), vmem_limit_bytes=VMEM_LIMIT),
        name="inproj",
    )(xf, g, wp, rope_c, rope_a, rope_b)


def _cumsum_body(sm_ref, bias_ref, tril_ref, out_ref, *, seq, chunk):
    carry = jnp.zeros((1, LANE), F32)
    tril = tril_ref[...]
    for c in range(seq // chunk):
        sl = slice(c * chunk, (c + 1) * chunk)
        z = sm_ref[sl, :] + bias_ref[...]
        logf = jnp.minimum(z, 0.0) - jnp.log1p(jnp.exp(-jnp.abs(z)))
        hi, mid, lo = _split3(logf)
        cs = (_dot(tril, hi) + _dot(tril, mid)) + _dot(tril, lo) + carry
        out_ref[sl, :] = cs
        carry = cs[chunk - 1:chunk, :]


def _forget_cumsum(small, fbias, tril, *, batch, seq):
    chunk = tril.shape[0]
    return pl.pallas_call(
        functools.partial(_cumsum_body, seq=seq, chunk=chunk),
        grid=(batch,),
        in_specs=[
            pl.BlockSpec((seq, LANE), lambda b: (b, 0)),
            pl.BlockSpec((1, LANE), lambda b: (0, 0)),
            pl.BlockSpec((chunk, chunk), lambda b: (0, 0)),
        ],
        out_specs=pl.BlockSpec((seq, LANE), lambda b: (b, 0)),
        out_shape=jax.ShapeDtypeStruct((batch * seq, LANE), F32),
        compiler_params=pltpu.CompilerParams(
            dimension_semantics=("arbitrary",), vmem_limit_bytes=VMEM_LIMIT),
        name="forget_cumsum",
    )(small, fbias, tril)


def _compress_body(x_ref, pos_ref, w1_ref, b1_ref, w2_ref, b2_ref, o_ref, *, n_chunk):
    half = CMP_STRIDE * HEAD_DIM
    x = x_ref[0, 0]
    first = _dot((x + pos_ref[0, 0:1, :]).astype(MXU_DTYPE), w1_ref[0, 0:half, :])
    second = _dot((x + pos_ref[0, 1:2, :]).astype(MXU_DTYPE), w1_ref[0, half:2 * half, :])
    hid = first + pltpu.roll(second, n_chunk - 1, 0) + b1_ref[0]
    act = jax.nn.gelu(hid)
    o_ref[0, 0] = (_dot(act.astype(MXU_DTYPE), w2_ref[0]) + b2_ref[0]).astype(MXU_DTYPE)


def _compress(kvc, pos, w1, b1, w2p, b2p, *, batch, n_chunk):
    half = CMP_STRIDE * HEAD_DIM
    kv = lambda i, b: (i // N_NSA_KV, 0, 0)
    return pl.pallas_call(
        functools.partial(_compress_body, n_chunk=n_chunk),
        grid=(2 * N_NSA_KV, batch),
        in_specs=[
            pl.BlockSpec((1, 1, n_chunk, half), lambda i, b: (i, b, 0, 0)),
            pl.BlockSpec((1, 2, half), kv),
            pl.BlockSpec((1, 2 * half, CMP_HIDDEN), kv),
            pl.BlockSpec((1, 1, CMP_HIDDEN), kv),
            pl.BlockSpec((1, CMP_HIDDEN, LANE), kv),
            pl.BlockSpec((1, 1, LANE), kv),
        ],
        out_specs=pl.BlockSpec((1, 1, n_chunk, LANE), lambda i, b: (i, b, 0, 0)),
        out_shape=jax.ShapeDtypeStruct((2 * N_NSA_KV, batch, n_chunk, LANE), MXU_DTYPE),
        compiler_params=pltpu.CompilerParams(
            dimension_semantics=("arbitrary", "arbitrary"), vmem_limit_bytes=VMEM_LIMIT),
        name="compress",
    )(kvc, pos, w1, b1, w2p, b2p)


def _lane_max(s):
    out = s[:, 0:LANE]
    for c in range(1, s.shape[1] // LANE):
        out = jnp.maximum(out, s[:, LANE * c:LANE * (c + 1)])
    return out


def _row_max_lanes(mrun):
    return jnp.broadcast_to(jnp.max(mrun, axis=-1, keepdims=True), mrun.shape)


def _exp_tiles(s, mb):
    n = s.shape[1] // LANE
    return jnp.concatenate(
        [jnp.exp(s[:, LANE * c:LANE * (c + 1)] - mb).astype(MXU_DTYPE) for c in range(n)], axis=1)


def _values_with_ones(v, ones_low):
    low = _lanes(v.shape) < HEAD_DIM
    ones = jnp.ones_like(v)
    return jnp.where(low, ones, v) if ones_low else jnp.where(low, v, ones)


def _normalize(acc):
    return acc / pltpu.roll(acc, HEAD_DIM, 1)


def _spans(n_tiles, group, fn):
    n_groups = lax.div(n_tiles, group)

    def group_step(t, carry):
        fn(t * group, group)
        return carry

    def single_step(kb, carry):
        fn(kb, 1)
        return carry

    lax.fori_loop(0, n_groups, group_step, 0)
    lax.fori_loop(n_groups * group, n_tiles, single_step, 0)


def _nsa_body(qn_ref, qr_ref, kc_ref, vc_ref, ksa_ref, vs_ref, kw_ref, vw_ref, sm_ref, gb_ref, impt_ref,
              o_ref, s_ref, mrun_ref, acc_ref, *, tq, tk, group, n_sel, top_k):
    q0 = pl.program_id(2) * tq
    r = NSA_GROUP * tq
    n_cmp = kc_ref.shape[2]

    def window_block(hq):
        n_wc = WINDOW // LANE + 1
        upper = _lanes((NSA_GROUP * LANE, LANE)) > (_rows((NSA_GROUP * LANE, LANE)) & (LANE - 1))
        qs = q0 + hq * LANE
        qh = jnp.concatenate(
            [qr_ref[hq * LANE:(hq + 1) * LANE, LANE * h:LANE * (h + 1)] for h in range(NSA_GROUP)], axis=0)
        starts = [pl.multiple_of(jnp.maximum(qs - WINDOW + LANE * c, 0), LANE) for c in range(n_wc)]
        kk = jnp.concatenate([kw_ref[pl.ds(st, LANE), :] for st in starts], axis=0)
        vv = jnp.concatenate([vw_ref[pl.ds(st, LANE), :] for st in starts], axis=0)
        sw = _dot_nt(qh, kk)
        chunks = [sw[:, LANE * c:LANE * (c + 1)] for c in range(n_wc)]
        chunks[0] = jnp.where(upper, chunks[0], NEG_INF)
        chunks[-1] = jnp.where(upper, NEG_INF, chunks[-1])
        for c in range(n_wc - 1):
            chunks[c] = jnp.where(qs - WINDOW + LANE * c >= 0, chunks[c], NEG_INF)
        mb = _row_max_lanes(functools.reduce(jnp.maximum, chunks))
        pw = jnp.concatenate([jnp.exp(ch - mb).astype(MXU_DTYPE) for ch in chunks], axis=1)
        return _normalize(_dot(pw, _values_with_ones(vv, False)))

    q4n = jnp.concatenate([qn_ref[:, LANE * h:LANE * (h + 1)] for h in range(NSA_GROUP)], axis=0)
    sc = _dot_nt(q4n, kc_ref[0, 0])
    last_vis = (q0 + _rows((tq, LANE)) - (CMP_LEN - 1)) >> CMP_SHIFT
    last_vis = jnp.concatenate([last_vis] * NSA_GROUP, axis=0)
    sc_chunks = [
        jnp.where(_lanes((r, LANE)) + LANE * c <= last_vis, sc[:, LANE * c:LANE * (c + 1)], NEG_INF)
        for c in range(n_cmp // LANE)]
    mb = jnp.maximum(_row_max_lanes(functools.reduce(jnp.maximum, sc_chunks)), 0.1 * NEG_INF)
    e_chunks = [jnp.exp(ch - mb) for ch in sc_chunks]
    den = jnp.sum(functools.reduce(jnp.add, e_chunks), axis=-1, keepdims=True)
    inv = 1.0 / jnp.maximum(jnp.broadcast_to(den, (r, LANE)), 1e-30)
    pc = jnp.concatenate([ech * inv for ech in e_chunks], axis=1)
    o_cmp = _dot(pc.astype(MXU_DTYPE), vc_ref[0, 0])

    pcs = (pc[0:tq] + pc[tq:2 * tq]) + (pc[2 * tq:3 * tq] + pc[3 * tq:4 * tq])
    impt_w = impt_ref[...]
    hi, mid, lo = _split3(pcs)
    imp_t = (_dot_nt(impt_w, hi) + _dot_nt(impt_w, mid)) + _dot_nt(impt_w, lo)
    jrow = _rows((n_sel, tq))
    qpos = q0 + _lanes((n_sel, tq))
    cur = qpos >> SEL_SHIFT
    valid = jrow * SEL_BLOCK <= qpos
    forced = (jrow == 0) | (jrow == cur) | (jrow == cur - 1)
    score = jnp.where(forced, FORCED_SCORE, jnp.where(valid, imp_t, -1.0))

    sub = 8
    chunks = [score[sub * c:sub * (c + 1), :] for c in range(n_sel // sub)]
    ranks = [jnp.zeros((sub, tq), F32) for _ in chunks]
    later = _rows((sub, tq))
    for jp in range(n_sel):
        cj, rj = divmod(jp, sub)
        rowv = chunks[cj][rj:rj + 1, :]
        for c, sc_c in enumerate(chunks):
            if c > cj:
                beats = rowv >= sc_c
            elif c < cj:
                beats = rowv > sc_c
            else:
                beats = (rowv > sc_c) | ((rowv == sc_c) & (later > rj))
            ranks[c] = ranks[c] + jnp.where(beats, 1.0, 0.0)
    sel_t = jnp.where(jnp.concatenate(ranks, axis=0) < float(top_k), 1.0, 0.0)
    sel_pad = jnp.concatenate([sel_t, jnp.zeros((LANE - n_sel, tq), F32)], axis=0)
    sel_q = pltpu.roll(sel_pad.T, HEAD_DIM, 1)
    sel_bias = ((sel_q - 1.0) * (-NEG_INF)).astype(MXU_DTYPE)

    high = _lanes((tq, LANE)) >= HEAD_DIM
    q4r = jnp.concatenate(
        [jnp.where(high, sel_bias, qr_ref[:, LANE * h:LANE * (h + 1)]) for h in range(NSA_GROUP)], axis=0)

    win_blocks = [window_block(0), window_block(1)]
    o_win = jnp.concatenate([blk[h * LANE:(h + 1) * LANE] for h in range(NSA_GROUP) for blk in win_blocks], axis=0)

    gate = jax.nn.sigmoid(sm_ref[...] + gb_ref[...])
    partial, gate_sel = [], []
    for h in range(NSA_GROUP):
        rs = slice(h * tq, (h + 1) * tq)
        c = N_BRANCH * h
        partial.append(gate[:, c:c + 1] * o_cmp[rs] + gate[:, c + 2:c + 3] * o_win[rs])
        gate_sel.append(jnp.broadcast_to(gate[:, c + 1:c + 2], (tq, LANE)))

    rowq_k = q0 + (_rows((r, tk)) & (tq - 1))
    n_full = lax.div(q0, tk)

    def score_span(kb, n_tiles, masked):
        k0 = pl.multiple_of(kb * tk, tk)
        s = _dot_nt(q4r, ksa_ref[pl.ds(k0, n_tiles * tk), :])
        if masked:
            s = jnp.where(k0 + _lanes((r, tk)) <= rowq_k, s, NEG_INF)
        for j in range(n_tiles):
            s_ref[kb + j] = s[:, j * tk:(j + 1) * tk]
        mrun_ref[...] = jnp.maximum(mrun_ref[...], _lane_max(s))

    def value_span(kb, n_tiles):
        k0 = pl.multiple_of(kb * tk, tk)
        mb = mrun_ref[...]
        p = jnp.concatenate([_exp_tiles(s_ref[kb + j], mb) for j in range(n_tiles)], axis=1)
        acc_ref[...] += _dot(p, _values_with_ones(vs_ref[pl.ds(k0, n_tiles * tk), :], False))

    mrun_ref[...] = jnp.full((r, LANE), NEG_INF, F32)
    _spans(n_full, group, lambda kb, n: score_span(kb, n, False))
    score_span(n_full, 1, True)
    mrun_ref[...] = _row_max_lanes(mrun_ref[...])
    acc_ref[...] = jnp.zeros((r, LANE), F32)
    _spans(n_full + 1, group, value_span)
    o_sel = _normalize(acc_ref[...])

    heads = [partial[h] + gate_sel[h] * o_sel[h * tq:(h + 1) * tq] for h in range(NSA_GROUP)]
    low = _lanes((tq, LANE)) < HEAD_DIM
    for p in range(NSA_GROUP // 2):
        pair = jnp.where(low, heads[2 * p], pltpu.roll(heads[2 * p + 1], HEAD_DIM, 1))
        o_ref[:, LANE * p:LANE * (p + 1)] = pair.astype(MXU_DTYPE)


def _nsa(qn, qr, cmp_kv, ksa, vs, kw, vw, small, gbias, imp_t, *, batch, seq, tq, tk, group):
    assert tq == 2 * LANE
    n_q = seq // tq
    n_cmp = cmp_kv.shape[2]
    n_sel = seq // SEL_BLOCK
    top_k = min(SEL_TOPK, n_sel)
    gw = NSA_GROUP * LANE
    qmap = lambda b, g, i: (b * n_q + i, g)
    kvmap = lambda b, g, i: (b, g)
    r = NSA_GROUP * tq
    return pl.pallas_call(
        functools.partial(_nsa_body, tq=tq, tk=tk, group=group, n_sel=n_sel, top_k=top_k),
        grid=(batch, N_NSA_KV, n_q),
        in_specs=[
            pl.BlockSpec((tq, gw), qmap),
            pl.BlockSpec((tq, gw), qmap),
            pl.BlockSpec((1, 1, n_cmp, LANE), lambda b, g, i: (g, b, 0, 0)),
            pl.BlockSpec((1, 1, n_cmp, LANE), lambda b, g, i: (N_NSA_KV + g, b, 0, 0)),
            pl.BlockSpec((seq, LANE), kvmap),
            pl.BlockSpec((seq, LANE), kvmap),
            pl.BlockSpec((seq, LANE), kvmap),
            pl.BlockSpec((seq, LANE), kvmap),
            pl.BlockSpec((tq, LANE), qmap),
            pl.BlockSpec((1, LANE), lambda b, g, i: (0, g)),
            pl.BlockSpec((n_sel, n_cmp), lambda b, g, i: (0, 0)),
        ],
        out_specs=pl.BlockSpec((tq, NSA_GROUP * HEAD_DIM), qmap),
        out_shape=jax.ShapeDtypeStruct((batch * seq, N_NSA_HEADS * HEAD_DIM), MXU_DTYPE),
        scratch_shapes=[
            pltpu.VMEM((seq // tk, r, tk), F32),
            pltpu.VMEM((r, LANE), F32),
            pltpu.VMEM((r, LANE), F32),
        ],
        compiler_params=pltpu.CompilerParams(
            dimension_semantics=("arbitrary", "arbitrary", "arbitrary"), vmem_limit_bytes=VMEM_LIMIT),
        name="nsa_attention",
    )(qn, qr, cmp_kv, cmp_kv, ksa, vs, kw, vw, small, gbias, imp_t)


def _fox_body(q_ref, k_ref, v_ref, cq_ref, ck_ref, o_ref, s_ref, mrun_ref, acc_ref, *, tq, tk, group):
    i = pl.program_id(2)
    q0 = i * tq
    qp = q_ref[...]
    low = _lanes((tq, LANE)) < HEAD_DIM
    causal_rows = q0 + _rows((tq, tk))
    n_full = lax.div(q0, tk)
    qs, cqs = [], []
    for e in range(2):
        qs.append(jnp.where(low if e == 0 else jnp.logical_not(low), qp, jnp.zeros_like(qp)))
        cq = jnp.broadcast_to(cq_ref[0, 0, :, e:e + 1], (tq, LANE))
        cqs.append(jnp.concatenate([cq] * (tk // LANE), axis=1))

    def score_span(kb, n_tiles, masked):
        k0 = pl.multiple_of(kb * tk, tk)
        k = k_ref[pl.ds(k0, n_tiles * tk), :]
        for e in range(2):
            s = _dot_nt(qs[e], k)
            for j in range(n_tiles):
                sj = s[:, j * tk:(j + 1) * tk] + cqs[e] - ck_ref[0, 0, e:e + 1, pl.ds(k0 + j * tk, tk)]
                if masked:
                    sj = jnp.where(k0 + _lanes((tq, tk)) <= causal_rows, sj, NEG_INF)
                s_ref[e, kb + j] = sj
                mrun_ref[e] = jnp.maximum(mrun_ref[e], _lane_max(sj))

    def value_span(kb, n_tiles):
        k0 = pl.multiple_of(kb * tk, tk)
        v = v_ref[pl.ds(k0, n_tiles * tk), :]
        for e in range(2):
            mb = mrun_ref[e]
            p = jnp.concatenate([_exp_tiles(s_ref[e, kb + j], mb) for j in range(n_tiles)], axis=1)
            acc_ref[e] += _dot(p, _values_with_ones(v, e == 1))

    mrun_ref[...] = jnp.full((2, tq, LANE), NEG_INF, F32)
    _spans(n_full, group, lambda kb, n: score_span(kb, n, False))
    score_span(n_full, 1, True)
    for e in range(2):
        mrun_ref[e] = _row_max_lanes(mrun_ref[e])
    acc_ref[...] = jnp.zeros((2, tq, LANE), F32)
    _spans(n_full + 1, group, value_span)
    o_ref[...] = jnp.where(low, _normalize(acc_ref[0]), _normalize(acc_ref[1])).astype(MXU_DTYPE)


def _fox(fq, fk, fv, cq, ck, *, batch, seq, tq, tk, group):
    n_q = seq // tq
    n_pairs = N_FOX_HEADS // 2
    qmap = lambda b, p, i: (b * n_q + i, p)
    kvmap = lambda b, p, i: (b, p)
    return pl.pallas_call(
        functools.partial(_fox_body, tq=tq, tk=tk, group=group),
        grid=(batch, n_pairs, n_q),
        in_specs=[
            pl.BlockSpec((tq, LANE), qmap),
            pl.BlockSpec((seq, LANE), kvmap),
            pl.BlockSpec((seq, LANE), kvmap),
            pl.BlockSpec((1, 1, tq, 2), lambda b, p, i: (b, p, i, 0)),
            pl.BlockSpec((1, 1, 2, seq), lambda b, p, i: (b, p, 0, 0)),
        ],
        out_specs=pl.BlockSpec((tq, LANE), qmap),
        out_shape=jax.ShapeDtypeStruct((batch * seq, N_FOX_HEADS * HEAD_DIM), MXU_DTYPE),
        scratch_shapes=[
            pltpu.VMEM((2, seq // tk, tq, tk), F32),
            pltpu.VMEM((2, tq, LANE), F32),
            pltpu.VMEM((2, tq, LANE), F32),
        ],
        compiler_params=pltpu.CompilerParams(
            dimension_semantics=("arbitrary", "arbitrary", "arbitrary"), vmem_limit_bytes=VMEM_LIMIT),
        name="fox_attention",
    )(fq, fk, fv, cq, ck)


def _post_body(on_ref, of_ref, x_ref, wo_ref, gpm_ref, gpl_ref, wu_ref, wd_ref, gpo_ref, o_ref, *, ff_chunk):
    n_nsa = on_ref.shape[1]
    mix = _dot(on_ref[...], wo_ref[0:n_nsa, :]) + _dot(of_ref[...], wo_ref[n_nsa:, :])
    x1 = x_ref[...] + _rms(mix, gpm_ref[...])
    h = _rms(x1, gpl_ref[...]).astype(MXU_DTYPE)
    d_ff = wu_ref.shape[1]
    acc = jnp.zeros(x1.shape, F32)
    for c in range(d_ff // ff_chunk):
        sl = slice(c * ff_chunk, (c + 1) * ff_chunk)
        a = jnp.square(jnp.maximum(_dot(h, wu_ref[:, sl]), 0.0))
        acc = acc + _dot(a.astype(MXU_DTYPE), wd_ref[sl, :])
    o_ref[...] = x1 + _rms(acc, gpo_ref[...])


def _post(o_nsa, o_fox, xf, wo, g_post_mix, g_pre_mlp, wu, wd, g_post_mlp, *, tm, ff_chunk):
    m, d = xf.shape
    d_ff = wu.shape[1]
    row = lambda i: (i, 0)
    fixed = lambda i: (0, 0)
    resident = functools.partial(pl.BlockSpec, index_map=fixed, pipeline_mode=pl.Buffered(1))
    return pl.pallas_call(
        functools.partial(_post_body, ff_chunk=ff_chunk),
        grid=(m // tm,),
        in_specs=[
            pl.BlockSpec((tm, o_nsa.shape[1]), row),
            pl.BlockSpec((tm, o_fox.shape[1]), row),
            pl.BlockSpec((tm, d), row),
            resident((wo.shape[0], d)),
            pl.BlockSpec((1, d), fixed),
            pl.BlockSpec((1, d), fixed),
            resident((d, d_ff)),
            resident((d_ff, d)),
            pl.BlockSpec((1, d), fixed),
        ],
        out_specs=pl.BlockSpec((tm, d), row),
        out_shape=jax.ShapeDtypeStruct((m, d), F32),
        compiler_params=pltpu.CompilerParams(
            dimension_semantics=("arbitrary",), vmem_limit_bytes=VMEM_LIMIT),
        name="outproj_mlp",
    )(o_nsa, o_fox, xf, wo, g_post_mix, g_pre_mlp, wu, wd, g_post_mlp)


def _rope_tables(seq):
    half = ROPE_DIMS // 2
    inv_freq = ROPE_THETA ** (-jnp.arange(half, dtype=F32) * 2.0 / ROPE_DIMS)
    ang = jnp.arange(seq).astype(F32)[:, None] * inv_freq[None, :]
    cos, sin = jnp.cos(ang), jnp.sin(ang)
    pad = jnp.zeros((seq, LANE - ROPE_DIMS), F32)
    zero = jnp.zeros((seq, half), F32)
    c = jnp.concatenate([cos, cos, pad + 1.0], axis=-1)
    a = jnp.concatenate([-sin, zero, pad], axis=-1)
    b = jnp.concatenate([zero, sin, pad], axis=-1)
    return c, a, b


def _importance_map_t(seq, n_chunk):
    n_cmp = (seq - CMP_LEN) // CMP_STRIDE + 1
    n_sel = seq // SEL_BLOCK
    cs = np.arange(n_cmp)[None, :] * CMP_STRIDE
    ss = np.arange(n_sel)[:, None] * SEL_BLOCK
    overlap = np.clip(np.minimum(cs + CMP_LEN, ss + SEL_BLOCK) - np.maximum(cs, ss), 0, None) / CMP_LEN
    out = np.zeros((n_sel, n_chunk), np.float32)
    out[:, :n_cmp] = overlap
    return jnp.asarray(out, dtype=MXU_DTYPE)


def kernel(x, w_in, b_nsa_gate, b_forget, cmp_pos_k, cmp_w1_k, cmp_b1_k, cmp_w2_k, cmp_b2_k, cmp_pos_v, cmp_w1_v, cmp_b1_v, cmp_w2_v, cmp_b2_v, w_out, w_up, w_down, g_pre_mix, g_post_mix, g_pre_mlp, g_post_mlp):
    batch, seq, d_model = x.shape
    depth = w_in.shape[0]
    m = batch * seq
    n_chunk = seq // CMP_STRIDE
    n_sel = seq // SEL_BLOCK
    tm = min(512, seq)
    nsa_tq, nsa_tk = 2 * LANE, min(256, seq)
    fox_t = min(512, seq)
    assert seq % tm == 0 and seq % fox_t == 0 and seq % nsa_tk == 0
    assert n_sel <= HEAD_DIM and n_sel % 8 == 0 and seq >= WINDOW + nsa_tq
    assert x.dtype == F32

    rope_c, rope_a, rope_b = _rope_tables(seq)
    imp_t = _importance_map_t(seq, n_chunk)
    cs_chunk = min(256, seq)
    tril = jnp.asarray(np.tril(np.ones((cs_chunk, cs_chunk), np.float32)), dtype=MXU_DTYPE)

    src = _projection_columns()
    keep = jnp.asarray(src >= 0)
    wp = jnp.where(keep[None, None, :], jnp.take(w_in, jnp.asarray(np.maximum(src, 0)), axis=2), 0.0).astype(MXU_DTYPE)

    gate_bias = jnp.zeros((depth, 1, N_NSA_KV * LANE), F32)
    fbias = jnp.zeros((depth, 1, LANE), F32)
    n_gate = NSA_GROUP * N_BRANCH
    for g in range(N_NSA_KV):
        gate_bias = gate_bias.at[:, 0, LANE * g:LANE * g + n_gate].set(b_nsa_gate[:, n_gate * g:n_gate * (g + 1)])
    fbias = fbias.at[:, 0, _FF_LANE:_FF_LANE + N_FOX_HEADS].set(b_forget)

    half = CMP_STRIDE * HEAD_DIM
    cpos = jnp.stack([cmp_pos_k.reshape(depth, 2, half), cmp_pos_v.reshape(depth, 2, half)], axis=1)
    cw1 = jnp.stack([cmp_w1_k, cmp_w1_v], axis=1).astype(MXU_DTYPE)
    cb1 = jnp.stack([cmp_b1_k, cmp_b1_v], axis=1)[:, :, None, :]
    zw = jnp.zeros_like(cmp_w2_k)
    cw2 = jnp.stack([jnp.concatenate([cmp_w2_k, zw], axis=-1),
                     jnp.concatenate([cmp_w2_v, cmp_w2_v], axis=-1)], axis=1).astype(MXU_DTYPE)
    zb = jnp.zeros_like(cmp_b2_k)
    cb2 = jnp.stack([jnp.concatenate([cmp_b2_k, zb], axis=-1),
                     jnp.concatenate([cmp_b2_v, cmp_b2_v], axis=-1)], axis=1)[:, :, None, :]

    wo = w_out.astype(MXU_DTYPE)
    wu = w_up.astype(MXU_DTYPE)
    wd = w_down.astype(MXU_DTYPE)

    xf = x.reshape(m, d_model)
    n_pairs = N_FOX_HEADS // 2
    for l in range(depth):
        qn, qr, kvc, ksa, vs, kw, vw, fq, fk, fv, small = _inproj(
            xf, g_pre_mix[l][None, :], wp[l], rope_c, rope_a, rope_b, seq=seq, tm=tm)

        cum = _forget_cumsum(small, fbias[l], tril, batch=batch, seq=seq)
        cum = cum[:, _FF_LANE:_FF_LANE + N_FOX_HEADS].reshape(batch, seq, n_pairs, 2)
        cq = cum.transpose(0, 2, 1, 3)
        ck = cum.transpose(0, 2, 3, 1)

        cmp_kv = _compress(kvc.reshape(4, batch, n_chunk, half), cpos[l], cw1[l], cb1[l], cw2[l], cb2[l],
                           batch=batch, n_chunk=n_chunk)
        o_nsa = _nsa(qn, qr, cmp_kv, ksa, vs, kw, vw, small, gate_bias[l], imp_t,
                     batch=batch, seq=seq, tq=nsa_tq, tk=nsa_tk, group=4)
        o_fox = _fox(fq, fk, fv, cq, ck, batch=batch, seq=seq, tq=fox_t, tk=fox_t, group=2)
        xf = _post(o_nsa, o_fox, xf, wo[l], g_post_mix[l][None, :], g_pre_mlp[l][None, :],
                   wu[l], wd[l], g_post_mlp[l][None, :], tm=tm, ff_chunk=1024)
    return xf.reshape(batch, seq, d_model)
```

```python
import functools

import numpy as np
import jax
import jax.numpy as jnp
from jax import lax
from jax.experimental import pallas as pl
from jax.experimental.pallas import tpu as pltpu

HEAD_DIM = 64
N_NSA_HEADS = 8
N_NSA_KV = 2
NSA_GROUP = N_NSA_HEADS // N_NSA_KV
N_FOX_HEADS = 8
N_BRANCH = 3
CMP_LEN = 32
CMP_STRIDE = 16
CMP_SHIFT = 4
CMP_HIDDEN = 256
SEL_BLOCK = 64
SEL_SHIFT = 6
SEL_TOPK = 16
WINDOW = 512
ROPE_THETA = 500000.0
ROPE_DIMS = HEAD_DIM // 4
NORM_EPS = 1e-6
NEG_INF = -1e30
FORCED_SCORE = 1e4
SCALE = HEAD_DIM ** -0.5
LOG2E = 1.4426950408889634
Q_SCALE = SCALE * LOG2E

LANE = 128
MXU_DTYPE = jnp.bfloat16
F32 = jnp.float32
VMEM_LIMIT = 56 * 1024 * 1024

_SRC_NQ, _SRC_KC, _SRC_VC, _SRC_KSEL, _SRC_VSEL, _SRC_KWIN, _SRC_VWIN = 0, 512, 640, 768, 896, 1024, 1152
_SRC_GATE, _SRC_FQ, _SRC_FK, _SRC_FV, _SRC_FF = 1280, 1304, 1816, 2328, 2840
_Q0, _KVC0, _KS0, _VS0, _KW0, _VW0, _FQ0, _FK0, _FV0, _SM0, _NP = (
    0, 1024, 1536, 1792, 2048, 2304, 2560, 3072, 3584, 4096, 4352)
_FF_LANE = 16


def _projection_columns():
    src = np.full((_NP,), -1, np.int32)
    d = np.arange(HEAD_DIM)
    for h in range(N_NSA_HEADS):
        src[_Q0 + LANE * h + d] = _SRC_NQ + HEAD_DIM * h + d
    for i in range(4):
        src[_KVC0 + LANE * i + d] = _SRC_KC + HEAD_DIM * i + d
    for g in range(N_NSA_KV):
        src[_KS0 + LANE * g + d] = _SRC_KSEL + HEAD_DIM * g + d
        src[_KW0 + LANE * g + d] = _SRC_KWIN + HEAD_DIM * g + d
        for half in range(2):
            src[_VS0 + LANE * g + HEAD_DIM * half + d] = _SRC_VSEL + HEAD_DIM * g + d
            src[_VW0 + LANE * g + HEAD_DIM * half + d] = _SRC_VWIN + HEAD_DIM * g + d
        gates = np.arange(NSA_GROUP * N_BRANCH)
        src[_SM0 + LANE * g + gates] = _SRC_GATE + NSA_GROUP * N_BRANCH * g + gates
        src[_SM0 + LANE * g + _FF_LANE + np.arange(N_FOX_HEADS)] = _SRC_FF + np.arange(N_FOX_HEADS)
    fox = np.arange(N_FOX_HEADS * HEAD_DIM)
    src[_FQ0 + fox] = _SRC_FQ + fox
    src[_FK0 + fox] = _SRC_FK + fox
    src[_FV0 + fox] = _SRC_FV + fox
    return src


def _permute_columns(w, src):
    pieces, i = [], 0
    while i < len(src):
        j = i + 1
        if src[i] < 0:
            while j < len(src) and src[j] < 0:
                j += 1
            pieces.append(jnp.zeros(w.shape[:-1] + (j - i,), w.dtype))
        else:
            while j < len(src) and src[j] == src[j - 1] + 1:
                j += 1
            pieces.append(w[..., int(src[i]):int(src[i]) + (j - i)])
        i = j
    return jnp.concatenate(pieces, axis=-1)


def _rms(x, g):
    ms = jnp.mean(x * x, axis=-1, keepdims=True)
    return x * lax.rsqrt(ms + NORM_EPS) * g


def _dot(a, b):
    return jnp.dot(a, b, preferred_element_type=F32)


def _dot_nt(a, b):
    return lax.dot_general(a, b, (((1,), (1,)), ((), ())), preferred_element_type=F32)


def _split3(x):
    hi = x.astype(MXU_DTYPE)
    r1 = x - hi.astype(F32)
    mid = r1.astype(MXU_DTYPE)
    lo = (r1 - mid.astype(F32)).astype(MXU_DTYPE)
    return hi, mid, lo


def _rows(shape):
    return lax.broadcasted_iota(jnp.int32, shape, 0)


def _lanes(shape):
    return lax.broadcasted_iota(jnp.int32, shape, 1)


def _inproj_body(x_ref, g_ref, w_ref, rc_ref, ra_ref, rb_ref,
                 qn_ref, qr_ref, kvc_ref, ksa_ref, vs_ref, kw_ref, vw_ref,
                 fq_ref, fk_ref, fv_ref, small_ref, *, tm, n_s_tiles):
    h = _rms(x_ref[...], g_ref[...]).astype(MXU_DTYPE)
    rc, ra, rb = rc_ref[...], ra_ref[...], rb_ref[...]

    def mm(lo, n):
        return _dot(h, w_ref[:, lo:lo + n])

    def rope(v):
        return v * rc + pltpu.roll(v, LANE - ROPE_DIMS // 2, 1) * ra + pltpu.roll(v, ROPE_DIMS // 2, 1) * rb

    q = mm(_Q0, N_NSA_HEADS * LANE) * Q_SCALE
    qn_ref[...] = q.astype(MXU_DTYPE)
    for hh in range(N_NSA_HEADS):
        sl = slice(LANE * hh, LANE * (hh + 1))
        qr_ref[:, sl] = rope(q[:, sl]).astype(MXU_DTYPE)

    kvc = mm(_KVC0, 4 * LANE)
    for i in range(4):
        kvc_ref[i] = kvc[:, LANE * i:LANE * i + HEAD_DIM]

    pos = (pl.program_id(0) % n_s_tiles) * tm + _rows((tm, LANE))
    onehot = ((_lanes((tm, LANE)) - HEAD_DIM) == (pos >> SEL_SHIFT)).astype(F32)
    ks = mm(_KS0, N_NSA_KV * LANE)
    kw = mm(_KW0, N_NSA_KV * LANE)
    for g in range(N_NSA_KV):
        sl = slice(LANE * g, LANE * (g + 1))
        ksa_ref[:, sl] = (rope(ks[:, sl]) + onehot).astype(MXU_DTYPE)
        kw_ref[:, sl] = rope(kw[:, sl]).astype(MXU_DTYPE)
    vs_ref[...] = mm(_VS0, N_NSA_KV * LANE).astype(MXU_DTYPE)
    vw_ref[...] = mm(_VW0, N_NSA_KV * LANE).astype(MXU_DTYPE)

    nf = N_FOX_HEADS * HEAD_DIM
    fq_ref[...] = (mm(_FQ0, nf) * Q_SCALE).astype(MXU_DTYPE)
    fk_ref[...] = mm(_FK0, nf).astype(MXU_DTYPE)
    fv_ref[...] = mm(_FV0, nf).astype(MXU_DTYPE)
    small_ref[...] = mm(_SM0, N_NSA_KV * LANE)


def _inproj(xf, g, wp, rope_c, rope_a, rope_b, *, layer, seq, tm):
    m, d = xf.shape
    n_s_tiles = seq // tm
    nf = N_FOX_HEADS * HEAD_DIM
    row = lambda i: (i, 0)
    fixed = lambda i: (0, 0)
    tab = lambda i: (i % n_s_tiles, 0)
    out_shape = (
        jax.ShapeDtypeStruct((m, N_NSA_HEADS * LANE), MXU_DTYPE),
        jax.ShapeDtypeStruct((m, N_NSA_HEADS * LANE), MXU_DTYPE),
        jax.ShapeDtypeStruct((4, m, HEAD_DIM), F32),
        jax.ShapeDtypeStruct((m, N_NSA_KV * LANE), MXU_DTYPE),
        jax.ShapeDtypeStruct((m, N_NSA_KV * LANE), MXU_DTYPE),
        jax.ShapeDtypeStruct((m, N_NSA_KV * LANE), MXU_DTYPE),
        jax.ShapeDtypeStruct((m, N_NSA_KV * LANE), MXU_DTYPE),
        jax.ShapeDtypeStruct((m, nf), MXU_DTYPE),
        jax.ShapeDtypeStruct((m, nf), MXU_DTYPE),
        jax.ShapeDtypeStruct((m, nf), MXU_DTYPE),
        jax.ShapeDtypeStruct((m, N_NSA_KV * LANE), F32),
    )
    out_specs = (
        pl.BlockSpec((tm, N_NSA_HEADS * LANE), row),
        pl.BlockSpec((tm, N_NSA_HEADS * LANE), row),
        pl.BlockSpec((4, tm, HEAD_DIM), lambda i: (0, i, 0)),
        pl.BlockSpec((tm, N_NSA_KV * LANE), row),
        pl.BlockSpec((tm, N_NSA_KV * LANE), row),
        pl.BlockSpec((tm, N_NSA_KV * LANE), row),
        pl.BlockSpec((tm, N_NSA_KV * LANE), row),
        pl.BlockSpec((tm, nf), row),
        pl.BlockSpec((tm, nf), row),
        pl.BlockSpec((tm, nf), row),
        pl.BlockSpec((tm, N_NSA_KV * LANE), row),
    )
    return pl.pallas_call(
        functools.partial(_inproj_body, tm=tm, n_s_tiles=n_s_tiles),
        grid=(m // tm,),
        in_specs=[
            pl.BlockSpec((tm, d), row),
            pl.BlockSpec((1, d), fixed),
            pl.BlockSpec((None, d, _NP), lambda i: (layer, 0, 0)),
            pl.BlockSpec((tm, LANE), tab),
            pl.BlockSpec((tm, LANE), tab),
            pl.BlockSpec((tm, LANE), tab),
        ],
        out_specs=out_specs,
        out_shape=out_shape,
        compiler_params=pltpu.CompilerParams(
            dimension_semantics=("arbitrary",), vmem_limit_bytes=VMEM_LIMIT),
        name="inproj",
    )(xf, g, wp, rope_c, rope_a, rope_b)


def _cumsum_body(sm_ref, bias_ref, tril_ref, out_ref, *, seq, chunk):
    carry = jnp.zeros((1, LANE), F32)
    tril = tril_ref[...]
    for c in range(seq // chunk):
        sl = slice(c * chunk, (c + 1) * chunk)
        z = sm_ref[sl, :] + bias_ref[...]
        logf = jnp.minimum(z, 0.0) - jnp.log1p(jnp.exp(-jnp.abs(z)))
        hi, mid, lo = _split3(logf)
        cs = (_dot(tril, hi) + _dot(tril, mid)) + _dot(tril, lo) + carry
        out_ref[sl, :] = cs * LOG2E
        carry = cs[chunk - 1:chunk, :]


def _forget_cumsum(small, fbias, tril, *, batch, seq):
    chunk = tril.shape[0]
    return pl.pallas_call(
        functools.partial(_cumsum_body, seq=seq, chunk=chunk),
        grid=(batch,),
        in_specs=[
            pl.BlockSpec((seq, LANE), lambda b: (b, 0)),
            pl.BlockSpec((1, LANE), lambda b: (0, 0)),
            pl.BlockSpec((chunk, chunk), lambda b: (0, 0)),
        ],
        out_specs=pl.BlockSpec((seq, LANE), lambda b: (b, 0)),
        out_shape=jax.ShapeDtypeStruct((batch * seq, LANE), F32),
        compiler_params=pltpu.CompilerParams(
            dimension_semantics=("arbitrary",), vmem_limit_bytes=VMEM_LIMIT),
        name="forget_cumsum",
    )(small, fbias, tril)


def _compress_body(x_ref, pos_ref, w1_ref, b1_ref, w2_ref, b2_ref, o_ref, *, n_chunk):
    half = CMP_STRIDE * HEAD_DIM
    x = x_ref[0, 0]
    first = _dot((x + pos_ref[0, 0:1, :]).astype(MXU_DTYPE), w1_ref[0, 0:half, :])
    second = _dot((x + pos_ref[0, 1:2, :]).astype(MXU_DTYPE), w1_ref[0, half:2 * half, :])
    hid = first + pltpu.roll(second, n_chunk - 1, 0) + b1_ref[0]
    act = jax.nn.gelu(hid)
    o_ref[0, 0] = (_dot(act.astype(MXU_DTYPE), w2_ref[0]) + b2_ref[0]).astype(MXU_DTYPE)


def _compress(kvc, pos, w1, b1, w2p, b2p, *, batch, n_chunk):
    half = CMP_STRIDE * HEAD_DIM
    kv = lambda i, b: (i // N_NSA_KV, 0, 0)
    return pl.pallas_call(
        functools.partial(_compress_body, n_chunk=n_chunk),
        grid=(2 * N_NSA_KV, batch),
        in_specs=[
            pl.BlockSpec((1, 1, n_chunk, half), lambda i, b: (i, b, 0, 0)),
            pl.BlockSpec((1, 2, half), kv),
            pl.BlockSpec((1, 2 * half, CMP_HIDDEN), kv),
            pl.BlockSpec((1, 1, CMP_HIDDEN), kv),
            pl.BlockSpec((1, CMP_HIDDEN, LANE), kv),
            pl.BlockSpec((1, 1, LANE), kv),
        ],
        out_specs=pl.BlockSpec((1, 1, n_chunk, LANE), lambda i, b: (i, b, 0, 0)),
        out_shape=jax.ShapeDtypeStruct((2 * N_NSA_KV, batch, n_chunk, LANE), MXU_DTYPE),
        compiler_params=pltpu.CompilerParams(
            dimension_semantics=("arbitrary", "arbitrary"), vmem_limit_bytes=VMEM_LIMIT),
        name="compress",
    )(kvc, pos, w1, b1, w2p, b2p)


def _lane_max(s):
    out = s[:, 0:LANE]
    for c in range(1, s.shape[1] // LANE):
        out = jnp.maximum(out, s[:, LANE * c:LANE * (c + 1)])
    return out


def _row_max_lanes(mrun):
    return jnp.broadcast_to(jnp.max(mrun, axis=-1, keepdims=True), mrun.shape)


def _exp_tiles(s, mb):
    n = s.shape[1] // LANE
    return jnp.concatenate(
        [jnp.exp2(s[:, LANE * c:LANE * (c + 1)] - mb).astype(MXU_DTYPE) for c in range(n)], axis=1)


def _values_with_ones(v, ones_low):
    low = _lanes(v.shape) < HEAD_DIM
    ones = jnp.ones_like(v)
    return jnp.where(low, ones, v) if ones_low else jnp.where(low, v, ones)


def _normalize(acc):
    return acc / pltpu.roll(acc, HEAD_DIM, 1)


def _spans(n_tiles, groups, fn):
    start = 0
    for size in tuple(groups) + (1,):
        n_groups = lax.div(n_tiles - start, size)

        def step(t, carry, size=size, start=start):
            fn(start + t * size, size)
            return carry

        lax.fori_loop(0, n_groups, step, 0)
        start = start + n_groups * size


def _nsa_body(qn_ref, qr_ref, kc_ref, vc_ref, ksa_ref, vs_ref, kw_ref, vw_ref, sm_ref, gb_ref, impt_ref,
              o_ref, s_ref, mrun_ref, acc_ref, *, tq, tk, group, n_sel, top_k):
    q0 = pl.program_id(2) * tq
    r = NSA_GROUP * tq
    n_cmp = kc_ref.shape[2]

    def window_block(hq):
        n_wc = WINDOW // LANE + 1
        upper = _lanes((NSA_GROUP * LANE, LANE)) > (_rows((NSA_GROUP * LANE, LANE)) & (LANE - 1))
        qs = q0 + hq * LANE
        qh = jnp.concatenate(
            [qr_ref[hq * LANE:(hq + 1) * LANE, LANE * h:LANE * (h + 1)] for h in range(NSA_GROUP)], axis=0)
        starts = [pl.multiple_of(jnp.maximum(qs - WINDOW + LANE * c, 0), LANE) for c in range(n_wc)]
        kk = jnp.concatenate([kw_ref[pl.ds(st, LANE), :] for st in starts], axis=0)
        vv = jnp.concatenate([vw_ref[pl.ds(st, LANE), :] for st in starts], axis=0)
        sw = _dot_nt(qh, kk)
        chunks = [sw[:, LANE * c:LANE * (c + 1)] for c in range(n_wc)]
        chunks[0] = jnp.where(upper, chunks[0], NEG_INF)
        chunks[-1] = jnp.where(upper, NEG_INF, chunks[-1])
        for c in range(n_wc - 1):
            chunks[c] = jnp.where(qs - WINDOW + LANE * c >= 0, chunks[c], NEG_INF)
        mb = _row_max_lanes(functools.reduce(jnp.maximum, chunks))
        pw = jnp.concatenate([jnp.exp2(ch - mb).astype(MXU_DTYPE) for ch in chunks], axis=1)
        return _normalize(_dot(pw, _values_with_ones(vv, False)))

    q4n = jnp.concatenate([qn_ref[:, LANE * h:LANE * (h + 1)] for h in range(NSA_GROUP)], axis=0)
    sc = _dot_nt(q4n, kc_ref[0, 0])
    last_vis = (q0 + _rows((tq, LANE)) - (CMP_LEN - 1)) >> CMP_SHIFT
    last_vis = jnp.concatenate([last_vis] * NSA_GROUP, axis=0)
    sc_chunks = [
        jnp.where(_lanes((r, LANE)) + LANE * c <= last_vis, sc[:, LANE * c:LANE * (c + 1)], NEG_INF)
        for c in range(n_cmp // LANE)]
    mb = jnp.maximum(_row_max_lanes(functools.reduce(jnp.maximum, sc_chunks)), 0.1 * NEG_INF)
    e_chunks = [jnp.exp2(ch - mb) for ch in sc_chunks]
    den = jnp.sum(functools.reduce(jnp.add, e_chunks), axis=-1, keepdims=True)
    inv = 1.0 / jnp.maximum(jnp.broadcast_to(den, (r, LANE)), 1e-30)
    pc = jnp.concatenate([ech * inv for ech in e_chunks], axis=1)
    o_cmp = _dot(pc.astype(MXU_DTYPE), vc_ref[0, 0])

    pcs = (pc[0:tq] + pc[tq:2 * tq]) + (pc[2 * tq:3 * tq] + pc[3 * tq:4 * tq])
    impt_w = impt_ref[...]
    hi, mid, lo = _split3(pcs)
    imp_t = (_dot_nt(impt_w, hi) + _dot_nt(impt_w, mid)) + _dot_nt(impt_w, lo)
    jrow = _rows((n_sel, tq))
    qpos = q0 + _lanes((n_sel, tq))
    cur = qpos >> SEL_SHIFT
    valid = jrow * SEL_BLOCK <= qpos
    forced = (jrow == 0) | (jrow == cur) | (jrow == cur - 1)
    score = jnp.where(forced, FORCED_SCORE, jnp.where(valid, imp_t, -1.0))

    sub = 8
    chunks = [score[sub * c:sub * (c + 1), :] for c in range(n_sel // sub)]
    ranks = [jnp.zeros((sub, tq), F32) for _ in chunks]
    later = _rows((sub, tq))
    for jp in range(n_sel):
        cj, rj = divmod(jp, sub)
        rowv = chunks[cj][rj:rj + 1, :]
        for c, sc_c in enumerate(chunks):
            if c > cj:
                beats = rowv >= sc_c
            elif c < cj:
                beats = rowv > sc_c
            else:
                beats = (rowv > sc_c) | ((rowv == sc_c) & (later > rj))
            ranks[c] = ranks[c] + jnp.where(beats, 1.0, 0.0)
    sel_t = jnp.where(jnp.concatenate(ranks, axis=0) < float(top_k), 1.0, 0.0)
    sel_pad = jnp.concatenate([sel_t, jnp.zeros((LANE - n_sel, tq), F32)], axis=0)
    sel_q = pltpu.roll(sel_pad.T, HEAD_DIM, 1)
    sel_bias = ((sel_q - 1.0) * (-NEG_INF)).astype(MXU_DTYPE)

    high = _lanes((tq, LANE)) >= HEAD_DIM
    q4r = jnp.concatenate(
        [jnp.where(high, sel_bias, qr_ref[:, LANE * h:LANE * (h + 1)]) for h in range(NSA_GROUP)], axis=0)

    win_blocks = [window_block(0), window_block(1)]
    o_win = jnp.concatenate([blk[h * LANE:(h + 1) * LANE] for h in range(NSA_GROUP) for blk in win_blocks], axis=0)

    gate = jax.nn.sigmoid(sm_ref[...] + gb_ref[...])
    partial, gate_sel = [], []
    for h in range(NSA_GROUP):
        rs = slice(h * tq, (h + 1) * tq)
        c = N_BRANCH * h
        partial.append(gate[:, c:c + 1] * o_cmp[rs] + gate[:, c + 2:c + 3] * o_win[rs])
        gate_sel.append(jnp.broadcast_to(gate[:, c + 1:c + 2], (tq, LANE)))

    rowq_k = q0 + (_rows((r, tk)) & (tq - 1))
    n_full = lax.div(q0, tk)

    def score_span(kb, n_tiles, masked):
        k0 = pl.multiple_of(kb * tk, tk)
        s = _dot_nt(q4r, ksa_ref[pl.ds(k0, n_tiles * tk), :])
        if masked:
            s = jnp.where(k0 + _lanes((r, tk)) <= rowq_k, s, NEG_INF)
        for j in range(n_tiles):
            s_ref[kb + j] = s[:, j * tk:(j + 1) * tk]
        mrun_ref[...] = jnp.maximum(mrun_ref[...], _lane_max(s))

    def value_span(kb, n_tiles):
        k0 = pl.multiple_of(kb * tk, tk)
        mb = mrun_ref[...]
        p = jnp.concatenate([_exp_tiles(s_ref[kb + j], mb) for j in range(n_tiles)], axis=1)
        acc_ref[...] += _dot(p, _values_with_ones(vs_ref[pl.ds(k0, n_tiles * tk), :], False))

    mrun_ref[...] = jnp.full((r, LANE), NEG_INF, F32)
    _spans(n_full, group, lambda kb, n: score_span(kb, n, False))
    score_span(n_full, 1, True)
    mrun_ref[...] = _row_max_lanes(mrun_ref[...])
    acc_ref[...] = jnp.zeros((r, LANE), F32)
    _spans(n_full + 1, group, value_span)
    o_sel = _normalize(acc_ref[...])

    heads = [partial[h] + gate_sel[h] * o_sel[h * tq:(h + 1) * tq] for h in range(NSA_GROUP)]
    low = _lanes((tq, LANE)) < HEAD_DIM
    for p in range(NSA_GROUP // 2):
        pair = jnp.where(low, heads[2 * p], pltpu.roll(heads[2 * p + 1], HEAD_DIM, 1))
        o_ref[:, LANE * p:LANE * (p + 1)] = pair.astype(MXU_DTYPE)


def _nsa(qn, qr, cmp_kv, ksa, vs, kw, vw, small, gbias, imp_t, *, batch, seq, tq, tk, group):
    assert tq == 2 * LANE
    n_q = seq // tq
    n_cmp = cmp_kv.shape[2]
    n_sel = seq // SEL_BLOCK
    top_k = min(SEL_TOPK, n_sel)
    gw = NSA_GROUP * LANE
    qmap = lambda b, g, i: (b * n_q + i, g)
    kvmap = lambda b, g, i: (b, g)
    r = NSA_GROUP * tq
    return pl.pallas_call(
        functools.partial(_nsa_body, tq=tq, tk=tk, group=group, n_sel=n_sel, top_k=top_k),
        grid=(batch, N_NSA_KV, n_q),
        in_specs=[
            pl.BlockSpec((tq, gw), qmap),
            pl.BlockSpec((tq, gw), qmap),
            pl.BlockSpec((1, 1, n_cmp, LANE), lambda b, g, i: (g, b, 0, 0)),
            pl.BlockSpec((1, 1, n_cmp, LANE), lambda b, g, i: (N_NSA_KV + g, b, 0, 0)),
            pl.BlockSpec((seq, LANE), kvmap),
            pl.BlockSpec((seq, LANE), kvmap),
            pl.BlockSpec((seq, LANE), kvmap),
            pl.BlockSpec((seq, LANE), kvmap),
            pl.BlockSpec((tq, LANE), qmap),
            pl.BlockSpec((1, LANE), lambda b, g, i: (0, g)),
            pl.BlockSpec((n_sel, n_cmp), lambda b, g, i: (0, 0)),
        ],
        out_specs=pl.BlockSpec((tq, NSA_GROUP * HEAD_DIM), qmap),
        out_shape=jax.ShapeDtypeStruct((batch * seq, N_NSA_HEADS * HEAD_DIM), MXU_DTYPE),
        scratch_shapes=[
            pltpu.VMEM((seq // tk, r, tk), F32),
            pltpu.VMEM((r, LANE), F32),
            pltpu.VMEM((r, LANE), F32),
        ],
        compiler_params=pltpu.CompilerParams(
            dimension_semantics=("arbitrary", "arbitrary", "arbitrary"), vmem_limit_bytes=VMEM_LIMIT),
        name="nsa_attention",
    )(qn, qr, cmp_kv, cmp_kv, ksa, vs, kw, vw, small, gbias, imp_t)


def _fox_body(q_ref, k_ref, v_ref, cq_ref, ck_ref, o_ref, s_ref, mrun_ref, acc_ref, *, tq, tk, group):
    i = pl.program_id(2)
    q0 = i * tq
    qp = q_ref[...]
    low = _lanes((tq, LANE)) < HEAD_DIM
    causal_rows = q0 + _rows((tq, tk))
    n_full = lax.div(q0, tk)
    qs, cqs = [], []
    for e in range(2):
        qs.append(jnp.where(low if e == 0 else jnp.logical_not(low), qp, jnp.zeros_like(qp)))
        cq = jnp.broadcast_to(cq_ref[0, 0, :, e:e + 1], (tq, LANE))
        cqs.append(jnp.concatenate([cq] * (tk // LANE), axis=1))

    def score_span(kb, n_tiles, masked):
        k0 = pl.multiple_of(kb * tk, tk)
        k = k_ref[pl.ds(k0, n_tiles * tk), :]
        for e in range(2):
            s = _dot_nt(qs[e], k)
            for j in range(n_tiles):
                sj = s[:, j * tk:(j + 1) * tk] + cqs[e] - ck_ref[0, 0, e:e + 1, pl.ds(k0 + j * tk, tk)]
                if masked:
                    sj = jnp.where(k0 + _lanes((tq, tk)) <= causal_rows, sj, NEG_INF)
                s_ref[e, kb + j] = sj
                mrun_ref[e] = jnp.maximum(mrun_ref[e], _lane_max(sj))

    def value_span(kb, n_tiles):
        k0 = pl.multiple_of(kb * tk, tk)
        v = v_ref[pl.ds(k0, n_tiles * tk), :]
        for e in range(2):
            mb = mrun_ref[e]
            p = jnp.concatenate([_exp_tiles(s_ref[e, kb + j], mb) for j in range(n_tiles)], axis=1)
            acc_ref[e] += _dot(p, _values_with_ones(v, e == 1))

    mrun_ref[...] = jnp.full((2, tq, LANE), NEG_INF, F32)
    _spans(n_full, group, lambda kb, n: score_span(kb, n, False))
    score_span(n_full, 1, True)
    for e in range(2):
        mrun_ref[e] = _row_max_lanes(mrun_ref[e])
    acc_ref[...] = jnp.zeros((2, tq, LANE), F32)
    _spans(n_full + 1, group, value_span)
    o_ref[...] = jnp.where(low, _normalize(acc_ref[0]), _normalize(acc_ref[1])).astype(MXU_DTYPE)


def _fox(fq, fk, fv, cq, ck, *, batch, seq, tq, tk, group):
    n_q = seq // tq
    n_pairs = N_FOX_HEADS // 2
    qmap = lambda b, p, i: (b * n_q + i, p)
    kvmap = lambda b, p, i: (b, p)
    return pl.pallas_call(
        functools.partial(_fox_body, tq=tq, tk=tk, group=group),
        grid=(batch, n_pairs, n_q),
        in_specs=[
            pl.BlockSpec((tq, LANE), qmap),
            pl.BlockSpec((seq, LANE), kvmap),
            pl.BlockSpec((seq, LANE), kvmap),
            pl.BlockSpec((1, 1, tq, 2), lambda b, p, i: (b, p, i, 0)),
            pl.BlockSpec((1, 1, 2, seq), lambda b, p, i: (b, p, 0, 0)),
        ],
        out_specs=pl.BlockSpec((tq, LANE), qmap),
        out_shape=jax.ShapeDtypeStruct((batch * seq, N_FOX_HEADS * HEAD_DIM), MXU_DTYPE),
        scratch_shapes=[
            pltpu.VMEM((2, seq // tk, tq, tk), F32),
            pltpu.VMEM((2, tq, LANE), F32),
            pltpu.VMEM((2, tq, LANE), F32),
        ],
        compiler_params=pltpu.CompilerParams(
            dimension_semantics=("arbitrary", "arbitrary", "arbitrary"), vmem_limit_bytes=VMEM_LIMIT),
        name="fox_attention",
    )(fq, fk, fv, cq, ck)


def _post_body(on_ref, of_ref, x_ref, wo_ref, gpm_ref, gpl_ref, wu_ref, wd_ref, gpo_ref, o_ref, *, ff_chunk):
    n_nsa = on_ref.shape[1]
    mix = _dot(on_ref[...], wo_ref[0:n_nsa, :]) + _dot(of_ref[...], wo_ref[n_nsa:, :])
    x1 = x_ref[...] + _rms(mix, gpm_ref[...])
    h = _rms(x1, gpl_ref[...]).astype(MXU_DTYPE)
    d_ff = wu_ref.shape[1]
    acc = jnp.zeros(x1.shape, F32)
    for c in range(d_ff // ff_chunk):
        sl = slice(c * ff_chunk, (c + 1) * ff_chunk)
        a = jnp.square(jnp.maximum(_dot(h, wu_ref[:, sl]), 0.0))
        acc = acc + _dot(a.astype(MXU_DTYPE), wd_ref[sl, :])
    o_ref[...] = x1 + _rms(acc, gpo_ref[...])


def _post(o_nsa, o_fox, xf, wo, g_post_mix, g_pre_mlp, wu, wd, g_post_mlp, *, layer, tm, ff_chunk):
    m, d = xf.shape
    d_ff = wu.shape[2]
    row = lambda i: (i, 0)
    fixed = lambda i: (0, 0)
    resident = lambda rows, cols: pl.BlockSpec(
        (None, rows, cols), lambda i: (layer, 0, 0), pipeline_mode=pl.Buffered(1))
    return pl.pallas_call(
        functools.partial(_post_body, ff_chunk=ff_chunk),
        grid=(m // tm,),
        in_specs=[
            pl.BlockSpec((tm, o_nsa.shape[1]), row),
            pl.BlockSpec((tm, o_fox.shape[1]), row),
            pl.BlockSpec((tm, d), row),
            resident(wo.shape[1], d),
            pl.BlockSpec((1, d), fixed),
            pl.BlockSpec((1, d), fixed),
            resident(d, d_ff),
            resident(d_ff, d),
            pl.BlockSpec((1, d), fixed),
        ],
        out_specs=pl.BlockSpec((tm, d), row),
        out_shape=jax.ShapeDtypeStruct((m, d), F32),
        compiler_params=pltpu.CompilerParams(
            dimension_semantics=("arbitrary",), vmem_limit_bytes=VMEM_LIMIT),
        name="outproj_mlp",
    )(o_nsa, o_fox, xf, wo, g_post_mix, g_pre_mlp, wu, wd, g_post_mlp)


def _rope_tables(seq):
    half = ROPE_DIMS // 2
    inv_freq = ROPE_THETA ** (-jnp.arange(half, dtype=F32) * 2.0 / ROPE_DIMS)
    ang = jnp.arange(seq).astype(F32)[:, None] * inv_freq[None, :]
    cos, sin = jnp.cos(ang), jnp.sin(ang)
    pad = jnp.zeros((seq, LANE - ROPE_DIMS), F32)
    zero = jnp.zeros((seq, half), F32)
    c = jnp.concatenate([cos, cos, pad + 1.0], axis=-1)
    a = jnp.concatenate([-sin, zero, pad], axis=-1)
    b = jnp.concatenate([zero, sin, pad], axis=-1)
    return c, a, b


def _importance_map_t(seq, n_chunk):
    n_cmp = (seq - CMP_LEN) // CMP_STRIDE + 1
    n_sel = seq // SEL_BLOCK
    cs = np.arange(n_cmp)[None, :] * CMP_STRIDE
    ss = np.arange(n_sel)[:, None] * SEL_BLOCK
    overlap = np.clip(np.minimum(cs + CMP_LEN, ss + SEL_BLOCK) - np.maximum(cs, ss), 0, None) / CMP_LEN
    out = np.zeros((n_sel, n_chunk), np.float32)
    out[:, :n_cmp] = overlap
    return jnp.asarray(out, dtype=MXU_DTYPE)


def kernel(x, w_in, b_nsa_gate, b_forget, cmp_pos_k, cmp_w1_k, cmp_b1_k, cmp_w2_k, cmp_b2_k, cmp_pos_v, cmp_w1_v, cmp_b1_v, cmp_w2_v, cmp_b2_v, w_out, w_up, w_down, g_pre_mix, g_post_mix, g_pre_mlp, g_post_mlp):
    batch, seq, d_model = x.shape
    depth = w_in.shape[0]
    m = batch * seq
    n_chunk = seq // CMP_STRIDE
    n_sel = seq // SEL_BLOCK
    tm = min(512, seq)
    nsa_tq, nsa_tk = 2 * LANE, min(256, seq)
    fox_t = min(512, seq)
    assert seq % tm == 0 and seq % fox_t == 0 and seq % nsa_tk == 0
    assert n_sel <= HEAD_DIM and n_sel % 8 == 0 and seq >= WINDOW + nsa_tq
    assert x.dtype == F32

    rope_c, rope_a, rope_b = _rope_tables(seq)
    imp_t = _importance_map_t(seq, n_chunk)
    cs_chunk = min(256, seq)
    tril = jnp.asarray(np.tril(np.ones((cs_chunk, cs_chunk), np.float32)), dtype=MXU_DTYPE)

    wp = _permute_columns(w_in.astype(MXU_DTYPE), _projection_columns())

    gate_bias = jnp.zeros((depth, 1, N_NSA_KV * LANE), F32)
    fbias = jnp.zeros((depth, 1, LANE), F32)
    n_gate = NSA_GROUP * N_BRANCH
    for g in range(N_NSA_KV):
        gate_bias = gate_bias.at[:, 0, LANE * g:LANE * g + n_gate].set(b_nsa_gate[:, n_gate * g:n_gate * (g + 1)])
    fbias = fbias.at[:, 0, _FF_LANE:_FF_LANE + N_FOX_HEADS].set(b_forget)

    half = CMP_STRIDE * HEAD_DIM
    cpos = jnp.stack([cmp_pos_k.reshape(depth, 2, half), cmp_pos_v.reshape(depth, 2, half)], axis=1)
    cw1 = jnp.stack([cmp_w1_k, cmp_w1_v], axis=1).astype(MXU_DTYPE)
    cb1 = jnp.stack([cmp_b1_k, cmp_b1_v], axis=1)[:, :, None, :]
    zw = jnp.zeros_like(cmp_w2_k)
    cw2 = jnp.stack([jnp.concatenate([cmp_w2_k, zw], axis=-1),
                     jnp.concatenate([cmp_w2_v, cmp_w2_v], axis=-1)], axis=1).astype(MXU_DTYPE)
    zb = jnp.zeros_like(cmp_b2_k)
    cb2 = jnp.stack([jnp.concatenate([cmp_b2_k, zb], axis=-1),
                     jnp.concatenate([cmp_b2_v, cmp_b2_v], axis=-1)], axis=1)[:, :, None, :]

    wo = w_out.astype(MXU_DTYPE)
    wu = w_up.astype(MXU_DTYPE)
    wd = w_down.astype(MXU_DTYPE)

    xf = x.reshape(m, d_model)
    n_pairs = N_FOX_HEADS // 2
    for l in range(depth):
        qn, qr, kvc, ksa, vs, kw, vw, fq, fk, fv, small = _inproj(
            xf, g_pre_mix[l][None, :], wp, rope_c, rope_a, rope_b, layer=l, seq=seq, tm=tm)

        cum = _forget_cumsum(small, fbias[l], tril, batch=batch, seq=seq)
        cum = cum[:, _FF_LANE:_FF_LANE + N_FOX_HEADS].reshape(batch, seq, n_pairs, 2)
        cq = cum.transpose(0, 2, 1, 3)
        ck = cum.transpose(0, 2, 3, 1)

        cmp_kv = _compress(kvc.reshape(4, batch, n_chunk, half), cpos[l], cw1[l], cb1[l], cw2[l], cb2[l],
                           batch=batch, n_chunk=n_chunk)
        o_nsa = _nsa(qn, qr, cmp_kv, ksa, vs, kw, vw, small, gate_bias[l], imp_t,
                     batch=batch, seq=seq, tq=nsa_tq, tk=nsa_tk, group=(4, 2))
        o_fox = _fox(fq, fk, fv, cq, ck, batch=batch, seq=seq, tq=fox_t, tk=fox_t, group=(2,))
        xf = _post(o_nsa, o_fox, xf, wo, g_post_mix[l][None, :], g_pre_mlp[l][None, :],
                   wu, wd, g_post_mlp[l][None, :], layer=l, tm=tm, ff_chunk=1024)
    return xf.reshape(batch, seq, d_model)
```

```python
import functools

import numpy as np
import jax
import jax.numpy as jnp
from jax import lax
from jax.experimental import pallas as pl
from jax.experimental.pallas import tpu as pltpu

HEAD_DIM = 64
N_NSA_HEADS = 8
N_NSA_KV = 2
NSA_GROUP = N_NSA_HEADS // N_NSA_KV
N_FOX_HEADS = 8
N_BRANCH = 3
CMP_LEN = 32
CMP_STRIDE = 16
CMP_SHIFT = 4
CMP_HIDDEN = 256
SEL_BLOCK = 64
SEL_SHIFT = 6
SEL_TOPK = 16
WINDOW = 512
ROPE_THETA = 500000.0
ROPE_DIMS = HEAD_DIM // 4
NORM_EPS = 1e-6
NEG_INF = -1e30
FORCED_SCORE = 1e4
SCALE = HEAD_DIM ** -0.5
LOG2E = 1.4426950408889634
Q_SCALE = SCALE * LOG2E

LANE = 128
MXU_DTYPE = jnp.bfloat16
F32 = jnp.float32
VMEM_LIMIT = 56 * 1024 * 1024

_SRC_NQ, _SRC_KC, _SRC_VC, _SRC_KSEL, _SRC_VSEL, _SRC_KWIN, _SRC_VWIN = 0, 512, 640, 768, 896, 1024, 1152
_SRC_GATE, _SRC_FQ, _SRC_FK, _SRC_FV, _SRC_FF = 1280, 1304, 1816, 2328, 2840
_Q0, _KVC0, _KS0, _VS0, _KW0, _VW0, _FQ0, _FK0, _FV0, _SM0, _NP = (
    0, 1024, 1536, 1792, 2048, 2304, 2560, 3072, 3584, 4096, 4352)
_FF_LANE = 16


def _projection_columns():
    src = np.full((_NP,), -1, np.int32)
    d = np.arange(HEAD_DIM)
    for h in range(N_NSA_HEADS):
        src[_Q0 + LANE * h + d] = _SRC_NQ + HEAD_DIM * h + d
    for i in range(4):
        src[_KVC0 + LANE * i + d] = _SRC_KC + HEAD_DIM * i + d
    for g in range(N_NSA_KV):
        src[_KS0 + LANE * g + d] = _SRC_KSEL + HEAD_DIM * g + d
        src[_KW0 + LANE * g + d] = _SRC_KWIN + HEAD_DIM * g + d
        for half in range(2):
            src[_VS0 + LANE * g + HEAD_DIM * half + d] = _SRC_VSEL + HEAD_DIM * g + d
            src[_VW0 + LANE * g + HEAD_DIM * half + d] = _SRC_VWIN + HEAD_DIM * g + d
        gates = np.arange(NSA_GROUP * N_BRANCH)
        src[_SM0 + LANE * g + gates] = _SRC_GATE + NSA_GROUP * N_BRANCH * g + gates
        src[_SM0 + LANE * g + _FF_LANE + np.arange(N_FOX_HEADS)] = _SRC_FF + np.arange(N_FOX_HEADS)
    fox = np.arange(N_FOX_HEADS * HEAD_DIM)
    src[_FQ0 + fox] = _SRC_FQ + fox
    src[_FK0 + fox] = _SRC_FK + fox
    src[_FV0 + fox] = _SRC_FV + fox
    return src


def _permute_columns(w, src):
    pieces, i = [], 0
    while i < len(src):
        j = i + 1
        if src[i] < 0:
            while j < len(src) and src[j] < 0:
                j += 1
            pieces.append(jnp.zeros(w.shape[:-1] + (j - i,), w.dtype))
        else:
            while j < len(src) and src[j] == src[j - 1] + 1:
                j += 1
            pieces.append(w[..., int(src[i]):int(src[i]) + (j - i)])
        i = j
    return jnp.concatenate(pieces, axis=-1)


def _rms(x, g):
    ms = jnp.mean(x * x, axis=-1, keepdims=True)
    return x * lax.rsqrt(ms + NORM_EPS) * g


def _dot(a, b):
    return jnp.dot(a, b, preferred_element_type=F32)


def _dot_nt(a, b):
    return lax.dot_general(a, b, (((1,), (1,)), ((), ())), preferred_element_type=F32)


def _split3(x):
    hi = x.astype(MXU_DTYPE)
    r1 = x - hi.astype(F32)
    mid = r1.astype(MXU_DTYPE)
    lo = (r1 - mid.astype(F32)).astype(MXU_DTYPE)
    return hi, mid, lo


def _rows(shape):
    return lax.broadcasted_iota(jnp.int32, shape, 0)


def _lanes(shape):
    return lax.broadcasted_iota(jnp.int32, shape, 1)


def _inproj_body(x_ref, g_ref, w_ref, rc_ref, ra_ref, rb_ref,
                 qn_ref, qr_ref, kvc_ref, ksa_ref, vs_ref, kw_ref, vw_ref,
                 fq_ref, fk_ref, fv_ref, small_ref, *, tm, n_s_tiles):
    h = _rms(x_ref[...], g_ref[...]).astype(MXU_DTYPE)
    rc, ra, rb = rc_ref[...], ra_ref[...], rb_ref[...]

    def mm(lo, n):
        return _dot(h, w_ref[:, lo:lo + n])

    def rope(v):
        return v * rc + pltpu.roll(v, LANE - ROPE_DIMS // 2, 1) * ra + pltpu.roll(v, ROPE_DIMS // 2, 1) * rb

    q = mm(_Q0, N_NSA_HEADS * LANE) * Q_SCALE
    qn_ref[...] = q.astype(MXU_DTYPE)
    for hh in range(N_NSA_HEADS):
        sl = slice(LANE * hh, LANE * (hh + 1))
        qr_ref[:, sl] = rope(q[:, sl]).astype(MXU_DTYPE)

    kvc = mm(_KVC0, 4 * LANE)
    for i in range(4):
        kvc_ref[i] = kvc[:, LANE * i:LANE * i + HEAD_DIM]

    pos = (pl.program_id(0) % n_s_tiles) * tm + _rows((tm, LANE))
    onehot = ((_lanes((tm, LANE)) - HEAD_DIM) == (pos >> SEL_SHIFT)).astype(F32)
    ks = mm(_KS0, N_NSA_KV * LANE)
    kw = mm(_KW0, N_NSA_KV * LANE)
    for g in range(N_NSA_KV):
        sl = slice(LANE * g, LANE * (g + 1))
        ksa_ref[:, sl] = (rope(ks[:, sl]) + onehot).astype(MXU_DTYPE)
        kw_ref[:, sl] = rope(kw[:, sl]).astype(MXU_DTYPE)
    vs_ref[...] = mm(_VS0, N_NSA_KV * LANE).astype(MXU_DTYPE)
    vw_ref[...] = mm(_VW0, N_NSA_KV * LANE).astype(MXU_DTYPE)

    nf = N_FOX_HEADS * HEAD_DIM
    fq_ref[...] = (mm(_FQ0, nf) * Q_SCALE).astype(MXU_DTYPE)
    fk_ref[...] = mm(_FK0, nf).astype(MXU_DTYPE)
    fv_ref[...] = mm(_FV0, nf).astype(MXU_DTYPE)
    small_ref[...] = mm(_SM0, N_NSA_KV * LANE)


def _inproj(xf, g, wp, rope_c, rope_a, rope_b, *, layer, seq, tm):
    m, d = xf.shape
    n_s_tiles = seq // tm
    nf = N_FOX_HEADS * HEAD_DIM
    row = lambda i: (i, 0)
    fixed = lambda i: (0, 0)
    tab = lambda i: (i % n_s_tiles, 0)
    out_shape = (
        jax.ShapeDtypeStruct((m, N_NSA_HEADS * LANE), MXU_DTYPE),
        jax.ShapeDtypeStruct((m, N_NSA_HEADS * LANE), MXU_DTYPE),
        jax.ShapeDtypeStruct((4, m, HEAD_DIM), F32),
        jax.ShapeDtypeStruct((m, N_NSA_KV * LANE), MXU_DTYPE),
        jax.ShapeDtypeStruct((m, N_NSA_KV * LANE), MXU_DTYPE),
        jax.ShapeDtypeStruct((m, N_NSA_KV * LANE), MXU_DTYPE),
        jax.ShapeDtypeStruct((m, N_NSA_KV * LANE), MXU_DTYPE),
        jax.ShapeDtypeStruct((m, nf), MXU_DTYPE),
        jax.ShapeDtypeStruct((m, nf), MXU_DTYPE),
        jax.ShapeDtypeStruct((m, nf), MXU_DTYPE),
        jax.ShapeDtypeStruct((m, N_NSA_KV * LANE), F32),
    )
    out_specs = (
        pl.BlockSpec((tm, N_NSA_HEADS * LANE), row),
        pl.BlockSpec((tm, N_NSA_HEADS * LANE), row),
        pl.BlockSpec((4, tm, HEAD_DIM), lambda i: (0, i, 0)),
        pl.BlockSpec((tm, N_NSA_KV * LANE), row),
        pl.BlockSpec((tm, N_NSA_KV * LANE), row),
        pl.BlockSpec((tm, N_NSA_KV * LANE), row),
        pl.BlockSpec((tm, N_NSA_KV * LANE), row),
        pl.BlockSpec((tm, nf), row),
        pl.BlockSpec((tm, nf), row),
        pl.BlockSpec((tm, nf), row),
        pl.BlockSpec((tm, N_NSA_KV * LANE), row),
    )
    return pl.pallas_call(
        functools.partial(_inproj_body, tm=tm, n_s_tiles=n_s_tiles),
        grid=(m // tm,),
        in_specs=[
            pl.BlockSpec((tm, d), row),
            pl.BlockSpec((1, d), fixed),
            pl.BlockSpec((None, d, _NP), lambda i: (layer, 0, 0)),
            pl.BlockSpec((tm, LANE), tab),
            pl.BlockSpec((tm, LANE), tab),
            pl.BlockSpec((tm, LANE), tab),
        ],
        out_specs=out_specs,
        out_shape=out_shape,
        compiler_params=pltpu.CompilerParams(
            dimension_semantics=("arbitrary",), vmem_limit_bytes=VMEM_LIMIT),
        name="inproj",
    )(xf, g, wp, rope_c, rope_a, rope_b)


def _cumsum_body(sm_ref, bias_ref, tril_ref, out_ref, *, seq, chunk):
    carry = jnp.zeros((1, LANE), F32)
    tril = tril_ref[...]
    for c in range(seq // chunk):
        sl = slice(c * chunk, (c + 1) * chunk)
        z = sm_ref[sl, :] + bias_ref[...]
        logf = jnp.minimum(z, 0.0) - jnp.log1p(jnp.exp(-jnp.abs(z)))
        hi, mid, lo = _split3(logf)
        cs = (_dot(tril, hi) + _dot(tril, mid)) + _dot(tril, lo) + carry
        out_ref[sl, :] = cs * LOG2E
        carry = cs[chunk - 1:chunk, :]


def _forget_cumsum(small, fbias, tril, *, batch, seq):
    chunk = tril.shape[0]
    return pl.pallas_call(
        functools.partial(_cumsum_body, seq=seq, chunk=chunk),
        grid=(batch,),
        in_specs=[
            pl.BlockSpec((seq, LANE), lambda b: (b, 0)),
            pl.BlockSpec((1, LANE), lambda b: (0, 0)),
            pl.BlockSpec((chunk, chunk), lambda b: (0, 0)),
        ],
        out_specs=pl.BlockSpec((seq, LANE), lambda b: (b, 0)),
        out_shape=jax.ShapeDtypeStruct((batch * seq, LANE), F32),
        compiler_params=pltpu.CompilerParams(
            dimension_semantics=("arbitrary",), vmem_limit_bytes=VMEM_LIMIT),
        name="forget_cumsum",
    )(small, fbias, tril)


def _compress_body(x_ref, pos_ref, w1_ref, b1_ref, w2_ref, b2_ref, o_ref, *, n_chunk):
    half = CMP_STRIDE * HEAD_DIM
    x = jnp.concatenate(
        [x_ref[0, pl.ds(t, n_chunk, stride=CMP_STRIDE), :] for t in range(CMP_STRIDE)], axis=1)
    first = _dot((x + pos_ref[0, 0:1, :]).astype(MXU_DTYPE), w1_ref[0, 0:half, :])
    second = _dot((x + pos_ref[0, 1:2, :]).astype(MXU_DTYPE), w1_ref[0, half:2 * half, :])
    hid = first + pltpu.roll(second, n_chunk - 1, 0) + b1_ref[0]
    act = jax.nn.gelu(hid)
    o_ref[0, 0] = (_dot(act.astype(MXU_DTYPE), w2_ref[0]) + b2_ref[0]).astype(MXU_DTYPE)


def _compress(kvc, pos, w1, b1, w2p, b2p, *, batch, n_chunk):
    half = CMP_STRIDE * HEAD_DIM
    kv = lambda i, b: (i // N_NSA_KV, 0, 0)
    return pl.pallas_call(
        functools.partial(_compress_body, n_chunk=n_chunk),
        grid=(2 * N_NSA_KV, batch),
        in_specs=[
            pl.BlockSpec((1, n_chunk * CMP_STRIDE, HEAD_DIM), lambda i, b: (i, b, 0)),
            pl.BlockSpec((1, 2, half), kv),
            pl.BlockSpec((1, 2 * half, CMP_HIDDEN), kv),
            pl.BlockSpec((1, 1, CMP_HIDDEN), kv),
            pl.BlockSpec((1, CMP_HIDDEN, LANE), kv),
            pl.BlockSpec((1, 1, LANE), kv),
        ],
        out_specs=pl.BlockSpec((1, 1, n_chunk, LANE), lambda i, b: (i, b, 0, 0)),
        out_shape=jax.ShapeDtypeStruct((2 * N_NSA_KV, batch, n_chunk, LANE), MXU_DTYPE),
        compiler_params=pltpu.CompilerParams(
            dimension_semantics=("arbitrary", "arbitrary"), vmem_limit_bytes=VMEM_LIMIT),
        name="compress",
    )(kvc, pos, w1, b1, w2p, b2p)


def _lane_max(s):
    out = s[:, 0:LANE]
    for c in range(1, s.shape[1] // LANE):
        out = jnp.maximum(out, s[:, LANE * c:LANE * (c + 1)])
    return out


def _row_max_lanes(mrun):
    return jnp.broadcast_to(jnp.max(mrun, axis=-1, keepdims=True), mrun.shape)


def _exp_tiles(s, mb):
    n = s.shape[1] // LANE
    return jnp.concatenate(
        [jnp.exp2(s[:, LANE * c:LANE * (c + 1)] - mb).astype(MXU_DTYPE) for c in range(n)], axis=1)


def _values_with_ones(v, ones_low):
    low = _lanes(v.shape) < HEAD_DIM
    ones = jnp.ones_like(v)
    return jnp.where(low, ones, v) if ones_low else jnp.where(low, v, ones)


def _normalize(acc):
    return acc / pltpu.roll(acc, HEAD_DIM, 1)


def _spans(n_tiles, groups, fn):
    start = 0
    for size in tuple(groups) + (1,):
        n_groups = lax.div(n_tiles - start, size)

        def step(t, carry, size=size, start=start):
            fn(start + t * size, size)
            return carry

        lax.fori_loop(0, n_groups, step, 0)
        start = start + n_groups * size


def _nsa_body(qn_ref, qr_ref, kc_ref, vc_ref, ksa_ref, vs_ref, kw_ref, vw_ref, sm_ref, gb_ref, impt_ref,
              o_ref, s_ref, mrun_ref, acc_ref, *, tq, tk, group, n_sel, top_k):
    q0 = pl.program_id(2) * tq
    r = NSA_GROUP * tq
    n_cmp = kc_ref.shape[2]

    def window_scores(hq):
        n_wc = WINDOW // LANE + 1
        upper = _lanes((NSA_GROUP * LANE, LANE)) > (_rows((NSA_GROUP * LANE, LANE)) & (LANE - 1))
        qs = q0 + hq * LANE
        qh = jnp.concatenate(
            [qr_ref[hq * LANE:(hq + 1) * LANE, LANE * h:LANE * (h + 1)] for h in range(NSA_GROUP)], axis=0)
        starts = [pl.multiple_of(jnp.maximum(qs - WINDOW + LANE * c, 0), LANE) for c in range(n_wc)]
        kk = jnp.concatenate([kw_ref[pl.ds(st, LANE), :] for st in starts], axis=0)
        vv = jnp.concatenate([vw_ref[pl.ds(st, LANE), :] for st in starts], axis=0)
        sw = _dot_nt(qh, kk)
        chunks = [sw[:, LANE * c:LANE * (c + 1)] for c in range(n_wc)]
        chunks[0] = jnp.where(upper, chunks[0], NEG_INF)
        chunks[-1] = jnp.where(upper, NEG_INF, chunks[-1])
        for c in range(n_wc - 1):
            chunks[c] = jnp.where(qs - WINDOW + LANE * c >= 0, chunks[c], NEG_INF)
        return chunks, vv

    def window_probs(chunks):
        mb = _row_max_lanes(functools.reduce(jnp.maximum, chunks))
        return jnp.concatenate([jnp.exp2(ch - mb).astype(MXU_DTYPE) for ch in chunks], axis=1)

    def window_out(pw, vv):
        return _normalize(_dot(pw, _values_with_ones(vv, False)))

    q4n = jnp.concatenate([qn_ref[:, LANE * h:LANE * (h + 1)] for h in range(NSA_GROUP)], axis=0)
    sc = _dot_nt(q4n, kc_ref[0, 0])
    win_scores = [window_scores(0), window_scores(1)]
    last_vis = (q0 + _rows((tq, LANE)) - (CMP_LEN - 1)) >> CMP_SHIFT
    last_vis = jnp.concatenate([last_vis] * NSA_GROUP, axis=0)
    sc_chunks = [
        jnp.where(_lanes((r, LANE)) + LANE * c <= last_vis, sc[:, LANE * c:LANE * (c + 1)], NEG_INF)
        for c in range(n_cmp // LANE)]
    mb = jnp.maximum(_row_max_lanes(functools.reduce(jnp.maximum, sc_chunks)), 0.1 * NEG_INF)
    e_chunks = [jnp.exp2(ch - mb) for ch in sc_chunks]
    den = jnp.sum(functools.reduce(jnp.add, e_chunks), axis=-1, keepdims=True)
    inv = 1.0 / jnp.maximum(jnp.broadcast_to(den, (r, LANE)), 1e-30)
    pc = jnp.concatenate([ech * inv for ech in e_chunks], axis=1)
    o_cmp = _dot(pc.astype(MXU_DTYPE), vc_ref[0, 0])

    pcs = (pc[0:tq] + pc[tq:2 * tq]) + (pc[2 * tq:3 * tq] + pc[3 * tq:4 * tq])
    impt_w = impt_ref[...]
    hi, mid, lo = _split3(pcs)
    imp_t = (_dot_nt(impt_w, hi) + _dot_nt(impt_w, mid)) + _dot_nt(impt_w, lo)
    jrow = _rows((n_sel, tq))
    qpos = q0 + _lanes((n_sel, tq))
    cur = qpos >> SEL_SHIFT
    valid = jrow * SEL_BLOCK <= qpos
    forced = (jrow == 0) | (jrow == cur) | (jrow == cur - 1)
    score = jnp.where(forced, FORCED_SCORE, jnp.where(valid, imp_t, -1.0))

    sub = 8
    chunks = [score[sub * c:sub * (c + 1), :] for c in range(n_sel // sub)]
    ranks = [jnp.zeros((sub, tq), F32) for _ in chunks]
    later = _rows((sub, tq))

    win_probs = [window_probs(win_scores[0][0])]
    win_blocks = []
    for jp in range(n_sel):
        if jp == n_sel // 2:
            win_blocks.append(window_out(win_probs[0], win_scores[0][1]))
            win_probs.append(window_probs(win_scores[1][0]))
        cj, rj = divmod(jp, sub)
        rowv = chunks[cj][rj:rj + 1, :]
        for c, sc_c in enumerate(chunks):
            if c > cj:
                beats = rowv >= sc_c
            elif c < cj:
                beats = rowv > sc_c
            else:
                beats = (rowv > sc_c) | ((rowv == sc_c) & (later > rj))
            ranks[c] = ranks[c] + jnp.where(beats, 1.0, 0.0)
    sel_t = jnp.where(jnp.concatenate(ranks, axis=0) < float(top_k), 1.0, 0.0)
    sel_pad = jnp.concatenate([sel_t, jnp.zeros((LANE - n_sel, tq), F32)], axis=0)
    sel_q = pltpu.roll(sel_pad.T, HEAD_DIM, 1)
    sel_bias = ((sel_q - 1.0) * (-NEG_INF)).astype(MXU_DTYPE)

    high = _lanes((tq, LANE)) >= HEAD_DIM
    q4r = jnp.concatenate(
        [jnp.where(high, sel_bias, qr_ref[:, LANE * h:LANE * (h + 1)]) for h in range(NSA_GROUP)], axis=0)

    win_blocks.append(window_out(win_probs[1], win_scores[1][1]))
    o_win = jnp.concatenate([blk[h * LANE:(h + 1) * LANE] for h in range(NSA_GROUP) for blk in win_blocks], axis=0)

    gate = jax.nn.sigmoid(sm_ref[...] + gb_ref[...])
    partial, gate_sel = [], []
    for h in range(NSA_GROUP):
        rs = slice(h * tq, (h + 1) * tq)
        c = N_BRANCH * h
        partial.append(gate[:, c:c + 1] * o_cmp[rs] + gate[:, c + 2:c + 3] * o_win[rs])
        gate_sel.append(jnp.broadcast_to(gate[:, c + 1:c + 2], (tq, LANE)))

    rowq_k = q0 + (_rows((r, tk)) & (tq - 1))
    n_full = lax.div(q0, tk)

    def score_span(kb, n_tiles, masked):
        k0 = pl.multiple_of(kb * tk, tk)
        s = _dot_nt(q4r, ksa_ref[pl.ds(k0, n_tiles * tk), :])
        if masked:
            s = jnp.where(k0 + _lanes((r, tk)) <= rowq_k, s, NEG_INF)
        for j in range(n_tiles):
            s_ref[kb + j] = s[:, j * tk:(j + 1) * tk]
        mrun_ref[...] = jnp.maximum(mrun_ref[...], _lane_max(s))

    def value_span(kb, n_tiles):
        k0 = pl.multiple_of(kb * tk, tk)
        mb = mrun_ref[...]
        p = jnp.concatenate([_exp_tiles(s_ref[kb + j], mb) for j in range(n_tiles)], axis=1)
        acc_ref[...] += _dot(p, _values_with_ones(vs_ref[pl.ds(k0, n_tiles * tk), :], False))

    mrun_ref[...] = jnp.full((r, LANE), NEG_INF, F32)
    _spans(n_full, group, lambda kb, n: score_span(kb, n, False))
    score_span(n_full, 1, True)
    mrun_ref[...] = _row_max_lanes(mrun_ref[...])
    acc_ref[...] = jnp.zeros((r, LANE), F32)
    _spans(n_full + 1, group, value_span)
    o_sel = _normalize(acc_ref[...])

    heads = [partial[h] + gate_sel[h] * o_sel[h * tq:(h + 1) * tq] for h in range(NSA_GROUP)]
    low = _lanes((tq, LANE)) < HEAD_DIM
    for p in range(NSA_GROUP // 2):
        pair = jnp.where(low, heads[2 * p], pltpu.roll(heads[2 * p + 1], HEAD_DIM, 1))
        o_ref[:, LANE * p:LANE * (p + 1)] = pair.astype(MXU_DTYPE)


def _nsa(qn, qr, cmp_kv, ksa, vs, kw, vw, small, gbias, imp_t, *, batch, seq, tq, tk, group):
    assert tq == 2 * LANE
    n_q = seq // tq
    n_cmp = cmp_kv.shape[2]
    n_sel = seq // SEL_BLOCK
    top_k = min(SEL_TOPK, n_sel)
    gw = NSA_GROUP * LANE
    qmap = lambda b, g, i: (b * n_q + i, g)
    kvmap = lambda b, g, i: (b, g)
    r = NSA_GROUP * tq
    return pl.pallas_call(
        functools.partial(_nsa_body, tq=tq, tk=tk, group=group, n_sel=n_sel, top_k=top_k),
        grid=(batch, N_NSA_KV, n_q),
        in_specs=[
            pl.BlockSpec((tq, gw), qmap),
            pl.BlockSpec((tq, gw), qmap),
            pl.BlockSpec((1, 1, n_cmp, LANE), lambda b, g, i: (g, b, 0, 0)),
            pl.BlockSpec((1, 1, n_cmp, LANE), lambda b, g, i: (N_NSA_KV + g, b, 0, 0)),
            pl.BlockSpec((seq, LANE), kvmap),
            pl.BlockSpec((seq, LANE), kvmap),
            pl.BlockSpec((seq, LANE), kvmap),
            pl.BlockSpec((seq, LANE), kvmap),
            pl.BlockSpec((tq, LANE), qmap),
            pl.BlockSpec((1, LANE), lambda b, g, i: (0, g)),
            pl.BlockSpec((n_sel, n_cmp), lambda b, g, i: (0, 0)),
        ],
        out_specs=pl.BlockSpec((tq, NSA_GROUP * HEAD_DIM), qmap),
        out_shape=jax.ShapeDtypeStruct((batch * seq, N_NSA_HEADS * HEAD_DIM), MXU_DTYPE),
        scratch_shapes=[
            pltpu.VMEM((seq // tk, r, tk), F32),
            pltpu.VMEM((r, LANE), F32),
            pltpu.VMEM((r, LANE), F32),
        ],
        compiler_params=pltpu.CompilerParams(
            dimension_semantics=("arbitrary", "arbitrary", "arbitrary"), vmem_limit_bytes=VMEM_LIMIT),
        name="nsa_attention",
    )(qn, qr, cmp_kv, cmp_kv, ksa, vs, kw, vw, small, gbias, imp_t)


def _fox_body(q_ref, k_ref, v_ref, cq_ref, ck_ref, o_ref, s_ref, mrun_ref, acc_ref, *, tq, tk, group):
    i = pl.program_id(2)
    q0 = i * tq
    qp = q_ref[...]
    low = _lanes((tq, LANE)) < HEAD_DIM
    causal_rows = q0 + _rows((tq, tk))
    n_full = lax.div(q0, tk)
    qs, cqs = [], []
    for e in range(2):
        qs.append(jnp.where(low if e == 0 else jnp.logical_not(low), qp, jnp.zeros_like(qp)))
        cq = jnp.broadcast_to(cq_ref[0, 0, :, e:e + 1], (tq, LANE))
        cqs.append(jnp.concatenate([cq] * (tk // LANE), axis=1))

    def score_span(kb, n_tiles, masked):
        k0 = pl.multiple_of(kb * tk, tk)
        k = k_ref[pl.ds(k0, n_tiles * tk), :]
        for e in range(2):
            s = _dot_nt(qs[e], k)
            for j in range(n_tiles):
                sj = s[:, j * tk:(j + 1) * tk] + cqs[e] - ck_ref[0, 0, e:e + 1, pl.ds(k0 + j * tk, tk)]
                if masked:
                    sj = jnp.where(k0 + _lanes((tq, tk)) <= causal_rows, sj, NEG_INF)
                s_ref[e, kb + j] = sj
                mrun_ref[e] = jnp.maximum(mrun_ref[e], _lane_max(sj))

    def value_span(kb, n_tiles):
        k0 = pl.multiple_of(kb * tk, tk)
        v = v_ref[pl.ds(k0, n_tiles * tk), :]
        for e in range(2):
            mb = mrun_ref[e]
            p = jnp.concatenate([_exp_tiles(s_ref[e, kb + j], mb) for j in range(n_tiles)], axis=1)
            acc_ref[e] += _dot(p, _values_with_ones(v, e == 1))

    mrun_ref[...] = jnp.full((2, tq, LANE), NEG_INF, F32)
    _spans(n_full, group, lambda kb, n: score_span(kb, n, False))
    score_span(n_full, 1, True)
    for e in range(2):
        mrun_ref[e] = _row_max_lanes(mrun_ref[e])
    acc_ref[...] = jnp.zeros((2, tq, LANE), F32)
    _spans(n_full + 1, group, value_span)
    o_ref[...] = jnp.where(low, _normalize(acc_ref[0]), _normalize(acc_ref[1])).astype(MXU_DTYPE)


def _fox(fq, fk, fv, cq, ck, *, batch, seq, tq, tk, group):
    n_q = seq // tq
    n_pairs = N_FOX_HEADS // 2
    qmap = lambda b, p, i: (b * n_q + i, p)
    kvmap = lambda b, p, i: (b, p)
    return pl.pallas_call(
        functools.partial(_fox_body, tq=tq, tk=tk, group=group),
        grid=(batch, n_pairs, n_q),
        in_specs=[
            pl.BlockSpec((tq, LANE), qmap),
            pl.BlockSpec((seq, LANE), kvmap),
            pl.BlockSpec((seq, LANE), kvmap),
            pl.BlockSpec((1, 1, tq, 2), lambda b, p, i: (b, p, i, 0)),
            pl.BlockSpec((1, 1, 2, seq), lambda b, p, i: (b, p, 0, 0)),
        ],
        out_specs=pl.BlockSpec((tq, LANE), qmap),
        out_shape=jax.ShapeDtypeStruct((batch * seq, N_FOX_HEADS * HEAD_DIM), MXU_DTYPE),
        scratch_shapes=[
            pltpu.VMEM((2, seq // tk, tq, tk), F32),
            pltpu.VMEM((2, tq, LANE), F32),
            pltpu.VMEM((2, tq, LANE), F32),
        ],
        compiler_params=pltpu.CompilerParams(
            dimension_semantics=("arbitrary", "arbitrary", "arbitrary"), vmem_limit_bytes=VMEM_LIMIT),
        name="fox_attention",
    )(fq, fk, fv, cq, ck)


def _post_body(on_ref, of_ref, x_ref, wo_ref, gpm_ref, gpl_ref, wu_ref, wd_ref, gpo_ref, o_ref, *, ff_chunk):
    n_nsa = on_ref.shape[1]
    mix = _dot(on_ref[...], wo_ref[0:n_nsa, :]) + _dot(of_ref[...], wo_ref[n_nsa:, :])
    x1 = x_ref[...] + _rms(mix, gpm_ref[...])
    h = _rms(x1, gpl_ref[...]).astype(MXU_DTYPE)
    d_ff = wu_ref.shape[1]
    acc = jnp.zeros(x1.shape, F32)
    for c in range(d_ff // ff_chunk):
        sl = slice(c * ff_chunk, (c + 1) * ff_chunk)
        a = jnp.square(jnp.maximum(_dot(h, wu_ref[:, sl]), 0.0))
        acc = acc + _dot(a.astype(MXU_DTYPE), wd_ref[sl, :])
    o_ref[...] = x1 + _rms(acc, gpo_ref[...])


def _post(o_nsa, o_fox, xf, wo, g_post_mix, g_pre_mlp, wu, wd, g_post_mlp, *, layer, tm, ff_chunk):
    m, d = xf.shape
    d_ff = wu.shape[2]
    row = lambda i: (i, 0)
    fixed = lambda i: (0, 0)
    resident = lambda rows, cols: pl.BlockSpec(
        (None, rows, cols), lambda i: (layer, 0, 0), pipeline_mode=pl.Buffered(1))
    return pl.pallas_call(
        functools.partial(_post_body, ff_chunk=ff_chunk),
        grid=(m // tm,),
        in_specs=[
            pl.BlockSpec((tm, o_nsa.shape[1]), row),
            pl.BlockSpec((tm, o_fox.shape[1]), row),
            pl.BlockSpec((tm, d), row),
            resident(wo.shape[1], d),
            pl.BlockSpec((1, d), fixed),
            pl.BlockSpec((1, d), fixed),
            resident(d, d_ff),
            resident(d_ff, d),
            pl.BlockSpec((1, d), fixed),
        ],
        out_specs=pl.BlockSpec((tm, d), row),
        out_shape=jax.ShapeDtypeStruct((m, d), F32),
        compiler_params=pltpu.CompilerParams(
            dimension_semantics=("arbitrary",), vmem_limit_bytes=VMEM_LIMIT),
        name="outproj_mlp",
    )(o_nsa, o_fox, xf, wo, g_post_mix, g_pre_mlp, wu, wd, g_post_mlp)


def _rope_tables(seq):
    half = ROPE_DIMS // 2
    inv_freq = ROPE_THETA ** (-jnp.arange(half, dtype=F32) * 2.0 / ROPE_DIMS)
    ang = jnp.arange(seq).astype(F32)[:, None] * inv_freq[None, :]
    cos, sin = jnp.cos(ang), jnp.sin(ang)
    pad = jnp.zeros((seq, LANE - ROPE_DIMS), F32)
    zero = jnp.zeros((seq, half), F32)
    c = jnp.concatenate([cos, cos, pad + 1.0], axis=-1)
    a = jnp.concatenate([-sin, zero, pad], axis=-1)
    b = jnp.concatenate([zero, sin, pad], axis=-1)
    return c, a, b


def _importance_map_t(seq, n_chunk):
    n_cmp = (seq - CMP_LEN) // CMP_STRIDE + 1
    n_sel = seq // SEL_BLOCK
    cs = np.arange(n_cmp)[None, :] * CMP_STRIDE
    ss = np.arange(n_sel)[:, None] * SEL_BLOCK
    overlap = np.clip(np.minimum(cs + CMP_LEN, ss + SEL_BLOCK) - np.maximum(cs, ss), 0, None) / CMP_LEN
    out = np.zeros((n_sel, n_chunk), np.float32)
    out[:, :n_cmp] = overlap
    return jnp.asarray(out, dtype=MXU_DTYPE)


def kernel(x, w_in, b_nsa_gate, b_forget, cmp_pos_k, cmp_w1_k, cmp_b1_k, cmp_w2_k, cmp_b2_k, cmp_pos_v, cmp_w1_v, cmp_b1_v, cmp_w2_v, cmp_b2_v, w_out, w_up, w_down, g_pre_mix, g_post_mix, g_pre_mlp, g_post_mlp):
    batch, seq, d_model = x.shape
    depth = w_in.shape[0]
    m = batch * seq
    n_chunk = seq // CMP_STRIDE
    n_sel = seq // SEL_BLOCK
    tm = min(512, seq)
    nsa_tq, nsa_tk = 2 * LANE, min(256, seq)
    fox_t = min(512, seq)
    assert seq % tm == 0 and seq % fox_t == 0 and seq % nsa_tk == 0
    assert n_sel <= HEAD_DIM and n_sel % 8 == 0 and seq >= WINDOW + nsa_tq
    assert x.dtype == F32

    rope_c, rope_a, rope_b = _rope_tables(seq)
    imp_t = _importance_map_t(seq, n_chunk)
    cs_chunk = min(256, seq)
    tril = jnp.asarray(np.tril(np.ones((cs_chunk, cs_chunk), np.float32)), dtype=MXU_DTYPE)

    wp = _permute_columns(w_in.astype(MXU_DTYPE), _projection_columns())

    gate_bias = jnp.zeros((depth, 1, N_NSA_KV * LANE), F32)
    fbias = jnp.zeros((depth, 1, LANE), F32)
    n_gate = NSA_GROUP * N_BRANCH
    for g in range(N_NSA_KV):
        gate_bias = gate_bias.at[:, 0, LANE * g:LANE * g + n_gate].set(b_nsa_gate[:, n_gate * g:n_gate * (g + 1)])
    fbias = fbias.at[:, 0, _FF_LANE:_FF_LANE + N_FOX_HEADS].set(b_forget)

    half = CMP_STRIDE * HEAD_DIM
    cpos = jnp.stack([cmp_pos_k.reshape(depth, 2, half), cmp_pos_v.reshape(depth, 2, half)], axis=1)
    cw1 = jnp.stack([cmp_w1_k, cmp_w1_v], axis=1).astype(MXU_DTYPE)
    cb1 = jnp.stack([cmp_b1_k, cmp_b1_v], axis=1)[:, :, None, :]
    zw = jnp.zeros_like(cmp_w2_k)
    cw2 = jnp.stack([jnp.concatenate([cmp_w2_k, zw], axis=-1),
                     jnp.concatenate([cmp_w2_v, cmp_w2_v], axis=-1)], axis=1).astype(MXU_DTYPE)
    zb = jnp.zeros_like(cmp_b2_k)
    cb2 = jnp.stack([jnp.concatenate([cmp_b2_k, zb], axis=-1),
                     jnp.concatenate([cmp_b2_v, cmp_b2_v], axis=-1)], axis=1)[:, :, None, :]

    wo = w_out.astype(MXU_DTYPE)
    wu = w_up.astype(MXU_DTYPE)
    wd = w_down.astype(MXU_DTYPE)

    xf = x.reshape(m, d_model)
    n_pairs = N_FOX_HEADS // 2
    for l in range(depth):
        qn, qr, kvc, ksa, vs, kw, vw, fq, fk, fv, small = _inproj(
            xf, g_pre_mix[l][None, :], wp, rope_c, rope_a, rope_b, layer=l, seq=seq, tm=tm)

        cum = _forget_cumsum(small, fbias[l], tril, batch=batch, seq=seq)
        cum = cum[:, _FF_LANE:_FF_LANE + N_FOX_HEADS].reshape(batch, seq, n_pairs, 2)
        cq = cum.transpose(0, 2, 1, 3)
        ck = cum.transpose(0, 2, 3, 1)

        cmp_kv = _compress(kvc, cpos[l], cw1[l], cb1[l], cw2[l], cb2[l],
                           batch=batch, n_chunk=n_chunk)
        o_nsa = _nsa(qn, qr, cmp_kv, ksa, vs, kw, vw, small, gate_bias[l], imp_t,
                     batch=batch, seq=seq, tq=nsa_tq, tk=nsa_tk, group=(4, 2))
        o_fox = _fox(fq, fk, fv, cq, ck, batch=batch, seq=seq, tq=fox_t, tk=fox_t, group=(4, 2))
        xf = _post(o_nsa, o_fox, xf, wo, g_post_mix[l][None, :], g_pre_mlp[l][None, :],
                   wu, wd, g_post_mlp[l][None, :], layer=l, tm=tm, ff_chunk=1024)
    return xf.reshape(batch, seq, d_model)
```

```python
import functools

import numpy as np
import jax
import jax.numpy as jnp
from jax import lax
from jax.experimental import pallas as pl
from jax.experimental.pallas import tpu as pltpu

HEAD_DIM = 64
N_NSA_HEADS = 8
N_NSA_KV = 2
NSA_GROUP = N_NSA_HEADS // N_NSA_KV
N_FOX_HEADS = 8
N_BRANCH = 3
CMP_LEN = 32
CMP_STRIDE = 16
CMP_SHIFT = 4
CMP_HIDDEN = 256
SEL_BLOCK = 64
SEL_SHIFT = 6
SEL_TOPK = 16
WINDOW = 512
ROPE_THETA = 500000.0
ROPE_DIMS = HEAD_DIM // 4
NORM_EPS = 1e-6
NEG_INF = -1e30
FORCED_SCORE = 1e4
SCALE = HEAD_DIM ** -0.5
LOG2E = 1.4426950408889634
Q_SCALE = SCALE * LOG2E

LANE = 128
MXU_DTYPE = jnp.bfloat16
F32 = jnp.float32
VMEM_LIMIT = 56 * 1024 * 1024

_SRC_NQ, _SRC_KC, _SRC_VC, _SRC_KSEL, _SRC_VSEL, _SRC_KWIN, _SRC_VWIN = 0, 512, 640, 768, 896, 1024, 1152
_SRC_GATE, _SRC_FQ, _SRC_FK, _SRC_FV, _SRC_FF = 1280, 1304, 1816, 2328, 2840
_Q0, _KVC0, _KS0, _VS0, _KW0, _VW0, _FQ0, _FK0, _FV0, _SM0, _NP = (
    0, 1024, 1536, 1792, 2048, 2304, 2560, 3072, 3584, 4096, 4352)
_FF_LANE = 16


def _projection_columns():
    src = np.full((_NP,), -1, np.int32)
    d = np.arange(HEAD_DIM)
    for h in range(N_NSA_HEADS):
        src[_Q0 + LANE * h + d] = _SRC_NQ + HEAD_DIM * h + d
    for i in range(4):
        src[_KVC0 + LANE * i + d] = _SRC_KC + HEAD_DIM * i + d
    for g in range(N_NSA_KV):
        src[_KS0 + LANE * g + d] = _SRC_KSEL + HEAD_DIM * g + d
        src[_KW0 + LANE * g + d] = _SRC_KWIN + HEAD_DIM * g + d
        for half in range(2):
            src[_VS0 + LANE * g + HEAD_DIM * half + d] = _SRC_VSEL + HEAD_DIM * g + d
            src[_VW0 + LANE * g + HEAD_DIM * half + d] = _SRC_VWIN + HEAD_DIM * g + d
        gates = np.arange(NSA_GROUP * N_BRANCH)
        src[_SM0 + LANE * g + gates] = _SRC_GATE + NSA_GROUP * N_BRANCH * g + gates
        src[_SM0 + LANE * g + _FF_LANE + np.arange(N_FOX_HEADS)] = _SRC_FF + np.arange(N_FOX_HEADS)
    fox = np.arange(N_FOX_HEADS * HEAD_DIM)
    src[_FQ0 + fox] = _SRC_FQ + fox
    src[_FK0 + fox] = _SRC_FK + fox
    src[_FV0 + fox] = _SRC_FV + fox
    return src


def _permute_columns(w, src):
    pieces, i = [], 0
    while i < len(src):
        j = i + 1
        if src[i] < 0:
            while j < len(src) and src[j] < 0:
                j += 1
            pieces.append(jnp.zeros(w.shape[:-1] + (j - i,), w.dtype))
        else:
            while j < len(src) and src[j] == src[j - 1] + 1:
                j += 1
            pieces.append(w[..., int(src[i]):int(src[i]) + (j - i)])
        i = j
    return jnp.concatenate(pieces, axis=-1)


def _rms(x, g):
    ms = jnp.mean(x * x, axis=-1, keepdims=True)
    return x * lax.rsqrt(ms + NORM_EPS) * g


def _dot(a, b):
    return jnp.dot(a, b, preferred_element_type=F32)


def _dot_nt(a, b):
    return lax.dot_general(a, b, (((1,), (1,)), ((), ())), preferred_element_type=F32)


def _split3(x):
    hi = x.astype(MXU_DTYPE)
    r1 = x - hi.astype(F32)
    mid = r1.astype(MXU_DTYPE)
    lo = (r1 - mid.astype(F32)).astype(MXU_DTYPE)
    return hi, mid, lo


def _rows(shape):
    return lax.broadcasted_iota(jnp.int32, shape, 0)


def _lanes(shape):
    return lax.broadcasted_iota(jnp.int32, shape, 1)


def _inproj_body(x_ref, g_ref, w_ref, rc_ref, ra_ref, rb_ref,
                 qn_ref, qr_ref, kvc_ref, ksa_ref, vs_ref, kw_ref, vw_ref,
                 fq_ref, fk_ref, fv_ref, small_ref, *, tm, n_s_tiles):
    h = _rms(x_ref[...], g_ref[...]).astype(MXU_DTYPE)
    rc, ra, rb = rc_ref[...], ra_ref[...], rb_ref[...]

    def mm(lo, n):
        return _dot(h, w_ref[:, lo:lo + n])

    def rope(v):
        return v * rc + pltpu.roll(v, LANE - ROPE_DIMS // 2, 1) * ra + pltpu.roll(v, ROPE_DIMS // 2, 1) * rb

    q = mm(_Q0, N_NSA_HEADS * LANE) * Q_SCALE
    qn_ref[...] = q.astype(MXU_DTYPE)
    for hh in range(N_NSA_HEADS):
        sl = slice(LANE * hh, LANE * (hh + 1))
        qr_ref[:, sl] = rope(q[:, sl]).astype(MXU_DTYPE)

    kvc = mm(_KVC0, 4 * LANE)
    for i in range(4):
        kvc_ref[i] = kvc[:, LANE * i:LANE * i + HEAD_DIM]

    pos = (pl.program_id(0) % n_s_tiles) * tm + _rows((tm, LANE))
    onehot = ((_lanes((tm, LANE)) - HEAD_DIM) == (pos >> SEL_SHIFT)).astype(F32)
    ks = mm(_KS0, N_NSA_KV * LANE)
    kw = mm(_KW0, N_NSA_KV * LANE)
    for g in range(N_NSA_KV):
        sl = slice(LANE * g, LANE * (g + 1))
        ksa_ref[:, sl] = (rope(ks[:, sl]) + onehot).astype(MXU_DTYPE)
        kw_ref[:, sl] = rope(kw[:, sl]).astype(MXU_DTYPE)
    vs_ref[...] = mm(_VS0, N_NSA_KV * LANE).astype(MXU_DTYPE)
    vw_ref[...] = mm(_VW0, N_NSA_KV * LANE).astype(MXU_DTYPE)

    nf = N_FOX_HEADS * HEAD_DIM
    fq_ref[...] = (mm(_FQ0, nf) * Q_SCALE).astype(MXU_DTYPE)
    fk_ref[...] = mm(_FK0, nf).astype(MXU_DTYPE)
    fv_ref[...] = mm(_FV0, nf).astype(MXU_DTYPE)
    small_ref[...] = mm(_SM0, N_NSA_KV * LANE)


def _inproj(xf, g, wp, rope_c, rope_a, rope_b, *, layer, seq, tm):
    m, d = xf.shape
    n_s_tiles = seq // tm
    nf = N_FOX_HEADS * HEAD_DIM
    row = lambda i: (i, 0)
    fixed = lambda i: (0, 0)
    tab = lambda i: (i % n_s_tiles, 0)
    out_shape = (
        jax.ShapeDtypeStruct((m, N_NSA_HEADS * LANE), MXU_DTYPE),
        jax.ShapeDtypeStruct((m, N_NSA_HEADS * LANE), MXU_DTYPE),
        jax.ShapeDtypeStruct((4, m, HEAD_DIM), F32),
        jax.ShapeDtypeStruct((m, N_NSA_KV * LANE), MXU_DTYPE),
        jax.ShapeDtypeStruct((m, N_NSA_KV * LANE), MXU_DTYPE),
        jax.ShapeDtypeStruct((m, N_NSA_KV * LANE), MXU_DTYPE),
        jax.ShapeDtypeStruct((m, N_NSA_KV * LANE), MXU_DTYPE),
        jax.ShapeDtypeStruct((m, nf), MXU_DTYPE),
        jax.ShapeDtypeStruct((m, nf), MXU_DTYPE),
        jax.ShapeDtypeStruct((m, nf), MXU_DTYPE),
        jax.ShapeDtypeStruct((m, N_NSA_KV * LANE), F32),
    )
    out_specs = (
        pl.BlockSpec((tm, N_NSA_HEADS * LANE), row),
        pl.BlockSpec((tm, N_NSA_HEADS * LANE), row),
        pl.BlockSpec((4, tm, HEAD_DIM), lambda i: (0, i, 0)),
        pl.BlockSpec((tm, N_NSA_KV * LANE), row),
        pl.BlockSpec((tm, N_NSA_KV * LANE), row),
        pl.BlockSpec((tm, N_NSA_KV * LANE), row),
        pl.BlockSpec((tm, N_NSA_KV * LANE), row),
        pl.BlockSpec((tm, nf), row),
        pl.BlockSpec((tm, nf), row),
        pl.BlockSpec((tm, nf), row),
        pl.BlockSpec((tm, N_NSA_KV * LANE), row),
    )
    return pl.pallas_call(
        functools.partial(_inproj_body, tm=tm, n_s_tiles=n_s_tiles),
        grid=(m // tm,),
        in_specs=[
            pl.BlockSpec((tm, d), row),
            pl.BlockSpec((1, d), fixed),
            pl.BlockSpec((None, d, _NP), lambda i: (layer, 0, 0)),
            pl.BlockSpec((tm, LANE), tab),
            pl.BlockSpec((tm, LANE), tab),
            pl.BlockSpec((tm, LANE), tab),
        ],
        out_specs=out_specs,
        out_shape=out_shape,
        compiler_params=pltpu.CompilerParams(
            dimension_semantics=("arbitrary",), vmem_limit_bytes=VMEM_LIMIT),
        name="inproj",
    )(xf, g, wp, rope_c, rope_a, rope_b)


def _cumsum_body(sm_ref, bias_ref, tril_ref, out_ref, *, seq, chunk):
    carry = jnp.zeros((1, LANE), F32)
    tril = tril_ref[...]
    for c in range(seq // chunk):
        sl = slice(c * chunk, (c + 1) * chunk)
        z = sm_ref[sl, :] + bias_ref[...]
        logf = jnp.minimum(z, 0.0) - jnp.log1p(jnp.exp(-jnp.abs(z)))
        hi, mid, lo = _split3(logf)
        cs = (_dot(tril, hi) + _dot(tril, mid)) + _dot(tril, lo) + carry
        out_ref[0, :, sl] = (cs * LOG2E).T[_FF_LANE:_FF_LANE + N_FOX_HEADS, :]
        carry = cs[chunk - 1:chunk, :]


def _forget_cumsum(small, fbias, tril, *, batch, seq):
    chunk = tril.shape[0]
    return pl.pallas_call(
        functools.partial(_cumsum_body, seq=seq, chunk=chunk),
        grid=(batch,),
        in_specs=[
            pl.BlockSpec((seq, LANE), lambda b: (b, 0)),
            pl.BlockSpec((1, LANE), lambda b: (0, 0)),
            pl.BlockSpec((chunk, chunk), lambda b: (0, 0)),
        ],
        out_specs=pl.BlockSpec((1, N_FOX_HEADS, seq), lambda b: (b, 0, 0)),
        out_shape=jax.ShapeDtypeStruct((batch, N_FOX_HEADS, seq), F32),
        compiler_params=pltpu.CompilerParams(
            dimension_semantics=("arbitrary",), vmem_limit_bytes=VMEM_LIMIT),
        name="forget_cumsum",
    )(small, fbias, tril)


def _compress_body(x_ref, pos_ref, w1_ref, b1_ref, w2_ref, b2_ref, o_ref, *, n_chunk):
    half = CMP_STRIDE * HEAD_DIM
    x = jnp.concatenate(
        [x_ref[0, pl.ds(t, n_chunk, stride=CMP_STRIDE), :] for t in range(CMP_STRIDE)], axis=1)
    first = _dot((x + pos_ref[0, 0:1, :]).astype(MXU_DTYPE), w1_ref[0, 0:half, :])
    second = _dot((x + pos_ref[0, 1:2, :]).astype(MXU_DTYPE), w1_ref[0, half:2 * half, :])
    hid = first + pltpu.roll(second, n_chunk - 1, 0) + b1_ref[0]
    act = jax.nn.gelu(hid)
    o_ref[0, 0] = (_dot(act.astype(MXU_DTYPE), w2_ref[0]) + b2_ref[0]).astype(MXU_DTYPE)


def _compress(kvc, pos, w1, b1, w2p, b2p, *, batch, n_chunk):
    half = CMP_STRIDE * HEAD_DIM
    kv = lambda i, b: (i // N_NSA_KV, 0, 0)
    return pl.pallas_call(
        functools.partial(_compress_body, n_chunk=n_chunk),
        grid=(2 * N_NSA_KV, batch),
        in_specs=[
            pl.BlockSpec((1, n_chunk * CMP_STRIDE, HEAD_DIM), lambda i, b: (i, b, 0)),
            pl.BlockSpec((1, 2, half), kv),
            pl.BlockSpec((1, 2 * half, CMP_HIDDEN), kv),
            pl.BlockSpec((1, 1, CMP_HIDDEN), kv),
            pl.BlockSpec((1, CMP_HIDDEN, LANE), kv),
            pl.BlockSpec((1, 1, LANE), kv),
        ],
        out_specs=pl.BlockSpec((1, 1, n_chunk, LANE), lambda i, b: (i, b, 0, 0)),
        out_shape=jax.ShapeDtypeStruct((2 * N_NSA_KV, batch, n_chunk, LANE), MXU_DTYPE),
        compiler_params=pltpu.CompilerParams(
            dimension_semantics=("arbitrary", "arbitrary"), vmem_limit_bytes=VMEM_LIMIT),
        name="compress",
    )(kvc, pos, w1, b1, w2p, b2p)


def _lane_max(s):
    out = s[:, 0:LANE]
    for c in range(1, s.shape[1] // LANE):
        out = jnp.maximum(out, s[:, LANE * c:LANE * (c + 1)])
    return out


def _row_max_lanes(mrun):
    return jnp.broadcast_to(jnp.max(mrun, axis=-1, keepdims=True), mrun.shape)


def _exp_tiles(s, mb):
    n = s.shape[1] // LANE
    return jnp.concatenate(
        [jnp.exp2(s[:, LANE * c:LANE * (c + 1)] - mb).astype(MXU_DTYPE) for c in range(n)], axis=1)


def _values_with_ones(v, ones_low):
    low = _lanes(v.shape) < HEAD_DIM
    ones = jnp.ones_like(v)
    return jnp.where(low, ones, v) if ones_low else jnp.where(low, v, ones)


def _normalize(acc):
    return acc / pltpu.roll(acc, HEAD_DIM, 1)


def _spans(n_tiles, groups, fn):
    start = 0
    for size in tuple(groups) + (1,):
        n_groups = lax.div(n_tiles - start, size)

        def step(t, carry, size=size, start=start):
            fn(start + t * size, size)
            return carry

        lax.fori_loop(0, n_groups, step, 0)
        start = start + n_groups * size


def _nsa_body(qn_ref, qr_ref, kc_ref, vc_ref, ksa_ref, vs_ref, kw_ref, vw_ref, sm_ref, gb_ref, impt_ref,
              o_ref, s_ref, mrun_ref, acc_ref, *, tq, tk, group, n_sel, top_k):
    q0 = pl.program_id(2) * tq
    r = NSA_GROUP * tq
    n_cmp = kc_ref.shape[2]

    def window_scores(hq):
        n_wc = WINDOW // LANE + 1
        upper = _lanes((NSA_GROUP * LANE, LANE)) > (_rows((NSA_GROUP * LANE, LANE)) & (LANE - 1))
        qs = q0 + hq * LANE
        qh = jnp.concatenate(
            [qr_ref[hq * LANE:(hq + 1) * LANE, LANE * h:LANE * (h + 1)] for h in range(NSA_GROUP)], axis=0)
        starts = [pl.multiple_of(jnp.maximum(qs - WINDOW + LANE * c, 0), LANE) for c in range(n_wc)]
        kk = jnp.concatenate([kw_ref[pl.ds(st, LANE), :] for st in starts], axis=0)
        vv = jnp.concatenate([vw_ref[pl.ds(st, LANE), :] for st in starts], axis=0)
        sw = _dot_nt(qh, kk)
        chunks = [sw[:, LANE * c:LANE * (c + 1)] for c in range(n_wc)]
        chunks[0] = jnp.where(upper, chunks[0], NEG_INF)
        chunks[-1] = jnp.where(upper, NEG_INF, chunks[-1])
        for c in range(n_wc - 1):
            chunks[c] = jnp.where(qs - WINDOW + LANE * c >= 0, chunks[c], NEG_INF)
        return chunks, vv

    def window_probs(chunks):
        mb = _row_max_lanes(functools.reduce(jnp.maximum, chunks))
        return jnp.concatenate([jnp.exp2(ch - mb).astype(MXU_DTYPE) for ch in chunks], axis=1)

    def window_out(pw, vv):
        return _normalize(_dot(pw, _values_with_ones(vv, False)))

    q4n = jnp.concatenate([qn_ref[:, LANE * h:LANE * (h + 1)] for h in range(NSA_GROUP)], axis=0)
    sc = _dot_nt(q4n, kc_ref[0, 0])
    win_scores = [window_scores(0), window_scores(1)]
    last_vis = (q0 + _rows((tq, LANE)) - (CMP_LEN - 1)) >> CMP_SHIFT
    last_vis = jnp.concatenate([last_vis] * NSA_GROUP, axis=0)
    sc_chunks = [
        jnp.where(_lanes((r, LANE)) + LANE * c <= last_vis, sc[:, LANE * c:LANE * (c + 1)], NEG_INF)
        for c in range(n_cmp // LANE)]
    mb = jnp.maximum(_row_max_lanes(functools.reduce(jnp.maximum, sc_chunks)), 0.1 * NEG_INF)
    e_chunks = [jnp.exp2(ch - mb) for ch in sc_chunks]
    den = jnp.sum(functools.reduce(jnp.add, e_chunks), axis=-1, keepdims=True)
    inv = 1.0 / jnp.maximum(jnp.broadcast_to(den, (r, LANE)), 1e-30)
    pc = jnp.concatenate([ech * inv for ech in e_chunks], axis=1)
    o_cmp = _dot(pc.astype(MXU_DTYPE), vc_ref[0, 0])

    pcs = (pc[0:tq] + pc[tq:2 * tq]) + (pc[2 * tq:3 * tq] + pc[3 * tq:4 * tq])
    impt_w = impt_ref[...]
    hi, mid, lo = _split3(pcs)
    imp_t = (_dot_nt(impt_w, hi) + _dot_nt(impt_w, mid)) + _dot_nt(impt_w, lo)
    jrow = _rows((n_sel, tq))
    qpos = q0 + _lanes((n_sel, tq))
    cur = qpos >> SEL_SHIFT
    valid = jrow * SEL_BLOCK <= qpos
    forced = (jrow == 0) | (jrow == cur) | (jrow == cur - 1)
    score = jnp.where(forced, FORCED_SCORE, jnp.where(valid, imp_t, -1.0))

    sub = 8
    chunks = [score[sub * c:sub * (c + 1), :] for c in range(n_sel // sub)]
    ranks = [jnp.zeros((sub, tq), F32) for _ in chunks]
    later = _rows((sub, tq))

    win_probs = [window_probs(win_scores[0][0])]
    win_blocks = []
    for jp in range(n_sel):
        if jp == n_sel // 2:
            win_blocks.append(window_out(win_probs[0], win_scores[0][1]))
            win_probs.append(window_probs(win_scores[1][0]))
        cj, rj = divmod(jp, sub)
        rowv = chunks[cj][rj:rj + 1, :]
        for c, sc_c in enumerate(chunks):
            if c > cj:
                beats = rowv >= sc_c
            elif c < cj:
                beats = rowv > sc_c
            else:
                beats = (rowv > sc_c) | ((rowv == sc_c) & (later > rj))
            ranks[c] = ranks[c] + jnp.where(beats, 1.0, 0.0)
    sel_t = jnp.where(jnp.concatenate(ranks, axis=0) < float(top_k), 1.0, 0.0)
    sel_pad = jnp.concatenate([sel_t, jnp.zeros((LANE - n_sel, tq), F32)], axis=0)
    sel_q = pltpu.roll(sel_pad.T, HEAD_DIM, 1)
    sel_bias = ((sel_q - 1.0) * (-NEG_INF)).astype(MXU_DTYPE)

    high = _lanes((tq, LANE)) >= HEAD_DIM
    q4r = jnp.concatenate(
        [jnp.where(high, sel_bias, qr_ref[:, LANE * h:LANE * (h + 1)]) for h in range(NSA_GROUP)], axis=0)

    win_blocks.append(window_out(win_probs[1], win_scores[1][1]))
    o_win = jnp.concatenate([blk[h * LANE:(h + 1) * LANE] for h in range(NSA_GROUP) for blk in win_blocks], axis=0)

    gate = jax.nn.sigmoid(sm_ref[...] + gb_ref[...])
    partial, gate_sel = [], []
    for h in range(NSA_GROUP):
        rs = slice(h * tq, (h + 1) * tq)
        c = N_BRANCH * h
        partial.append(gate[:, c:c + 1] * o_cmp[rs] + gate[:, c + 2:c + 3] * o_win[rs])
        gate_sel.append(jnp.broadcast_to(gate[:, c + 1:c + 2], (tq, LANE)))

    rowq_k = q0 + (_rows((r, tk)) & (tq - 1))
    n_full = lax.div(q0, tk)

    def score_span(kb, n_tiles, masked):
        k0 = pl.multiple_of(kb * tk, tk)
        s = _dot_nt(q4r, ksa_ref[pl.ds(k0, n_tiles * tk), :])
        if masked:
            s = jnp.where(k0 + _lanes((r, tk)) <= rowq_k, s, NEG_INF)
        for j in range(n_tiles):
            s_ref[kb + j] = s[:, j * tk:(j + 1) * tk]
        mrun_ref[...] = jnp.maximum(mrun_ref[...], _lane_max(s))

    def value_span(kb, n_tiles):
        k0 = pl.multiple_of(kb * tk, tk)
        mb = mrun_ref[...]
        p = jnp.concatenate([_exp_tiles(s_ref[kb + j], mb) for j in range(n_tiles)], axis=1)
        acc_ref[...] += _dot(p, _values_with_ones(vs_ref[pl.ds(k0, n_tiles * tk), :], False))

    mrun_ref[...] = jnp.full((r, LANE), NEG_INF, F32)
    _spans(n_full, group, lambda kb, n: score_span(kb, n, False))
    score_span(n_full, 1, True)
    mrun_ref[...] = _row_max_lanes(mrun_ref[...])
    acc_ref[...] = jnp.zeros((r, LANE), F32)
    _spans(n_full + 1, group, value_span)
    o_sel = _normalize(acc_ref[...])

    heads = [partial[h] + gate_sel[h] * o_sel[h * tq:(h + 1) * tq] for h in range(NSA_GROUP)]
    low = _lanes((tq, LANE)) < HEAD_DIM
    for p in range(NSA_GROUP // 2):
        pair = jnp.where(low, heads[2 * p], pltpu.roll(heads[2 * p + 1], HEAD_DIM, 1))
        o_ref[:, LANE * p:LANE * (p + 1)] = pair.astype(MXU_DTYPE)


def _nsa(qn, qr, cmp_kv, ksa, vs, kw, vw, small, gbias, imp_t, *, batch, seq, tq, tk, group):
    assert tq == 2 * LANE
    n_q = seq // tq
    n_cmp = cmp_kv.shape[2]
    n_sel = seq // SEL_BLOCK
    top_k = min(SEL_TOPK, n_sel)
    gw = NSA_GROUP * LANE
    qmap = lambda b, g, i: (b * n_q + i, g)
    kvmap = lambda b, g, i: (b, g)
    r = NSA_GROUP * tq
    return pl.pallas_call(
        functools.partial(_nsa_body, tq=tq, tk=tk, group=group, n_sel=n_sel, top_k=top_k),
        grid=(batch, N_NSA_KV, n_q),
        in_specs=[
            pl.BlockSpec((tq, gw), qmap),
            pl.BlockSpec((tq, gw), qmap),
            pl.BlockSpec((1, 1, n_cmp, LANE), lambda b, g, i: (g, b, 0, 0)),
            pl.BlockSpec((1, 1, n_cmp, LANE), lambda b, g, i: (N_NSA_KV + g, b, 0, 0)),
            pl.BlockSpec((seq, LANE), kvmap),
            pl.BlockSpec((seq, LANE), kvmap),
            pl.BlockSpec((seq, LANE), kvmap),
            pl.BlockSpec((seq, LANE), kvmap),
            pl.BlockSpec((tq, LANE), qmap),
            pl.BlockSpec((1, LANE), lambda b, g, i: (0, g)),
            pl.BlockSpec((n_sel, n_cmp), lambda b, g, i: (0, 0)),
        ],
        out_specs=pl.BlockSpec((tq, NSA_GROUP * HEAD_DIM), qmap),
        out_shape=jax.ShapeDtypeStruct((batch * seq, N_NSA_HEADS * HEAD_DIM), MXU_DTYPE),
        scratch_shapes=[
            pltpu.VMEM((seq // tk, r, tk), F32),
            pltpu.VMEM((r, LANE), F32),
            pltpu.VMEM((r, LANE), F32),
        ],
        compiler_params=pltpu.CompilerParams(
            dimension_semantics=("arbitrary", "arbitrary", "arbitrary"), vmem_limit_bytes=VMEM_LIMIT),
        name="nsa_attention",
    )(qn, qr, cmp_kv, cmp_kv, ksa, vs, kw, vw, small, gbias, imp_t)


def _fox_body(q_ref, k_ref, v_ref, cum_ref, o_ref, s_ref, mrun_ref, acc_ref, *, tq, tk, group):
    i = pl.program_id(2)
    q0 = pl.multiple_of(i * tq, tq)
    qp = q_ref[...]
    low = _lanes((tq, LANE)) < HEAD_DIM
    causal_rows = q0 + _rows((tq, tk))
    n_full = lax.div(q0, tk)
    qs, cqs, cum_rows = [], [], []
    for e in range(2):
        qs.append(jnp.where(low if e == 0 else jnp.logical_not(low), qp, jnp.zeros_like(qp)))
        cum_rows.append(2 * pl.program_id(1) + e)
        cq = jnp.broadcast_to(cum_ref[0, pl.ds(cum_rows[e], 1), pl.ds(q0, tq)], (LANE, tq)).T
        cqs.append(jnp.concatenate([cq] * (tk // LANE), axis=1))

    def score_span(kb, n_tiles, masked):
        k0 = pl.multiple_of(kb * tk, tk)
        k = k_ref[pl.ds(k0, n_tiles * tk), :]
        for e in range(2):
            s = _dot_nt(qs[e], k)
            for j in range(n_tiles):
                ck = cum_ref[0, pl.ds(cum_rows[e], 1), pl.ds(k0 + j * tk, tk)]
                sj = s[:, j * tk:(j + 1) * tk] + cqs[e] - ck
                if masked:
                    sj = jnp.where(k0 + _lanes((tq, tk)) <= causal_rows, sj, NEG_INF)
                s_ref[e, kb + j] = sj
                mrun_ref[e] = jnp.maximum(mrun_ref[e], _lane_max(sj))

    def value_span(kb, n_tiles):
        k0 = pl.multiple_of(kb * tk, tk)
        v = v_ref[pl.ds(k0, n_tiles * tk), :]
        for e in range(2):
            mb = mrun_ref[e]
            p = jnp.concatenate([_exp_tiles(s_ref[e, kb + j], mb) for j in range(n_tiles)], axis=1)
            acc_ref[e] += _dot(p, _values_with_ones(v, e == 1))

    mrun_ref[...] = jnp.full((2, tq, LANE), NEG_INF, F32)
    _spans(n_full, group, lambda kb, n: score_span(kb, n, False))
    score_span(n_full, 1, True)
    for e in range(2):
        mrun_ref[e] = _row_max_lanes(mrun_ref[e])
    acc_ref[...] = jnp.zeros((2, tq, LANE), F32)
    _spans(n_full + 1, group, value_span)
    o_ref[...] = jnp.where(low, _normalize(acc_ref[0]), _normalize(acc_ref[1])).astype(MXU_DTYPE)


def _fox(fq, fk, fv, cum, *, batch, seq, tq, tk, group):
    n_q = seq // tq
    n_pairs = N_FOX_HEADS // 2
    qmap = lambda b, p, i: (b * n_q + i, p)
    kvmap = lambda b, p, i: (b, p)
    return pl.pallas_call(
        functools.partial(_fox_body, tq=tq, tk=tk, group=group),
        grid=(batch, n_pairs, n_q),
        in_specs=[
            pl.BlockSpec((tq, LANE), qmap),
            pl.BlockSpec((seq, LANE), kvmap),
            pl.BlockSpec((seq, LANE), kvmap),
            pl.BlockSpec((1, N_FOX_HEADS, seq), lambda b, p, i: (b, 0, 0)),
        ],
        out_specs=pl.BlockSpec((tq, LANE), qmap),
        out_shape=jax.ShapeDtypeStruct((batch * seq, N_FOX_HEADS * HEAD_DIM), MXU_DTYPE),
        scratch_shapes=[
            pltpu.VMEM((2, seq // tk, tq, tk), F32),
            pltpu.VMEM((2, tq, LANE), F32),
            pltpu.VMEM((2, tq, LANE), F32),
        ],
        compiler_params=pltpu.CompilerParams(
            dimension_semantics=("arbitrary", "arbitrary", "arbitrary"), vmem_limit_bytes=VMEM_LIMIT),
        name="fox_attention",
    )(fq, fk, fv, cum)


def _post_body(on_ref, of_ref, x_ref, wo_ref, gpm_ref, gpl_ref, wu_ref, wd_ref, gpo_ref, o_ref, *, ff_chunk):
    n_nsa = on_ref.shape[1]
    mix = _dot(on_ref[...], wo_ref[0:n_nsa, :]) + _dot(of_ref[...], wo_ref[n_nsa:, :])
    x1 = x_ref[...] + _rms(mix, gpm_ref[...])
    h = _rms(x1, gpl_ref[...]).astype(MXU_DTYPE)
    d_ff = wu_ref.shape[1]
    acc = jnp.zeros(x1.shape, F32)
    for c in range(d_ff // ff_chunk):
        sl = slice(c * ff_chunk, (c + 1) * ff_chunk)
        a = jnp.square(jnp.maximum(_dot(h, wu_ref[:, sl]), 0.0))
        acc = acc + _dot(a.astype(MXU_DTYPE), wd_ref[sl, :])
    o_ref[...] = x1 + _rms(acc, gpo_ref[...])


def _post(o_nsa, o_fox, xf, wo, g_post_mix, g_pre_mlp, wu, wd, g_post_mlp, *, layer, tm, ff_chunk):
    m, d = xf.shape
    d_ff = wu.shape[2]
    row = lambda i: (i, 0)
    fixed = lambda i: (0, 0)
    resident = lambda rows, cols: pl.BlockSpec(
        (None, rows, cols), lambda i: (layer, 0, 0), pipeline_mode=pl.Buffered(1))
    return pl.pallas_call(
        functools.partial(_post_body, ff_chunk=ff_chunk),
        grid=(m // tm,),
        in_specs=[
            pl.BlockSpec((tm, o_nsa.shape[1]), row),
            pl.BlockSpec((tm, o_fox.shape[1]), row),
            pl.BlockSpec((tm, d), row),
            resident(wo.shape[1], d),
            pl.BlockSpec((1, d), fixed),
            pl.BlockSpec((1, d), fixed),
            resident(d, d_ff),
            resident(d_ff, d),
            pl.BlockSpec((1, d), fixed),
        ],
        out_specs=pl.BlockSpec((tm, d), row),
        out_shape=jax.ShapeDtypeStruct((m, d), F32),
        compiler_params=pltpu.CompilerParams(
            dimension_semantics=("arbitrary",), vmem_limit_bytes=VMEM_LIMIT),
        name="outproj_mlp",
    )(o_nsa, o_fox, xf, wo, g_post_mix, g_pre_mlp, wu, wd, g_post_mlp)


def _rope_tables(seq):
    half = ROPE_DIMS // 2
    inv_freq = ROPE_THETA ** (-jnp.arange(half, dtype=F32) * 2.0 / ROPE_DIMS)
    ang = jnp.arange(seq).astype(F32)[:, None] * inv_freq[None, :]
    cos, sin = jnp.cos(ang), jnp.sin(ang)
    pad = jnp.zeros((seq, LANE - ROPE_DIMS), F32)
    zero = jnp.zeros((seq, half), F32)
    c = jnp.concatenate([cos, cos, pad + 1.0], axis=-1)
    a = jnp.concatenate([-sin, zero, pad], axis=-1)
    b = jnp.concatenate([zero, sin, pad], axis=-1)
    return c, a, b


def _importance_map_t(seq, n_chunk):
    n_cmp = (seq - CMP_LEN) // CMP_STRIDE + 1
    n_sel = seq // SEL_BLOCK
    cs = np.arange(n_cmp)[None, :] * CMP_STRIDE
    ss = np.arange(n_sel)[:, None] * SEL_BLOCK
    overlap = np.clip(np.minimum(cs + CMP_LEN, ss + SEL_BLOCK) - np.maximum(cs, ss), 0, None) / CMP_LEN
    out = np.zeros((n_sel, n_chunk), np.float32)
    out[:, :n_cmp] = overlap
    return jnp.asarray(out, dtype=MXU_DTYPE)


def kernel(x, w_in, b_nsa_gate, b_forget, cmp_pos_k, cmp_w1_k, cmp_b1_k, cmp_w2_k, cmp_b2_k, cmp_pos_v, cmp_w1_v, cmp_b1_v, cmp_w2_v, cmp_b2_v, w_out, w_up, w_down, g_pre_mix, g_post_mix, g_pre_mlp, g_post_mlp):
    batch, seq, d_model = x.shape
    depth = w_in.shape[0]
    m = batch * seq
    n_chunk = seq // CMP_STRIDE
    n_sel = seq // SEL_BLOCK
    tm = min(512, seq)
    nsa_tq, nsa_tk = 2 * LANE, min(256, seq)
    fox_t = min(512, seq)
    assert seq % tm == 0 and seq % fox_t == 0 and seq % nsa_tk == 0
    assert n_sel <= HEAD_DIM and n_sel % 8 == 0 and seq >= WINDOW + nsa_tq
    assert x.dtype == F32

    rope_c, rope_a, rope_b = _rope_tables(seq)
    imp_t = _importance_map_t(seq, n_chunk)
    cs_chunk = min(256, seq)
    tril = jnp.asarray(np.tril(np.ones((cs_chunk, cs_chunk), np.float32)), dtype=MXU_DTYPE)

    wp = _permute_columns(w_in.astype(MXU_DTYPE), _projection_columns())

    gate_bias = jnp.zeros((depth, 1, N_NSA_KV * LANE), F32)
    fbias = jnp.zeros((depth, 1, LANE), F32)
    n_gate = NSA_GROUP * N_BRANCH
    for g in range(N_NSA_KV):
        gate_bias = gate_bias.at[:, 0, LANE * g:LANE * g + n_gate].set(b_nsa_gate[:, n_gate * g:n_gate * (g + 1)])
    fbias = fbias.at[:, 0, _FF_LANE:_FF_LANE + N_FOX_HEADS].set(b_forget)

    half = CMP_STRIDE * HEAD_DIM
    cpos = jnp.stack([cmp_pos_k.reshape(depth, 2, half), cmp_pos_v.reshape(depth, 2, half)], axis=1)
    cw1 = jnp.stack([cmp_w1_k, cmp_w1_v], axis=1).astype(MXU_DTYPE)
    cb1 = jnp.stack([cmp_b1_k, cmp_b1_v], axis=1)[:, :, None, :]
    zw = jnp.zeros_like(cmp_w2_k)
    cw2 = jnp.stack([jnp.concatenate([cmp_w2_k, zw], axis=-1),
                     jnp.concatenate([cmp_w2_v, cmp_w2_v], axis=-1)], axis=1).astype(MXU_DTYPE)
    zb = jnp.zeros_like(cmp_b2_k)
    cb2 = jnp.stack([jnp.concatenate([cmp_b2_k, zb], axis=-1),
                     jnp.concatenate([cmp_b2_v, cmp_b2_v], axis=-1)], axis=1)[:, :, None, :]

    wo = w_out.astype(MXU_DTYPE)
    wu = w_up.astype(MXU_DTYPE)
    wd = w_down.astype(MXU_DTYPE)

    xf = x.reshape(m, d_model)
    for l in range(depth):
        qn, qr, kvc, ksa, vs, kw, vw, fq, fk, fv, small = _inproj(
            xf, g_pre_mix[l][None, :], wp, rope_c, rope_a, rope_b, layer=l, seq=seq, tm=tm)

        cum = _forget_cumsum(small, fbias[l], tril, batch=batch, seq=seq)

        cmp_kv = _compress(kvc, cpos[l], cw1[l], cb1[l], cw2[l], cb2[l],
                           batch=batch, n_chunk=n_chunk)
        o_nsa = _nsa(qn, qr, cmp_kv, ksa, vs, kw, vw, small, gate_bias[l], imp_t,
                     batch=batch, seq=seq, tq=nsa_tq, tk=nsa_tk, group=(4, 2))
        o_fox = _fox(fq, fk, fv, cum, batch=batch, seq=seq, tq=fox_t, tk=fox_t, group=(4, 2))
        xf = _post(o_nsa, o_fox, xf, wo, g_post_mix[l][None, :], g_pre_mlp[l][None, :],
                   wu, wd, g_post_mlp[l][None, :], layer=l, tm=tm, ff_chunk=1024)
    return xf.reshape(batch, seq, d_model)
```

```python
import functools

import numpy as np
import jax
import jax.numpy as jnp
from jax import lax
from jax.experimental import pallas as pl
from jax.experimental.pallas import tpu as pltpu

HEAD_DIM = 64
N_NSA_HEADS = 8
N_NSA_KV = 2
NSA_GROUP = N_NSA_HEADS // N_NSA_KV
N_FOX_HEADS = 8
N_BRANCH = 3
CMP_LEN = 32
CMP_STRIDE = 16
CMP_SHIFT = 4
CMP_HIDDEN = 256
SEL_BLOCK = 64
SEL_SHIFT = 6
SEL_TOPK = 16
WINDOW = 512
ROPE_THETA = 500000.0
ROPE_DIMS = HEAD_DIM // 4
NORM_EPS = 1e-6
NEG_INF = -1e30
FORCED_SCORE = 1e4
SCALE = HEAD_DIM ** -0.5
LOG2E = 1.4426950408889634
Q_SCALE = SCALE * LOG2E

LANE = 128
MXU_DTYPE = jnp.bfloat16
F32 = jnp.float32
V7X_VMEM_BYTES = 64 * 1024 * 1024
VMEM_LIMIT = V7X_VMEM_BYTES * 7 // 8

_SRC_NQ, _SRC_KC, _SRC_VC, _SRC_KSEL, _SRC_VSEL, _SRC_KWIN, _SRC_VWIN = 0, 512, 640, 768, 896, 1024, 1152
_SRC_GATE, _SRC_FQ, _SRC_FK, _SRC_FV, _SRC_FF = 1280, 1304, 1816, 2328, 2840
_Q0, _KVC0, _KS0, _VS0, _KW0, _VW0, _FQ0, _FK0, _FV0, _SM0, _NP = (
    0, 1024, 1536, 1792, 2048, 2304, 2560, 3072, 3584, 4096, 4352)
_FF_LANE = 16


def _projection_columns():
    src = np.full((_NP,), -1, np.int32)
    d = np.arange(HEAD_DIM)
    for h in range(N_NSA_HEADS):
        src[_Q0 + LANE * h + d] = _SRC_NQ + HEAD_DIM * h + d
    for i in range(4):
        src[_KVC0 + LANE * i + d] = _SRC_KC + HEAD_DIM * i + d
    for g in range(N_NSA_KV):
        src[_KS0 + LANE * g + d] = _SRC_KSEL + HEAD_DIM * g + d
        src[_KW0 + LANE * g + d] = _SRC_KWIN + HEAD_DIM * g + d
        for half in range(2):
            src[_VS0 + LANE * g + HEAD_DIM * half + d] = _SRC_VSEL + HEAD_DIM * g + d
            src[_VW0 + LANE * g + HEAD_DIM * half + d] = _SRC_VWIN + HEAD_DIM * g + d
        gates = np.arange(NSA_GROUP * N_BRANCH)
        src[_SM0 + LANE * g + gates] = _SRC_GATE + NSA_GROUP * N_BRANCH * g + gates
        src[_SM0 + LANE * g + _FF_LANE + np.arange(N_FOX_HEADS)] = _SRC_FF + np.arange(N_FOX_HEADS)
    fox = np.arange(N_FOX_HEADS * HEAD_DIM)
    src[_FQ0 + fox] = _SRC_FQ + fox
    src[_FK0 + fox] = _SRC_FK + fox
    src[_FV0 + fox] = _SRC_FV + fox
    return src


def _permute_columns(w, src):
    pieces, i = [], 0
    while i < len(src):
        j = i + 1
        if src[i] < 0:
            while j < len(src) and src[j] < 0:
                j += 1
            pieces.append(jnp.zeros(w.shape[:-1] + (j - i,), w.dtype))
        else:
            while j < len(src) and src[j] == src[j - 1] + 1:
                j += 1
            pieces.append(w[..., int(src[i]):int(src[i]) + (j - i)])
        i = j
    return jnp.concatenate(pieces, axis=-1)


def _rms(x, g):
    ms = jnp.mean(x * x, axis=-1, keepdims=True)
    return x * lax.rsqrt(ms + NORM_EPS) * g


def _dot(a, b):
    return jnp.dot(a, b, preferred_element_type=F32)


def _dot_nt(a, b):
    return lax.dot_general(a, b, (((1,), (1,)), ((), ())), preferred_element_type=F32)


def _split3(x):
    hi = x.astype(MXU_DTYPE)
    r1 = x - hi.astype(F32)
    mid = r1.astype(MXU_DTYPE)
    lo = (r1 - mid.astype(F32)).astype(MXU_DTYPE)
    return hi, mid, lo


def _rows(shape):
    return lax.broadcasted_iota(jnp.int32, shape, 0)


def _lanes(shape):
    return lax.broadcasted_iota(jnp.int32, shape, 1)


def _inproj_body(x_ref, g_ref, w_ref, rc_ref, ra_ref, rb_ref,
                 qn_ref, qr_ref, kvc_ref, ksa_ref, vs_ref, kw_ref, vw_ref,
                 fq_ref, fk_ref, fv_ref, small_ref, *, tm, n_s_tiles):
    h = _rms(x_ref[...], g_ref[...]).astype(MXU_DTYPE)
    rc, ra, rb = rc_ref[...], ra_ref[...], rb_ref[...]

    def mm(lo, n):
        return _dot(h, w_ref[:, lo:lo + n])

    def rope(v):
        return v * rc + pltpu.roll(v, LANE - ROPE_DIMS // 2, 1) * ra + pltpu.roll(v, ROPE_DIMS // 2, 1) * rb

    q = mm(_Q0, N_NSA_HEADS * LANE) * Q_SCALE
    qn_ref[...] = q.astype(MXU_DTYPE)
    for hh in range(N_NSA_HEADS):
        sl = slice(LANE * hh, LANE * (hh + 1))
        qr_ref[:, sl] = rope(q[:, sl]).astype(MXU_DTYPE)

    kvc = mm(_KVC0, 4 * LANE)
    for i in range(4):
        kvc_ref[i] = kvc[:, LANE * i:LANE * i + HEAD_DIM]

    pos = (pl.program_id(0) % n_s_tiles) * tm + _rows((tm, LANE))
    onehot = ((_lanes((tm, LANE)) - HEAD_DIM) == (pos >> SEL_SHIFT)).astype(F32)
    ks = mm(_KS0, N_NSA_KV * LANE)
    kw = mm(_KW0, N_NSA_KV * LANE)
    for g in range(N_NSA_KV):
        sl = slice(LANE * g, LANE * (g + 1))
        ksa_ref[:, sl] = (rope(ks[:, sl]) + onehot).astype(MXU_DTYPE)
        kw_ref[:, sl] = rope(kw[:, sl]).astype(MXU_DTYPE)
    vs_ref[...] = mm(_VS0, N_NSA_KV * LANE).astype(MXU_DTYPE)
    vw_ref[...] = mm(_VW0, N_NSA_KV * LANE).astype(MXU_DTYPE)

    nf = N_FOX_HEADS * HEAD_DIM
    fq_ref[...] = (mm(_FQ0, nf) * Q_SCALE).astype(MXU_DTYPE)
    fk_ref[...] = mm(_FK0, nf).astype(MXU_DTYPE)
    fv_ref[...] = mm(_FV0, nf).astype(MXU_DTYPE)
    small_ref[...] = mm(_SM0, N_NSA_KV * LANE)


def _inproj(xf, g, wp, rope_c, rope_a, rope_b, *, layer, seq, tm):
    m, d = xf.shape
    n_s_tiles = seq // tm
    nf = N_FOX_HEADS * HEAD_DIM
    row = lambda i: (i, 0)
    fixed = lambda i: (0, 0)
    tab = lambda i: (i % n_s_tiles, 0)
    out_shape = (
        jax.ShapeDtypeStruct((m, N_NSA_HEADS * LANE), MXU_DTYPE),
        jax.ShapeDtypeStruct((m, N_NSA_HEADS * LANE), MXU_DTYPE),
        jax.ShapeDtypeStruct((4, m, HEAD_DIM), F32),
        jax.ShapeDtypeStruct((m, N_NSA_KV * LANE), MXU_DTYPE),
        jax.ShapeDtypeStruct((m, N_NSA_KV * LANE), MXU_DTYPE),
        jax.ShapeDtypeStruct((m, N_NSA_KV * LANE), MXU_DTYPE),
        jax.ShapeDtypeStruct((m, N_NSA_KV * LANE), MXU_DTYPE),
        jax.ShapeDtypeStruct((m, nf), MXU_DTYPE),
        jax.ShapeDtypeStruct((m, nf), MXU_DTYPE),
        jax.ShapeDtypeStruct((m, nf), MXU_DTYPE),
        jax.ShapeDtypeStruct((m, N_NSA_KV * LANE), F32),
    )
    out_specs = (
        pl.BlockSpec((tm, N_NSA_HEADS * LANE), row),
        pl.BlockSpec((tm, N_NSA_HEADS * LANE), row),
        pl.BlockSpec((4, tm, HEAD_DIM), lambda i: (0, i, 0)),
        pl.BlockSpec((tm, N_NSA_KV * LANE), row),
        pl.BlockSpec((tm, N_NSA_KV * LANE), row),
        pl.BlockSpec((tm, N_NSA_KV * LANE), row),
        pl.BlockSpec((tm, N_NSA_KV * LANE), row),
        pl.BlockSpec((tm, nf), row),
        pl.BlockSpec((tm, nf), row),
        pl.BlockSpec((tm, nf), row),
        pl.BlockSpec((tm, N_NSA_KV * LANE), row),
    )
    return pl.pallas_call(
        functools.partial(_inproj_body, tm=tm, n_s_tiles=n_s_tiles),
        grid=(m // tm,),
        in_specs=[
            pl.BlockSpec((tm, d), row),
            pl.BlockSpec((1, d), fixed),
            pl.BlockSpec((None, d, _NP), lambda i: (layer, 0, 0)),
            pl.BlockSpec((tm, LANE), tab),
            pl.BlockSpec((tm, LANE), tab),
            pl.BlockSpec((tm, LANE), tab),
        ],
        out_specs=out_specs,
        out_shape=out_shape,
        compiler_params=pltpu.CompilerParams(
            dimension_semantics=("arbitrary",), vmem_limit_bytes=VMEM_LIMIT),
        name="inproj",
    )(xf, g, wp, rope_c, rope_a, rope_b)


def _cumsum_body(sm_ref, bias_ref, tril_ref, out_ref, *, seq, chunk):
    carry = jnp.zeros((1, LANE), F32)
    tril = tril_ref[...]
    for c in range(seq // chunk):
        sl = slice(c * chunk, (c + 1) * chunk)
        z = sm_ref[sl, :] + bias_ref[...]
        logf = jnp.minimum(z, 0.0) - jnp.log1p(jnp.exp(-jnp.abs(z)))
        hi, mid, lo = _split3(logf)
        cs = (_dot(tril, hi) + _dot(tril, mid)) + _dot(tril, lo) + carry
        out_ref[0, :, sl] = (cs * LOG2E).T[_FF_LANE:_FF_LANE + N_FOX_HEADS, :]
        carry = cs[chunk - 1:chunk, :]


def _forget_cumsum(small, fbias, tril, *, batch, seq):
    chunk = tril.shape[0]
    return pl.pallas_call(
        functools.partial(_cumsum_body, seq=seq, chunk=chunk),
        grid=(batch,),
        in_specs=[
            pl.BlockSpec((seq, LANE), lambda b: (b, 0)),
            pl.BlockSpec((1, LANE), lambda b: (0, 0)),
            pl.BlockSpec((chunk, chunk), lambda b: (0, 0)),
        ],
        out_specs=pl.BlockSpec((1, N_FOX_HEADS, seq), lambda b: (b, 0, 0)),
        out_shape=jax.ShapeDtypeStruct((batch, N_FOX_HEADS, seq), F32),
        compiler_params=pltpu.CompilerParams(
            dimension_semantics=("arbitrary",), vmem_limit_bytes=VMEM_LIMIT),
        name="forget_cumsum",
    )(small, fbias, tril)


def _compress_body(x_ref, pos_ref, w1_ref, b1_ref, w2_ref, b2_ref, o_ref, *, n_chunk):
    half = CMP_STRIDE * HEAD_DIM
    x = jnp.concatenate(
        [x_ref[0, pl.ds(t, n_chunk, stride=CMP_STRIDE), :] for t in range(CMP_STRIDE)], axis=1)
    first = _dot((x + pos_ref[0, 0:1, :]).astype(MXU_DTYPE), w1_ref[0, 0:half, :])
    second = _dot((x + pos_ref[0, 1:2, :]).astype(MXU_DTYPE), w1_ref[0, half:2 * half, :])
    hid = first + pltpu.roll(second, n_chunk - 1, 0) + b1_ref[0]
    act = jax.nn.gelu(hid)
    o_ref[0, 0] = (_dot(act.astype(MXU_DTYPE), w2_ref[0]) + b2_ref[0]).astype(MXU_DTYPE)


def _compress(kvc, pos, w1, b1, w2p, b2p, *, batch, n_chunk):
    half = CMP_STRIDE * HEAD_DIM
    kv = lambda i, b: (i // N_NSA_KV, 0, 0)
    return pl.pallas_call(
        functools.partial(_compress_body, n_chunk=n_chunk),
        grid=(2 * N_NSA_KV, batch),
        in_specs=[
            pl.BlockSpec((1, n_chunk * CMP_STRIDE, HEAD_DIM), lambda i, b: (i, b, 0)),
            pl.BlockSpec((1, 2, half), kv),
            pl.BlockSpec((1, 2 * half, CMP_HIDDEN), kv),
            pl.BlockSpec((1, 1, CMP_HIDDEN), kv),
            pl.BlockSpec((1, CMP_HIDDEN, LANE), kv),
            pl.BlockSpec((1, 1, LANE), kv),
        ],
        out_specs=pl.BlockSpec((1, 1, n_chunk, LANE), lambda i, b: (i, b, 0, 0)),
        out_shape=jax.ShapeDtypeStruct((2 * N_NSA_KV, batch, n_chunk, LANE), MXU_DTYPE),
        compiler_params=pltpu.CompilerParams(
            dimension_semantics=("arbitrary", "arbitrary"), vmem_limit_bytes=VMEM_LIMIT),
        name="compress",
    )(kvc, pos, w1, b1, w2p, b2p)


def _lane_max(s):
    out = s[:, 0:LANE]
    for c in range(1, s.shape[1] // LANE):
        out = jnp.maximum(out, s[:, LANE * c:LANE * (c + 1)])
    return out


def _row_max_lanes(mrun):
    return jnp.broadcast_to(jnp.max(mrun, axis=-1, keepdims=True), mrun.shape)


def _exp_tiles(s, mb):
    n = s.shape[1] // LANE
    return jnp.concatenate(
        [jnp.exp2(s[:, LANE * c:LANE * (c + 1)] - mb).astype(MXU_DTYPE) for c in range(n)], axis=1)


def _values_with_ones(v, ones_low):
    low = _lanes(v.shape) < HEAD_DIM
    ones = jnp.ones_like(v)
    return jnp.where(low, ones, v) if ones_low else jnp.where(low, v, ones)


def _normalize(acc):
    return acc / pltpu.roll(acc, HEAD_DIM, 1)


def _spans(n_tiles, groups, fn):
    start = 0
    for size in tuple(groups) + (1,):
        n_groups = lax.div(n_tiles - start, size)

        def step(t, carry, size=size, start=start):
            fn(start + t * size, size)
            return carry

        lax.fori_loop(0, n_groups, step, 0)
        start = start + n_groups * size


def _nsa_body(qn_ref, qr_ref, kc_ref, vc_ref, ksa_ref, vs_ref, kw_ref, vw_ref, sm_ref, gb_ref, impt_ref,
              o_ref, s_ref, mrun_ref, acc_ref, *, tq, tk, group, n_sel, top_k):
    q0 = pl.program_id(2) * tq
    r = NSA_GROUP * tq
    n_cmp = kc_ref.shape[2]

    def window_scores(hq):
        n_wc = WINDOW // LANE + 1
        upper = _lanes((NSA_GROUP * LANE, LANE)) > (_rows((NSA_GROUP * LANE, LANE)) & (LANE - 1))
        qs = q0 + hq * LANE
        qh = jnp.concatenate(
            [qr_ref[hq * LANE:(hq + 1) * LANE, LANE * h:LANE * (h + 1)] for h in range(NSA_GROUP)], axis=0)
        starts = [pl.multiple_of(jnp.maximum(qs - WINDOW + LANE * c, 0), LANE) for c in range(n_wc)]
        kk = jnp.concatenate([kw_ref[pl.ds(st, LANE), :] for st in starts], axis=0)
        vv = jnp.concatenate([vw_ref[pl.ds(st, LANE), :] for st in starts], axis=0)
        sw = _dot_nt(qh, kk)
        chunks = [sw[:, LANE * c:LANE * (c + 1)] for c in range(n_wc)]
        chunks[0] = jnp.where(upper, chunks[0], NEG_INF)
        chunks[-1] = jnp.where(upper, NEG_INF, chunks[-1])
        for c in range(n_wc - 1):
            chunks[c] = jnp.where(qs - WINDOW + LANE * c >= 0, chunks[c], NEG_INF)
        return chunks, vv

    def window_probs(chunks):
        mb = _row_max_lanes(functools.reduce(jnp.maximum, chunks))
        return jnp.concatenate([jnp.exp2(ch - mb).astype(MXU_DTYPE) for ch in chunks], axis=1)

    def window_out(pw, vv):
        return _normalize(_dot(pw, _values_with_ones(vv, False)))

    q4n = jnp.concatenate([qn_ref[:, LANE * h:LANE * (h + 1)] for h in range(NSA_GROUP)], axis=0)
    sc = _dot_nt(q4n, kc_ref[0, 0])
    win_scores = [window_scores(0), window_scores(1)]
    last_vis = (q0 + _rows((tq, LANE)) - (CMP_LEN - 1)) >> CMP_SHIFT
    last_vis = jnp.concatenate([last_vis] * NSA_GROUP, axis=0)
    sc_chunks = [
        jnp.where(_lanes((r, LANE)) + LANE * c <= last_vis, sc[:, LANE * c:LANE * (c + 1)], NEG_INF)
        for c in range(n_cmp // LANE)]
    mb = jnp.maximum(_row_max_lanes(functools.reduce(jnp.maximum, sc_chunks)), 0.1 * NEG_INF)
    e_chunks = [jnp.exp2(ch - mb) for ch in sc_chunks]
    den = jnp.sum(functools.reduce(jnp.add, e_chunks), axis=-1, keepdims=True)
    inv = 1.0 / jnp.maximum(jnp.broadcast_to(den, (r, LANE)), 1e-30)
    pc = jnp.concatenate([ech * inv for ech in e_chunks], axis=1)
    o_cmp = _dot(pc.astype(MXU_DTYPE), vc_ref[0, 0])

    pcs = (pc[0:tq] + pc[tq:2 * tq]) + (pc[2 * tq:3 * tq] + pc[3 * tq:4 * tq])
    impt_w = impt_ref[...]
    hi, mid, lo = _split3(pcs)
    imp_t = (_dot_nt(impt_w, hi) + _dot_nt(impt_w, mid)) + _dot_nt(impt_w, lo)
    jrow = _rows((n_sel, tq))
    qpos = q0 + _lanes((n_sel, tq))
    cur = qpos >> SEL_SHIFT
    valid = jrow * SEL_BLOCK <= qpos
    forced = (jrow == 0) | (jrow == cur) | (jrow == cur - 1)
    score = jnp.where(forced, FORCED_SCORE, jnp.where(valid, imp_t, -1.0))

    sub = 8
    chunks = [score[sub * c:sub * (c + 1), :] for c in range(n_sel // sub)]
    ranks = [jnp.zeros((sub, tq), F32) for _ in chunks]
    later = _rows((sub, tq))

    win_probs = [window_probs(win_scores[0][0])]
    win_blocks = []
    for jp in range(n_sel):
        if jp == n_sel // 2:
            win_blocks.append(window_out(win_probs[0], win_scores[0][1]))
            win_probs.append(window_probs(win_scores[1][0]))
        cj, rj = divmod(jp, sub)
        rowv = chunks[cj][rj:rj + 1, :]
        for c, sc_c in enumerate(chunks):
            if c > cj:
                beats = rowv >= sc_c
            elif c < cj:
                beats = rowv > sc_c
            else:
                beats = (rowv > sc_c) | ((rowv == sc_c) & (later > rj))
            ranks[c] = ranks[c] + jnp.where(beats, 1.0, 0.0)
    sel_t = jnp.where(jnp.concatenate(ranks, axis=0) < float(top_k), 1.0, 0.0)
    sel_pad = jnp.concatenate([sel_t, jnp.zeros((LANE - n_sel, tq), F32)], axis=0)
    sel_q = pltpu.roll(sel_pad.T, HEAD_DIM, 1)
    sel_bias = ((sel_q - 1.0) * (-NEG_INF)).astype(MXU_DTYPE)

    high = _lanes((tq, LANE)) >= HEAD_DIM
    q4r = jnp.concatenate(
        [jnp.where(high, sel_bias, qr_ref[:, LANE * h:LANE * (h + 1)]) for h in range(NSA_GROUP)], axis=0)

    rowq_k = q0 + (_rows((r, tk)) & (tq - 1))
    n_full = lax.div(q0, tk)

    def score_span(kb, n_tiles, masked):
        k0 = pl.multiple_of(kb * tk, tk)
        s = _dot_nt(q4r, ksa_ref[pl.ds(k0, n_tiles * tk), :])
        if masked:
            s = jnp.where(k0 + _lanes((r, tk)) <= rowq_k, s, NEG_INF)
        for j in range(n_tiles):
            s_ref[kb + j] = s[:, j * tk:(j + 1) * tk]
        mrun_ref[...] = jnp.maximum(mrun_ref[...], _lane_max(s))

    def value_span(kb, n_tiles):
        k0 = pl.multiple_of(kb * tk, tk)
        mb = mrun_ref[...]
        p = jnp.concatenate([_exp_tiles(s_ref[kb + j], mb) for j in range(n_tiles)], axis=1)
        acc_ref[...] += _dot(p, _values_with_ones(vs_ref[pl.ds(k0, n_tiles * tk), :], False))

    win_blocks.append(window_out(win_probs[1], win_scores[1][1]))
    o_win = jnp.concatenate([blk[h * LANE:(h + 1) * LANE] for h in range(NSA_GROUP) for blk in win_blocks], axis=0)

    gate = jax.nn.sigmoid(sm_ref[...] + gb_ref[...])
    partial, gate_sel = [], []
    for h in range(NSA_GROUP):
        rs = slice(h * tq, (h + 1) * tq)
        c = N_BRANCH * h
        partial.append(gate[:, c:c + 1] * o_cmp[rs] + gate[:, c + 2:c + 3] * o_win[rs])
        gate_sel.append(jnp.broadcast_to(gate[:, c + 1:c + 2], (tq, LANE)))

    mrun_ref[...] = jnp.full((r, LANE), NEG_INF, F32)
    score_span(n_full, 1, True)
    _spans(n_full, group, lambda kb, n: score_span(kb, n, False))
    mrun_ref[...] = _row_max_lanes(mrun_ref[...])
    acc_ref[...] = jnp.zeros((r, LANE), F32)
    _spans(n_full + 1, group, value_span)
    o_sel = _normalize(acc_ref[...])

    heads = [partial[h] + gate_sel[h] * o_sel[h * tq:(h + 1) * tq] for h in range(NSA_GROUP)]
    low = _lanes((tq, LANE)) < HEAD_DIM
    for p in range(NSA_GROUP // 2):
        pair = jnp.where(low, heads[2 * p], pltpu.roll(heads[2 * p + 1], HEAD_DIM, 1))
        o_ref[:, LANE * p:LANE * (p + 1)] = pair.astype(MXU_DTYPE)


def _nsa(qn, qr, cmp_kv, ksa, vs, kw, vw, small, gbias, imp_t, *, batch, seq, tq, tk, group):
    assert tq == 2 * LANE
    n_q = seq // tq
    n_cmp = cmp_kv.shape[2]
    n_sel = seq // SEL_BLOCK
    top_k = min(SEL_TOPK, n_sel)
    gw = NSA_GROUP * LANE
    qmap = lambda b, g, i: (b * n_q + i, g)
    kvmap = lambda b, g, i: (b, g)
    r = NSA_GROUP * tq
    return pl.pallas_call(
        functools.partial(_nsa_body, tq=tq, tk=tk, group=group, n_sel=n_sel, top_k=top_k),
        grid=(batch, N_NSA_KV, n_q),
        in_specs=[
            pl.BlockSpec((tq, gw), qmap),
            pl.BlockSpec((tq, gw), qmap),
            pl.BlockSpec((1, 1, n_cmp, LANE), lambda b, g, i: (g, b, 0, 0)),
            pl.BlockSpec((1, 1, n_cmp, LANE), lambda b, g, i: (N_NSA_KV + g, b, 0, 0)),
            pl.BlockSpec((seq, LANE), kvmap),
            pl.BlockSpec((seq, LANE), kvmap),
            pl.BlockSpec((seq, LANE), kvmap),
            pl.BlockSpec((seq, LANE), kvmap),
            pl.BlockSpec((tq, LANE), qmap),
            pl.BlockSpec((1, LANE), lambda b, g, i: (0, g)),
            pl.BlockSpec((n_sel, n_cmp), lambda b, g, i: (0, 0)),
        ],
        out_specs=pl.BlockSpec((tq, NSA_GROUP * HEAD_DIM), qmap),
        out_shape=jax.ShapeDtypeStruct((batch * seq, N_NSA_HEADS * HEAD_DIM), MXU_DTYPE),
        scratch_shapes=[
            pltpu.VMEM((seq // tk, r, tk), F32),
            pltpu.VMEM((r, LANE), F32),
            pltpu.VMEM((r, LANE), F32),
        ],
        compiler_params=pltpu.CompilerParams(
            dimension_semantics=("arbitrary", "arbitrary", "arbitrary"), vmem_limit_bytes=VMEM_LIMIT),
        name="nsa_attention",
    )(qn, qr, cmp_kv, cmp_kv, ksa, vs, kw, vw, small, gbias, imp_t)


def _fox_body(q_ref, k_ref, v_ref, cum_ref, o_ref, s_ref, mrun_ref, acc_ref, *, tq, tk, group):
    i = pl.program_id(2)
    q0 = pl.multiple_of(i * tq, tq)
    qp = q_ref[...]
    low = _lanes((tq, LANE)) < HEAD_DIM
    causal_rows = q0 + _rows((tq, tk))
    n_full = lax.div(q0, tk)
    qs, cqs, cum_rows = [], [], []
    for e in range(2):
        qs.append(jnp.where(low if e == 0 else jnp.logical_not(low), qp, jnp.zeros_like(qp)))
        cum_rows.append(2 * pl.program_id(1) + e)
        cq = jnp.broadcast_to(cum_ref[0, pl.ds(cum_rows[e], 1), pl.ds(q0, tq)], (LANE, tq)).T
        cqs.append(jnp.concatenate([cq] * (tk // LANE), axis=1))

    def score_span(kb, n_tiles, masked):
        k0 = pl.multiple_of(kb * tk, tk)
        k = k_ref[pl.ds(k0, n_tiles * tk), :]
        for e in range(2):
            s = _dot_nt(qs[e], k)
            for j in range(n_tiles):
                ck = cum_ref[0, pl.ds(cum_rows[e], 1), pl.ds(k0 + j * tk, tk)]
                sj = s[:, j * tk:(j + 1) * tk] + cqs[e] - ck
                if masked:
                    sj = jnp.where(k0 + _lanes((tq, tk)) <= causal_rows, sj, NEG_INF)
                s_ref[e, kb + j] = sj
                mrun_ref[e] = jnp.maximum(mrun_ref[e], _lane_max(sj))

    def value_span(kb, n_tiles):
        k0 = pl.multiple_of(kb * tk, tk)
        v = v_ref[pl.ds(k0, n_tiles * tk), :]
        for e in range(2):
            mb = mrun_ref[e]
            p = jnp.concatenate([_exp_tiles(s_ref[e, kb + j], mb) for j in range(n_tiles)], axis=1)
            acc_ref[e] += _dot(p, _values_with_ones(v, e == 1))

    mrun_ref[...] = jnp.full((2, tq, LANE), NEG_INF, F32)
    score_span(n_full, 1, True)
    _spans(n_full, group, lambda kb, n: score_span(kb, n, False))
    for e in range(2):
        mrun_ref[e] = _row_max_lanes(mrun_ref[e])
    acc_ref[...] = jnp.zeros((2, tq, LANE), F32)
    _spans(n_full + 1, group, value_span)
    o_ref[...] = jnp.where(low, _normalize(acc_ref[0]), _normalize(acc_ref[1])).astype(MXU_DTYPE)


def _fox(fq, fk, fv, cum, *, batch, seq, tq, tk, group):
    n_q = seq // tq
    n_pairs = N_FOX_HEADS // 2
    qmap = lambda b, p, i: (b * n_q + i, p)
    kvmap = lambda b, p, i: (b, p)
    return pl.pallas_call(
        functools.partial(_fox_body, tq=tq, tk=tk, group=group),
        grid=(batch, n_pairs, n_q),
        in_specs=[
            pl.BlockSpec((tq, LANE), qmap),
            pl.BlockSpec((seq, LANE), kvmap),
            pl.BlockSpec((seq, LANE), kvmap),
            pl.BlockSpec((1, N_FOX_HEADS, seq), lambda b, p, i: (b, 0, 0)),
        ],
        out_specs=pl.BlockSpec((tq, LANE), qmap),
        out_shape=jax.ShapeDtypeStruct((batch * seq, N_FOX_HEADS * HEAD_DIM), MXU_DTYPE),
        scratch_shapes=[
            pltpu.VMEM((2, seq // tk, tq, tk), F32),
            pltpu.VMEM((2, tq, LANE), F32),
            pltpu.VMEM((2, tq, LANE), F32),
        ],
        compiler_params=pltpu.CompilerParams(
            dimension_semantics=("arbitrary", "arbitrary", "arbitrary"), vmem_limit_bytes=VMEM_LIMIT),
        name="fox_attention",
    )(fq, fk, fv, cum)


def _post_body(on_ref, of_ref, x_ref, wo_ref, gpm_ref, gpl_ref, wu_ref, wd_ref, gpo_ref, o_ref, *, ff_chunk):
    n_nsa = on_ref.shape[1]
    mix = _dot(on_ref[...], wo_ref[0:n_nsa, :]) + _dot(of_ref[...], wo_ref[n_nsa:, :])
    x1 = x_ref[...] + _rms(mix, gpm_ref[...])
    h = _rms(x1, gpl_ref[...]).astype(MXU_DTYPE)
    d_ff = wu_ref.shape[1]
    acc = jnp.zeros(x1.shape, F32)
    for c in range(d_ff // ff_chunk):
        sl = slice(c * ff_chunk, (c + 1) * ff_chunk)
        a = jnp.square(jnp.maximum(_dot(h, wu_ref[:, sl]), 0.0))
        acc = acc + _dot(a.astype(MXU_DTYPE), wd_ref[sl, :])
    o_ref[...] = x1 + _rms(acc, gpo_ref[...])


def _post(o_nsa, o_fox, xf, wo, g_post_mix, g_pre_mlp, wu, wd, g_post_mlp, *, layer, tm, ff_chunk):
    m, d = xf.shape
    d_ff = wu.shape[2]
    row = lambda i: (i, 0)
    fixed = lambda i: (0, 0)
    resident = lambda rows, cols: pl.BlockSpec(
        (None, rows, cols), lambda i: (layer, 0, 0), pipeline_mode=pl.Buffered(1))
    return pl.pallas_call(
        functools.partial(_post_body, ff_chunk=ff_chunk),
        grid=(m // tm,),
        in_specs=[
            pl.BlockSpec((tm, o_nsa.shape[1]), row),
            pl.BlockSpec((tm, o_fox.shape[1]), row),
            pl.BlockSpec((tm, d), row),
            resident(wo.shape[1], d),
            pl.BlockSpec((1, d), fixed),
            pl.BlockSpec((1, d), fixed),
            resident(d, d_ff),
            resident(d_ff, d),
            pl.BlockSpec((1, d), fixed),
        ],
        out_specs=pl.BlockSpec((tm, d), row),
        out_shape=jax.ShapeDtypeStruct((m, d), F32),
        compiler_params=pltpu.CompilerParams(
            dimension_semantics=("arbitrary",), vmem_limit_bytes=VMEM_LIMIT),
        name="outproj_mlp",
    )(o_nsa, o_fox, xf, wo, g_post_mix, g_pre_mlp, wu, wd, g_post_mlp)


def _rope_tables(seq):
    half = ROPE_DIMS // 2
    inv_freq = ROPE_THETA ** (-jnp.arange(half, dtype=F32) * 2.0 / ROPE_DIMS)
    ang = jnp.arange(seq).astype(F32)[:, None] * inv_freq[None, :]
    cos, sin = jnp.cos(ang), jnp.sin(ang)
    pad = jnp.zeros((seq, LANE - ROPE_DIMS), F32)
    zero = jnp.zeros((seq, half), F32)
    c = jnp.concatenate([cos, cos, pad + 1.0], axis=-1)
    a = jnp.concatenate([-sin, zero, pad], axis=-1)
    b = jnp.concatenate([zero, sin, pad], axis=-1)
    return c, a, b


def _importance_map_t(seq, n_chunk):
    n_cmp = (seq - CMP_LEN) // CMP_STRIDE + 1
    n_sel = seq // SEL_BLOCK
    cs = np.arange(n_cmp)[None, :] * CMP_STRIDE
    ss = np.arange(n_sel)[:, None] * SEL_BLOCK
    overlap = np.clip(np.minimum(cs + CMP_LEN, ss + SEL_BLOCK) - np.maximum(cs, ss), 0, None) / CMP_LEN
    out = np.zeros((n_sel, n_chunk), np.float32)
    out[:, :n_cmp] = overlap
    return jnp.asarray(out, dtype=MXU_DTYPE)


def kernel(x, w_in, b_nsa_gate, b_forget, cmp_pos_k, cmp_w1_k, cmp_b1_k, cmp_w2_k, cmp_b2_k, cmp_pos_v, cmp_w1_v, cmp_b1_v, cmp_w2_v, cmp_b2_v, w_out, w_up, w_down, g_pre_mix, g_post_mix, g_pre_mlp, g_post_mlp):
    batch, seq, d_model = x.shape
    depth = w_in.shape[0]
    m = batch * seq
    n_chunk = seq // CMP_STRIDE
    n_sel = seq // SEL_BLOCK
    tm = min(512, seq)
    nsa_tq, nsa_tk = 2 * LANE, min(256, seq)
    fox_t = min(512, seq)
    assert seq % tm == 0 and seq % fox_t == 0 and seq % nsa_tk == 0
    assert n_sel <= HEAD_DIM and n_sel % 8 == 0 and seq >= WINDOW + nsa_tq
    assert x.dtype == F32

    rope_c, rope_a, rope_b = _rope_tables(seq)
    imp_t = _importance_map_t(seq, n_chunk)
    cs_chunk = min(256, seq)
    tril = jnp.asarray(np.tril(np.ones((cs_chunk, cs_chunk), np.float32)), dtype=MXU_DTYPE)

    wp = _permute_columns(w_in.astype(MXU_DTYPE), _projection_columns())

    gate_bias = jnp.zeros((depth, 1, N_NSA_KV * LANE), F32)
    fbias = jnp.zeros((depth, 1, LANE), F32)
    n_gate = NSA_GROUP * N_BRANCH
    for g in range(N_NSA_KV):
        gate_bias = gate_bias.at[:, 0, LANE * g:LANE * g + n_gate].set(b_nsa_gate[:, n_gate * g:n_gate * (g + 1)])
    fbias = fbias.at[:, 0, _FF_LANE:_FF_LANE + N_FOX_HEADS].set(b_forget)

    half = CMP_STRIDE * HEAD_DIM
    cpos = jnp.stack([cmp_pos_k.reshape(depth, 2, half), cmp_pos_v.reshape(depth, 2, half)], axis=1)
    cw1 = jnp.stack([cmp_w1_k, cmp_w1_v], axis=1).astype(MXU_DTYPE)
    cb1 = jnp.stack([cmp_b1_k, cmp_b1_v], axis=1)[:, :, None, :]
    zw = jnp.zeros_like(cmp_w2_k)
    cw2 = jnp.stack([jnp.concatenate([cmp_w2_k, zw], axis=-1),
                     jnp.concatenate([cmp_w2_v, cmp_w2_v], axis=-1)], axis=1).astype(MXU_DTYPE)
    zb = jnp.zeros_like(cmp_b2_k)
    cb2 = jnp.stack([jnp.concatenate([cmp_b2_k, zb], axis=-1),
                     jnp.concatenate([cmp_b2_v, cmp_b2_v], axis=-1)], axis=1)[:, :, None, :]

    wo = w_out.astype(MXU_DTYPE)
    wu = w_up.astype(MXU_DTYPE)
    wd = w_down.astype(MXU_DTYPE)

    xf = x.reshape(m, d_model)
    for l in range(depth):
        qn, qr, kvc, ksa, vs, kw, vw, fq, fk, fv, small = _inproj(
            xf, g_pre_mix[l][None, :], wp, rope_c, rope_a, rope_b, layer=l, seq=seq, tm=tm)

        cum = _forget_cumsum(small, fbias[l], tril, batch=batch, seq=seq)

        cmp_kv = _compress(kvc, cpos[l], cw1[l], cb1[l], cw2[l], cb2[l],
                           batch=batch, n_chunk=n_chunk)
        o_nsa = _nsa(qn, qr, cmp_kv, ksa, vs, kw, vw, small, gate_bias[l], imp_t,
                     batch=batch, seq=seq, tq=nsa_tq, tk=nsa_tk, group=(8, 4, 2))
        o_fox = _fox(fq, fk, fv, cum, batch=batch, seq=seq, tq=fox_t, tk=fox_t, group=(4, 2))
        xf = _post(o_nsa, o_fox, xf, wo, g_post_mix[l][None, :], g_pre_mlp[l][None, :],
                   wu, wd, g_post_mlp[l][None, :], layer=l, tm=tm, ff_chunk=1024)
    return xf.reshape(batch, seq, d_model)
```

```python
import functools

import numpy as np
import jax
import jax.numpy as jnp
from jax import lax
from jax.experimental import pallas as pl
from jax.experimental.pallas import tpu as pltpu

HEAD_DIM = 64
N_NSA_HEADS = 8
N_NSA_KV = 2
NSA_GROUP = N_NSA_HEADS // N_NSA_KV
N_FOX_HEADS = 8
N_BRANCH = 3
CMP_LEN = 32
CMP_STRIDE = 16
CMP_SHIFT = 4
CMP_HIDDEN = 256
SEL_BLOCK = 64
SEL_SHIFT = 6
SEL_TOPK = 16
WINDOW = 512
ROPE_THETA = 500000.0
ROPE_DIMS = HEAD_DIM // 4
NORM_EPS = 1e-6
NEG_INF = -1e30
FORCED_SCORE = 1e4
SCALE = HEAD_DIM ** -0.5
LOG2E = 1.4426950408889634
Q_SCALE = SCALE * LOG2E

LANE = 128
SUBLANE = 8
MXU_DTYPE = jnp.bfloat16
F32 = jnp.float32
V7X_VMEM_BYTES = 64 * 1024 * 1024
VMEM_LIMIT = V7X_VMEM_BYTES * 7 // 8

_SRC_NQ, _SRC_KC, _SRC_VC, _SRC_KSEL, _SRC_VSEL, _SRC_KWIN, _SRC_VWIN = 0, 512, 640, 768, 896, 1024, 1152
_SRC_GATE, _SRC_FQ, _SRC_FK, _SRC_FV, _SRC_FF = 1280, 1304, 1816, 2328, 2840
_Q0, _KVC0, _KS0, _VS0, _KW0, _VW0, _FQ0, _FK0, _FV0, _SM0, _NP = (
    0, 1024, 1536, 1792, 2048, 2304, 2560, 3072, 3584, 4096, 4352)
_FF_LANE = 16


def _projection_columns():
    src = np.full((_NP,), -1, np.int32)
    d = np.arange(HEAD_DIM)
    for h in range(N_NSA_HEADS):
        src[_Q0 + LANE * h + d] = _SRC_NQ + HEAD_DIM * h + d
    for i in range(4):
        src[_KVC0 + LANE * i + d] = _SRC_KC + HEAD_DIM * i + d
    for g in range(N_NSA_KV):
        src[_KS0 + LANE * g + d] = _SRC_KSEL + HEAD_DIM * g + d
        src[_KW0 + LANE * g + d] = _SRC_KWIN + HEAD_DIM * g + d
        for half in range(2):
            src[_VS0 + LANE * g + HEAD_DIM * half + d] = _SRC_VSEL + HEAD_DIM * g + d
            src[_VW0 + LANE * g + HEAD_DIM * half + d] = _SRC_VWIN + HEAD_DIM * g + d
        gates = np.arange(NSA_GROUP * N_BRANCH)
        src[_SM0 + LANE * g + gates] = _SRC_GATE + NSA_GROUP * N_BRANCH * g + gates
        src[_SM0 + LANE * g + _FF_LANE + np.arange(N_FOX_HEADS)] = _SRC_FF + np.arange(N_FOX_HEADS)
    fox = np.arange(N_FOX_HEADS * HEAD_DIM)
    src[_FQ0 + fox] = _SRC_FQ + fox
    src[_FK0 + fox] = _SRC_FK + fox
    src[_FV0 + fox] = _SRC_FV + fox
    return src


def _permute_columns(w, src):
    pieces, i = [], 0
    while i < len(src):
        j = i + 1
        if src[i] < 0:
            while j < len(src) and src[j] < 0:
                j += 1
            pieces.append(jnp.zeros(w.shape[:-1] + (j - i,), w.dtype))
        else:
            while j < len(src) and src[j] == src[j - 1] + 1:
                j += 1
            pieces.append(w[..., int(src[i]):int(src[i]) + (j - i)])
        i = j
    return jnp.concatenate(pieces, axis=-1)


def _rms(x, g):
    ms = jnp.mean(x * x, axis=-1, keepdims=True)
    return x * lax.rsqrt(ms + NORM_EPS) * g


def _dot(a, b):
    return jnp.dot(a, b, preferred_element_type=F32)


def _dot_nt(a, b):
    return lax.dot_general(a, b, (((1,), (1,)), ((), ())), preferred_element_type=F32)


def _split3(x):
    hi = x.astype(MXU_DTYPE)
    r1 = x - hi.astype(F32)
    mid = r1.astype(MXU_DTYPE)
    lo = (r1 - mid.astype(F32)).astype(MXU_DTYPE)
    return hi, mid, lo


def _rows(shape):
    return lax.broadcasted_iota(jnp.int32, shape, 0)


def _lanes(shape):
    return lax.broadcasted_iota(jnp.int32, shape, 1)


def _inproj_body(x_ref, g_ref, w_ref, rc_ref, ra_ref, rb_ref,
                 qn_ref, qr_ref, kvc_ref, ksa_ref, vs_ref, kw_ref, vw_ref,
                 fq_ref, fk_ref, fv_ref, small_ref, *, tm, n_s_tiles):
    h = _rms(x_ref[...], g_ref[...]).astype(MXU_DTYPE)
    rc, ra, rb = rc_ref[...], ra_ref[...], rb_ref[...]

    def mm(lo, n):
        return _dot(h, w_ref[:, lo:lo + n])

    def rope(v):
        return v * rc + pltpu.roll(v, LANE - ROPE_DIMS // 2, 1) * ra + pltpu.roll(v, ROPE_DIMS // 2, 1) * rb

    q = mm(_Q0, N_NSA_HEADS * LANE) * Q_SCALE
    qn_ref[...] = q.astype(MXU_DTYPE)
    for hh in range(N_NSA_HEADS):
        sl = slice(LANE * hh, LANE * (hh + 1))
        qr_ref[:, sl] = rope(q[:, sl]).astype(MXU_DTYPE)

    kvc = mm(_KVC0, 4 * LANE)
    for i in range(4):
        kvc_ref[i] = kvc[:, LANE * i:LANE * i + HEAD_DIM]

    pos = (pl.program_id(0) % n_s_tiles) * tm + _rows((tm, LANE))
    onehot = ((_lanes((tm, LANE)) - HEAD_DIM) == (pos >> SEL_SHIFT)).astype(F32)
    ks = mm(_KS0, N_NSA_KV * LANE)
    kw = mm(_KW0, N_NSA_KV * LANE)
    for g in range(N_NSA_KV):
        sl = slice(LANE * g, LANE * (g + 1))
        ksa_ref[:, sl] = (rope(ks[:, sl]) + onehot).astype(MXU_DTYPE)
        kw_ref[:, sl] = rope(kw[:, sl]).astype(MXU_DTYPE)
    vs_ref[...] = mm(_VS0, N_NSA_KV * LANE).astype(MXU_DTYPE)
    vw_ref[...] = mm(_VW0, N_NSA_KV * LANE).astype(MXU_DTYPE)

    nf = N_FOX_HEADS * HEAD_DIM
    fq_ref[...] = (mm(_FQ0, nf) * Q_SCALE).astype(MXU_DTYPE)
    fk_ref[...] = mm(_FK0, nf).astype(MXU_DTYPE)
    fv_ref[...] = mm(_FV0, nf).astype(MXU_DTYPE)
    small_ref[...] = mm(_SM0, N_NSA_KV * LANE)


def _inproj(xf, g, wp, rope_c, rope_a, rope_b, *, layer, seq, tm):
    m, d = xf.shape
    n_s_tiles = seq // tm
    nf = N_FOX_HEADS * HEAD_DIM
    row = lambda i: (i, 0)
    fixed = lambda i: (0, 0)
    tab = lambda i: (i % n_s_tiles, 0)
    out_shape = (
        jax.ShapeDtypeStruct((m, N_NSA_HEADS * LANE), MXU_DTYPE),
        jax.ShapeDtypeStruct((m, N_NSA_HEADS * LANE), MXU_DTYPE),
        jax.ShapeDtypeStruct((4, m, HEAD_DIM), F32),
        jax.ShapeDtypeStruct((m, N_NSA_KV * LANE), MXU_DTYPE),
        jax.ShapeDtypeStruct((m, N_NSA_KV * LANE), MXU_DTYPE),
        jax.ShapeDtypeStruct((m, N_NSA_KV * LANE), MXU_DTYPE),
        jax.ShapeDtypeStruct((m, N_NSA_KV * LANE), MXU_DTYPE),
        jax.ShapeDtypeStruct((m, nf), MXU_DTYPE),
        jax.ShapeDtypeStruct((m, nf), MXU_DTYPE),
        jax.ShapeDtypeStruct((m, nf), MXU_DTYPE),
        jax.ShapeDtypeStruct((m, N_NSA_KV * LANE), F32),
    )
    out_specs = (
        pl.BlockSpec((tm, N_NSA_HEADS * LANE), row),
        pl.BlockSpec((tm, N_NSA_HEADS * LANE), row),
        pl.BlockSpec((4, tm, HEAD_DIM), lambda i: (0, i, 0)),
        pl.BlockSpec((tm, N_NSA_KV * LANE), row),
        pl.BlockSpec((tm, N_NSA_KV * LANE), row),
        pl.BlockSpec((tm, N_NSA_KV * LANE), row),
        pl.BlockSpec((tm, N_NSA_KV * LANE), row),
        pl.BlockSpec((tm, nf), row),
        pl.BlockSpec((tm, nf), row),
        pl.BlockSpec((tm, nf), row),
        pl.BlockSpec((tm, N_NSA_KV * LANE), row),
    )
    return pl.pallas_call(
        functools.partial(_inproj_body, tm=tm, n_s_tiles=n_s_tiles),
        grid=(m // tm,),
        in_specs=[
            pl.BlockSpec((tm, d), row),
            pl.BlockSpec((1, d), fixed),
            pl.BlockSpec((None, d, _NP), lambda i: (layer, 0, 0)),
            pl.BlockSpec((tm, LANE), tab),
            pl.BlockSpec((tm, LANE), tab),
            pl.BlockSpec((tm, LANE), tab),
        ],
        out_specs=out_specs,
        out_shape=out_shape,
        compiler_params=pltpu.CompilerParams(
            dimension_semantics=("arbitrary",), vmem_limit_bytes=VMEM_LIMIT),
        name="inproj",
    )(xf, g, wp, rope_c, rope_a, rope_b)


def _cumsum_body(sm_ref, bias_ref, tril_ref, out_ref, *, seq, chunk):
    carry = jnp.zeros((1, LANE), F32)
    tril = tril_ref[...]
    for c in range(seq // chunk):
        sl = slice(c * chunk, (c + 1) * chunk)
        z = sm_ref[sl, :] + bias_ref[...]
        logf = jnp.minimum(z, 0.0) - jnp.log1p(jnp.exp(-jnp.abs(z)))
        hi, mid, lo = _split3(logf)
        cs = (_dot(tril, hi) + _dot(tril, mid)) + _dot(tril, lo) + carry
        out_ref[0, :, sl] = (cs * LOG2E).T[_FF_LANE:_FF_LANE + N_FOX_HEADS, :]
        carry = cs[chunk - 1:chunk, :]


def _forget_cumsum(small, fbias, tril, *, batch, seq):
    chunk = tril.shape[0]
    return pl.pallas_call(
        functools.partial(_cumsum_body, seq=seq, chunk=chunk),
        grid=(batch,),
        in_specs=[
            pl.BlockSpec((seq, LANE), lambda b: (b, 0)),
            pl.BlockSpec((1, LANE), lambda b: (0, 0)),
            pl.BlockSpec((chunk, chunk), lambda b: (0, 0)),
        ],
        out_specs=pl.BlockSpec((1, N_FOX_HEADS, seq), lambda b: (b, 0, 0)),
        out_shape=jax.ShapeDtypeStruct((batch, N_FOX_HEADS, seq), F32),
        compiler_params=pltpu.CompilerParams(
            dimension_semantics=("arbitrary",), vmem_limit_bytes=VMEM_LIMIT),
        name="forget_cumsum",
    )(small, fbias, tril)


def _compress_body(x_ref, pos_ref, w1_ref, b1_ref, w2_ref, b2_ref, o_ref, *, n_chunk):
    half = CMP_STRIDE * HEAD_DIM
    x = jnp.concatenate(
        [x_ref[0, pl.ds(t, n_chunk, stride=CMP_STRIDE), :] for t in range(CMP_STRIDE)], axis=1)
    first = _dot((x + pos_ref[0, 0:1, :]).astype(MXU_DTYPE), w1_ref[0, 0:half, :])
    second = _dot((x + pos_ref[0, 1:2, :]).astype(MXU_DTYPE), w1_ref[0, half:2 * half, :])
    hid = first + pltpu.roll(second, n_chunk - 1, 0) + b1_ref[0]
    act = jax.nn.gelu(hid)
    o_ref[0, 0] = (_dot(act.astype(MXU_DTYPE), w2_ref[0]) + b2_ref[0]).astype(MXU_DTYPE)


def _compress(kvc, pos, w1, b1, w2p, b2p, *, batch, n_chunk):
    half = CMP_STRIDE * HEAD_DIM
    kv = lambda i, b: (i // N_NSA_KV, 0, 0)
    return pl.pallas_call(
        functools.partial(_compress_body, n_chunk=n_chunk),
        grid=(2 * N_NSA_KV, batch),
        in_specs=[
            pl.BlockSpec((1, n_chunk * CMP_STRIDE, HEAD_DIM), lambda i, b: (i, b, 0)),
            pl.BlockSpec((1, 2, half), kv),
            pl.BlockSpec((1, 2 * half, CMP_HIDDEN), kv),
            pl.BlockSpec((1, 1, CMP_HIDDEN), kv),
            pl.BlockSpec((1, CMP_HIDDEN, LANE), kv),
            pl.BlockSpec((1, 1, LANE), kv),
        ],
        out_specs=pl.BlockSpec((1, 1, n_chunk, LANE), lambda i, b: (i, b, 0, 0)),
        out_shape=jax.ShapeDtypeStruct((2 * N_NSA_KV, batch, n_chunk, LANE), MXU_DTYPE),
        compiler_params=pltpu.CompilerParams(
            dimension_semantics=("arbitrary", "arbitrary"), vmem_limit_bytes=VMEM_LIMIT),
        name="compress",
    )(kvc, pos, w1, b1, w2p, b2p)


def _lane_max(s):
    out = s[:, 0:LANE]
    for c in range(1, s.shape[1] // LANE):
        out = jnp.maximum(out, s[:, LANE * c:LANE * (c + 1)])
    return out


def _row_max_lanes(mrun):
    return jnp.broadcast_to(jnp.max(mrun, axis=-1, keepdims=True), mrun.shape)


def _exp_tiles(s, mb):
    n = s.shape[1] // LANE
    return jnp.concatenate(
        [jnp.exp2(s[:, LANE * c:LANE * (c + 1)] - mb).astype(MXU_DTYPE) for c in range(n)], axis=1)


def _values_with_ones(v, ones_low):
    low = _lanes(v.shape) < HEAD_DIM
    ones = jnp.ones_like(v)
    return jnp.where(low, ones, v) if ones_low else jnp.where(low, v, ones)


def _normalize(acc):
    return acc / pltpu.roll(acc, HEAD_DIM, 1)


def _spans(n_tiles, groups, fn):
    start = 0
    for size in tuple(groups) + (1,):
        n_groups = lax.div(n_tiles - start, size)

        def step(t, carry, size=size, start=start):
            fn(start + t * size, size)
            return carry

        lax.fori_loop(0, n_groups, step, 0)
        start = start + n_groups * size


def _nsa_body(qn_ref, qr_ref, kc_ref, vc_ref, ksa_ref, vs_ref, kw_ref, vw_ref, sm_ref, gb_ref, impt_ref,
              o_ref, s_ref, mrun_ref, acc_ref, *, tq, tk, group, n_sel, top_k):
    q0 = pl.program_id(2) * tq
    r = NSA_GROUP * tq
    n_cmp = kc_ref.shape[2]

    def window_scores(hq):
        n_wc = WINDOW // LANE + 1
        upper = _lanes((NSA_GROUP * LANE, LANE)) > (_rows((NSA_GROUP * LANE, LANE)) & (LANE - 1))
        qs = q0 + hq * LANE
        qh = jnp.concatenate(
            [qr_ref[hq * LANE:(hq + 1) * LANE, LANE * h:LANE * (h + 1)] for h in range(NSA_GROUP)], axis=0)
        starts = [pl.multiple_of(jnp.maximum(qs - WINDOW + LANE * c, 0), LANE) for c in range(n_wc)]
        flag_lane = _lanes((LANE, LANE)) == HEAD_DIM
        k_chunks = []
        for c, st in enumerate(starts):
            kc = kw_ref[pl.ds(st, LANE), :]
            if c < n_wc - 1:
                before_start = jnp.logical_and(flag_lane, qs - WINDOW + LANE * c < 0)
                kc = jnp.where(before_start, jnp.ones_like(kc), kc)
            k_chunks.append(kc)
        kk = jnp.concatenate(k_chunks, axis=0)
        vv = jnp.concatenate([vw_ref[pl.ds(st, LANE), :] for st in starts], axis=0)
        q_flag = _lanes(qh.shape) == HEAD_DIM
        sw = _dot_nt(jnp.where(q_flag, jnp.full_like(qh, NEG_INF), qh), kk)
        chunks = [sw[:, LANE * c:LANE * (c + 1)] for c in range(n_wc)]
        chunks[0] = jnp.where(upper, chunks[0], NEG_INF)
        chunks[-1] = jnp.where(upper, NEG_INF, chunks[-1])
        return chunks, vv

    def window_probs(chunks):
        mb = _row_max_lanes(functools.reduce(jnp.maximum, chunks))
        return jnp.concatenate([jnp.exp2(ch - mb).astype(MXU_DTYPE) for ch in chunks], axis=1)

    def window_out(pw, vv):
        return _normalize(_dot(pw, _values_with_ones(vv, False)))

    q4n = jnp.concatenate([qn_ref[:, LANE * h:LANE * (h + 1)] for h in range(NSA_GROUP)], axis=0)
    sc = _dot_nt(q4n, kc_ref[0, 0])
    win_scores = [window_scores(0), window_scores(1)]
    last_vis = (q0 + _rows((tq, LANE)) - (CMP_LEN - 1)) >> CMP_SHIFT
    last_vis = jnp.concatenate([last_vis] * NSA_GROUP, axis=0)
    sc_chunks = [
        jnp.where(_lanes((r, LANE)) + LANE * c <= last_vis, sc[:, LANE * c:LANE * (c + 1)], NEG_INF)
        for c in range(n_cmp // LANE)]
    mb = jnp.maximum(_row_max_lanes(functools.reduce(jnp.maximum, sc_chunks)), 0.1 * NEG_INF)
    e_chunks = [jnp.exp2(ch - mb) for ch in sc_chunks]
    den = jnp.sum(functools.reduce(jnp.add, e_chunks), axis=-1, keepdims=True)
    inv = 1.0 / jnp.maximum(jnp.broadcast_to(den, (r, LANE)), 1e-30)
    pc = jnp.concatenate([ech * inv for ech in e_chunks], axis=1)
    o_cmp = _dot(pc.astype(MXU_DTYPE), vc_ref[0, 0])

    pcs = (pc[0:tq] + pc[tq:2 * tq]) + (pc[2 * tq:3 * tq] + pc[3 * tq:4 * tq])
    impt_w = impt_ref[...]
    hi, mid, lo = _split3(pcs)
    imp_t = (_dot_nt(impt_w, hi) + _dot_nt(impt_w, mid)) + _dot_nt(impt_w, lo)
    jrow = _rows((n_sel, tq))
    qpos = q0 + _lanes((n_sel, tq))
    cur = qpos >> SEL_SHIFT
    valid = jrow * SEL_BLOCK <= qpos
    forced = (jrow == 0) | (jrow == cur) | (jrow == cur - 1)
    score = jnp.where(forced, FORCED_SCORE, jnp.where(valid, imp_t, -1.0))

    sub = SUBLANE
    chunks = [score[sub * c:sub * (c + 1), :] for c in range(n_sel // sub)]
    ranks = [jnp.zeros((sub, tq), F32) for _ in chunks]
    later = _rows((sub, tq))

    win_probs = [window_probs(win_scores[0][0])]
    win_blocks = []
    for jp in range(n_sel):
        if jp == n_sel // 2:
            win_blocks.append(window_out(win_probs[0], win_scores[0][1]))
            win_probs.append(window_probs(win_scores[1][0]))
        cj, rj = divmod(jp, sub)
        rowv = chunks[cj][rj:rj + 1, :]
        for c, sc_c in enumerate(chunks):
            if c > cj:
                beats = rowv >= sc_c
            elif c < cj:
                beats = rowv > sc_c
            else:
                beats = (rowv > sc_c) | ((rowv == sc_c) & (later > rj))
            ranks[c] = ranks[c] + jnp.where(beats, 1.0, 0.0)
    sel_t = jnp.where(jnp.concatenate(ranks, axis=0) < float(top_k), 1.0, 0.0)
    sel_pad = jnp.concatenate([sel_t, jnp.zeros((LANE - n_sel, tq), F32)], axis=0)
    sel_q = pltpu.roll(sel_pad.T, HEAD_DIM, 1)
    sel_bias = ((sel_q - 1.0) * (-NEG_INF)).astype(MXU_DTYPE)

    high = _lanes((tq, LANE)) >= HEAD_DIM
    q4r = jnp.concatenate(
        [jnp.where(high, sel_bias, qr_ref[:, LANE * h:LANE * (h + 1)]) for h in range(NSA_GROUP)], axis=0)

    n_full = lax.div(q0, tk)
    assert tq == tk

    def score_span(kb, n_tiles, masked):
        k0 = pl.multiple_of(kb * tk, tk)
        s = _dot_nt(q4r, ksa_ref[pl.ds(k0, n_tiles * tk), :])
        if masked:
            s = jnp.where(_lanes((r, tk)) <= (_rows((r, tk)) & (tq - 1)), s, NEG_INF)
        for j in range(n_tiles):
            s_ref[kb + j] = s[:, j * tk:(j + 1) * tk]
        mrun_ref[...] = jnp.maximum(mrun_ref[...], _lane_max(s))

    def value_span(kb, n_tiles):
        k0 = pl.multiple_of(kb * tk, tk)
        mb = mrun_ref[...]
        p = jnp.concatenate([_exp_tiles(s_ref[kb + j], mb) for j in range(n_tiles)], axis=1)
        acc_ref[...] += _dot(p, _values_with_ones(vs_ref[pl.ds(k0, n_tiles * tk), :], False))

    win_blocks.append(window_out(win_probs[1], win_scores[1][1]))
    o_win = jnp.concatenate([blk[h * LANE:(h + 1) * LANE] for h in range(NSA_GROUP) for blk in win_blocks], axis=0)

    gate = jax.nn.sigmoid(sm_ref[...] + gb_ref[...])
    partial, gate_sel = [], []
    for h in range(NSA_GROUP):
        rs = slice(h * tq, (h + 1) * tq)
        c = N_BRANCH * h
        partial.append(gate[:, c:c + 1] * o_cmp[rs] + gate[:, c + 2:c + 3] * o_win[rs])
        gate_sel.append(jnp.broadcast_to(gate[:, c + 1:c + 2], (tq, LANE)))

    mrun_ref[...] = jnp.full((r, LANE), NEG_INF, F32)
    score_span(n_full, 1, True)
    _spans(n_full, group, lambda kb, n: score_span(kb, n, False))
    mrun_ref[...] = _row_max_lanes(mrun_ref[...])
    acc_ref[...] = jnp.zeros((r, LANE), F32)
    _spans(n_full + 1, group, value_span)
    o_sel = _normalize(acc_ref[...])

    heads = [partial[h] + gate_sel[h] * o_sel[h * tq:(h + 1) * tq] for h in range(NSA_GROUP)]
    low = _lanes((tq, LANE)) < HEAD_DIM
    for p in range(NSA_GROUP // 2):
        pair = jnp.where(low, heads[2 * p], pltpu.roll(heads[2 * p + 1], HEAD_DIM, 1))
        o_ref[:, LANE * p:LANE * (p + 1)] = pair.astype(MXU_DTYPE)


def _nsa(qn, qr, cmp_kv, ksa, vs, kw, vw, small, gbias, imp_t, *, batch, seq, tq, tk, group):
    assert tq == 2 * LANE
    n_q = seq // tq
    n_cmp = cmp_kv.shape[2]
    n_sel = seq // SEL_BLOCK
    top_k = min(SEL_TOPK, n_sel)
    gw = NSA_GROUP * LANE
    qmap = lambda b, g, i: (b * n_q + i, g)
    kvmap = lambda b, g, i: (b, g)
    r = NSA_GROUP * tq
    return pl.pallas_call(
        functools.partial(_nsa_body, tq=tq, tk=tk, group=group, n_sel=n_sel, top_k=top_k),
        grid=(batch, N_NSA_KV, n_q),
        in_specs=[
            pl.BlockSpec((tq, gw), qmap),
            pl.BlockSpec((tq, gw), qmap),
            pl.BlockSpec((1, 1, n_cmp, LANE), lambda b, g, i: (g, b, 0, 0)),
            pl.BlockSpec((1, 1, n_cmp, LANE), lambda b, g, i: (N_NSA_KV + g, b, 0, 0)),
            pl.BlockSpec((seq, LANE), kvmap),
            pl.BlockSpec((seq, LANE), kvmap),
            pl.BlockSpec((seq, LANE), kvmap),
            pl.BlockSpec((seq, LANE), kvmap),
            pl.BlockSpec((tq, LANE), qmap),
            pl.BlockSpec((1, LANE), lambda b, g, i: (0, g)),
            pl.BlockSpec((n_sel, n_cmp), lambda b, g, i: (0, 0)),
        ],
        out_specs=pl.BlockSpec((tq, NSA_GROUP * HEAD_DIM), qmap),
        out_shape=jax.ShapeDtypeStruct((batch * seq, N_NSA_HEADS * HEAD_DIM), MXU_DTYPE),
        scratch_shapes=[
            pltpu.VMEM((seq // tk, r, tk), F32),
            pltpu.VMEM((r, LANE), F32),
            pltpu.VMEM((r, LANE), F32),
        ],
        compiler_params=pltpu.CompilerParams(
            dimension_semantics=("arbitrary", "arbitrary", "arbitrary"), vmem_limit_bytes=VMEM_LIMIT),
        name="nsa_attention",
    )(qn, qr, cmp_kv, cmp_kv, ksa, vs, kw, vw, small, gbias, imp_t)


def _fox_body(q_ref, k_ref, v_ref, cum_ref, o_ref, s_ref, mrun_ref, acc_ref, *, tq, tk, group):
    i = pl.program_id(2)
    q0 = pl.multiple_of(i * tq, tq)
    qp = q_ref[...]
    low = _lanes((tq, LANE)) < HEAD_DIM
    n_full = lax.div(q0, tk)
    assert tq == tk
    qs, cqs, cum_rows = [], [], []
    for e in range(2):
        qs.append(jnp.where(low if e == 0 else jnp.logical_not(low), qp, jnp.zeros_like(qp)))
        cum_rows.append(2 * pl.program_id(1) + e)
        cq = jnp.broadcast_to(cum_ref[0, pl.ds(cum_rows[e], 1), pl.ds(q0, tq)], (LANE, tq)).T
        cqs.append(jnp.concatenate([cq] * (tk // LANE), axis=1))

    def score_span(kb, n_tiles, masked):
        k0 = pl.multiple_of(kb * tk, tk)
        k = k_ref[pl.ds(k0, n_tiles * tk), :]
        for e in range(2):
            s = _dot_nt(qs[e], k)
            for j in range(n_tiles):
                ck = cum_ref[0, pl.ds(cum_rows[e], 1), pl.ds(k0 + j * tk, tk)]
                sj = s[:, j * tk:(j + 1) * tk] + cqs[e] - ck
                if masked:
                    sj = jnp.where(_lanes((tq, tk)) <= _rows((tq, tk)), sj, NEG_INF)
                s_ref[e, kb + j] = sj
                mrun_ref[e] = jnp.maximum(mrun_ref[e], _lane_max(sj))

    def value_span(kb, n_tiles):
        k0 = pl.multiple_of(kb * tk, tk)
        v = v_ref[pl.ds(k0, n_tiles * tk), :]
        for e in range(2):
            mb = mrun_ref[e]
            p = jnp.concatenate([_exp_tiles(s_ref[e, kb + j], mb) for j in range(n_tiles)], axis=1)
            acc_ref[e] += _dot(p, _values_with_ones(v, e == 1))

    mrun_ref[...] = jnp.full((2, tq, LANE), NEG_INF, F32)
    score_span(n_full, 1, True)
    _spans(n_full, group, lambda kb, n: score_span(kb, n, False))
    for e in range(2):
        mrun_ref[e] = _row_max_lanes(mrun_ref[e])
    acc_ref[...] = jnp.zeros((2, tq, LANE), F32)
    _spans(n_full + 1, group, value_span)
    o_ref[...] = jnp.where(low, _normalize(acc_ref[0]), _normalize(acc_ref[1])).astype(MXU_DTYPE)


def _fox(fq, fk, fv, cum, *, batch, seq, tq, tk, group):
    n_q = seq // tq
    n_pairs = N_FOX_HEADS // 2
    qmap = lambda b, p, i: (b * n_q + i, p)
    kvmap = lambda b, p, i: (b, p)
    return pl.pallas_call(
        functools.partial(_fox_body, tq=tq, tk=tk, group=group),
        grid=(batch, n_pairs, n_q),
        in_specs=[
            pl.BlockSpec((tq, LANE), qmap),
            pl.BlockSpec((seq, LANE), kvmap),
            pl.BlockSpec((seq, LANE), kvmap),
            pl.BlockSpec((1, N_FOX_HEADS, seq), lambda b, p, i: (b, 0, 0)),
        ],
        out_specs=pl.BlockSpec((tq, LANE), qmap),
        out_shape=jax.ShapeDtypeStruct((batch * seq, N_FOX_HEADS * HEAD_DIM), MXU_DTYPE),
        scratch_shapes=[
            pltpu.VMEM((2, seq // tk, tq, tk), F32),
            pltpu.VMEM((2, tq, LANE), F32),
            pltpu.VMEM((2, tq, LANE), F32),
        ],
        compiler_params=pltpu.CompilerParams(
            dimension_semantics=("arbitrary", "arbitrary", "arbitrary"), vmem_limit_bytes=VMEM_LIMIT),
        name="fox_attention",
    )(fq, fk, fv, cum)


def _post_body(on_ref, of_ref, x_ref, wo_ref, gpm_ref, gpl_ref, wu_ref, wd_ref, gpo_ref, o_ref, *, ff_chunk):
    n_nsa = on_ref.shape[1]
    mix = _dot(on_ref[...], wo_ref[0:n_nsa, :]) + _dot(of_ref[...], wo_ref[n_nsa:, :])
    x1 = x_ref[...] + _rms(mix, gpm_ref[...])
    h = _rms(x1, gpl_ref[...]).astype(MXU_DTYPE)
    d_ff = wu_ref.shape[1]
    acc = jnp.zeros(x1.shape, F32)
    for c in range(d_ff // ff_chunk):
        sl = slice(c * ff_chunk, (c + 1) * ff_chunk)
        a = jnp.square(jnp.maximum(_dot(h, wu_ref[:, sl]), 0.0))
        acc = acc + _dot(a.astype(MXU_DTYPE), wd_ref[sl, :])
    o_ref[...] = x1 + _rms(acc, gpo_ref[...])


def _post(o_nsa, o_fox, xf, wo, g_post_mix, g_pre_mlp, wu, wd, g_post_mlp, *, layer, tm, ff_chunk):
    m, d = xf.shape
    d_ff = wu.shape[2]
    row = lambda i: (i, 0)
    fixed = lambda i: (0, 0)
    resident = lambda rows, cols: pl.BlockSpec(
        (None, rows, cols), lambda i: (layer, 0, 0), pipeline_mode=pl.Buffered(1))
    return pl.pallas_call(
        functools.partial(_post_body, ff_chunk=ff_chunk),
        grid=(m // tm,),
        in_specs=[
            pl.BlockSpec((tm, o_nsa.shape[1]), row),
            pl.BlockSpec((tm, o_fox.shape[1]), row),
            pl.BlockSpec((tm, d), row),
            resident(wo.shape[1], d),
            pl.BlockSpec((1, d), fixed),
            pl.BlockSpec((1, d), fixed),
            resident(d, d_ff),
            resident(d_ff, d),
            pl.BlockSpec((1, d), fixed),
        ],
        out_specs=pl.BlockSpec((tm, d), row),
        out_shape=jax.ShapeDtypeStruct((m, d), F32),
        compiler_params=pltpu.CompilerParams(
            dimension_semantics=("arbitrary",), vmem_limit_bytes=VMEM_LIMIT),
        name="outproj_mlp",
    )(o_nsa, o_fox, xf, wo, g_post_mix, g_pre_mlp, wu, wd, g_post_mlp)


def _rope_tables(seq):
    half = ROPE_DIMS // 2
    inv_freq = ROPE_THETA ** (-jnp.arange(half, dtype=F32) * 2.0 / ROPE_DIMS)
    ang = jnp.arange(seq).astype(F32)[:, None] * inv_freq[None, :]
    cos, sin = jnp.cos(ang), jnp.sin(ang)
    pad = jnp.zeros((seq, LANE - ROPE_DIMS), F32)
    zero = jnp.zeros((seq, half), F32)
    c = jnp.concatenate([cos, cos, pad + 1.0], axis=-1)
    a = jnp.concatenate([-sin, zero, pad], axis=-1)
    b = jnp.concatenate([zero, sin, pad], axis=-1)
    return c, a, b


def _importance_map_t(seq, n_chunk):
    n_cmp = (seq - CMP_LEN) // CMP_STRIDE + 1
    n_sel = seq // SEL_BLOCK
    cs = np.arange(n_cmp)[None, :] * CMP_STRIDE
    ss = np.arange(n_sel)[:, None] * SEL_BLOCK
    overlap = np.clip(np.minimum(cs + CMP_LEN, ss + SEL_BLOCK) - np.maximum(cs, ss), 0, None) / CMP_LEN
    out = np.zeros((n_sel, n_chunk), np.float32)
    out[:, :n_cmp] = overlap
    return jnp.asarray(out, dtype=MXU_DTYPE)


def kernel(x, w_in, b_nsa_gate, b_forget, cmp_pos_k, cmp_w1_k, cmp_b1_k, cmp_w2_k, cmp_b2_k, cmp_pos_v, cmp_w1_v, cmp_b1_v, cmp_w2_v, cmp_b2_v, w_out, w_up, w_down, g_pre_mix, g_post_mix, g_pre_mlp, g_post_mlp):
    batch, seq, d_model = x.shape
    depth = w_in.shape[0]
    m = batch * seq
    n_chunk = seq // CMP_STRIDE
    n_sel = seq // SEL_BLOCK
    tm = min(512, seq)
    nsa_tq, nsa_tk = 2 * LANE, min(256, seq)
    fox_t = min(512, seq)
    assert seq % tm == 0 and seq % fox_t == 0 and seq % nsa_tk == 0
    assert n_sel <= HEAD_DIM and n_sel % 8 == 0 and seq >= WINDOW + nsa_tq
    assert x.dtype == F32

    rope_c, rope_a, rope_b = _rope_tables(seq)
    imp_t = _importance_map_t(seq, n_chunk)
    cs_chunk = min(256, seq)
    tril = jnp.asarray(np.tril(np.ones((cs_chunk, cs_chunk), np.float32)), dtype=MXU_DTYPE)

    wp = _permute_columns(w_in.astype(MXU_DTYPE), _projection_columns())

    gate_bias = jnp.zeros((depth, 1, N_NSA_KV * LANE), F32)
    fbias = jnp.zeros((depth, 1, LANE), F32)
    n_gate = NSA_GROUP * N_BRANCH
    for g in range(N_NSA_KV):
        gate_bias = gate_bias.at[:, 0, LANE * g:LANE * g + n_gate].set(b_nsa_gate[:, n_gate * g:n_gate * (g + 1)])
    fbias = fbias.at[:, 0, _FF_LANE:_FF_LANE + N_FOX_HEADS].set(b_forget)

    half = CMP_STRIDE * HEAD_DIM
    cpos = jnp.stack([cmp_pos_k.reshape(depth, 2, half), cmp_pos_v.reshape(depth, 2, half)], axis=1)
    cw1 = jnp.stack([cmp_w1_k, cmp_w1_v], axis=1).astype(MXU_DTYPE)
    cb1 = jnp.stack([cmp_b1_k, cmp_b1_v], axis=1)[:, :, None, :]
    zw = jnp.zeros_like(cmp_w2_k)
    cw2 = jnp.stack([jnp.concatenate([cmp_w2_k, zw], axis=-1),
                     jnp.concatenate([cmp_w2_v, cmp_w2_v], axis=-1)], axis=1).astype(MXU_DTYPE)
    zb = jnp.zeros_like(cmp_b2_k)
    cb2 = jnp.stack([jnp.concatenate([cmp_b2_k, zb], axis=-1),
                     jnp.concatenate([cmp_b2_v, cmp_b2_v], axis=-1)], axis=1)[:, :, None, :]

    wo = w_out.astype(MXU_DTYPE)
    wu = w_up.astype(MXU_DTYPE)
    wd = w_down.astype(MXU_DTYPE)

    xf = x.reshape(m, d_model)
    for l in range(depth):
        qn, qr, kvc, ksa, vs, kw, vw, fq, fk, fv, small = _inproj(
            xf, g_pre_mix[l][None, :], wp, rope_c, rope_a, rope_b, layer=l, seq=seq, tm=tm)

        cum = _forget_cumsum(small, fbias[l], tril, batch=batch, seq=seq)

        cmp_kv = _compress(kvc, cpos[l], cw1[l], cb1[l], cw2[l], cb2[l],
                           batch=batch, n_chunk=n_chunk)
        o_nsa = _nsa(qn, qr, cmp_kv, ksa, vs, kw, vw, small, gate_bias[l], imp_t,
                     batch=batch, seq=seq, tq=nsa_tq, tk=nsa_tk, group=(8, 4, 2))
        o_fox = _fox(fq, fk, fv, cum, batch=batch, seq=seq, tq=fox_t, tk=fox_t, group=(4, 2))
        xf = _post(o_nsa, o_fox, xf, wo, g_post_mix[l][None, :], g_pre_mlp[l][None, :],
                   wu, wd, g_post_mlp[l][None, :], layer=l, tm=tm, ff_chunk=1024)
    return xf.reshape(batch, seq, d_model)
```

```python
import functools

import numpy as np
import jax
import jax.numpy as jnp
from jax import lax
from jax.experimental import pallas as pl
from jax.experimental.pallas import tpu as pltpu

HEAD_DIM = 64
N_NSA_HEADS = 8
N_NSA_KV = 2
NSA_GROUP = N_NSA_HEADS // N_NSA_KV
N_FOX_HEADS = 8
N_BRANCH = 3
CMP_LEN = 32
CMP_STRIDE = 16
CMP_SHIFT = 4
CMP_HIDDEN = 256
SEL_BLOCK = 64
SEL_SHIFT = 6
SEL_TOPK = 16
WINDOW = 512
ROPE_THETA = 500000.0
ROPE_DIMS = HEAD_DIM // 4
NORM_EPS = 1e-6
NEG_INF = -1e30
FORCED_SCORE = 1e4
SCALE = HEAD_DIM ** -0.5
LOG2E = 1.4426950408889634
Q_SCALE = SCALE * LOG2E

LANE = 128
SUBLANE = 8
MXU_DTYPE = jnp.bfloat16
F32 = jnp.float32
V7X_VMEM_BYTES = 64 * 1024 * 1024
VMEM_LIMIT = V7X_VMEM_BYTES * 7 // 8

_SRC_NQ, _SRC_KC, _SRC_VC, _SRC_KSEL, _SRC_VSEL, _SRC_KWIN, _SRC_VWIN = 0, 512, 640, 768, 896, 1024, 1152
_SRC_GATE, _SRC_FQ, _SRC_FK, _SRC_FV, _SRC_FF = 1280, 1304, 1816, 2328, 2840
_Q0, _KVC0, _KS0, _VS0, _KW0, _VW0, _FQ0, _FK0, _FV0, _SM0, _NP = (
    0, 1024, 1536, 1792, 2048, 2304, 2560, 3072, 3584, 4096, 4352)
_FF_LANE = 16


def _projection_columns():
    src = np.full((_NP,), -1, np.int32)
    d = np.arange(HEAD_DIM)
    for h in range(N_NSA_HEADS):
        src[_Q0 + LANE * h + d] = _SRC_NQ + HEAD_DIM * h + d
    for i in range(4):
        src[_KVC0 + LANE * i + d] = _SRC_KC + HEAD_DIM * i + d
    for g in range(N_NSA_KV):
        src[_KS0 + LANE * g + d] = _SRC_KSEL + HEAD_DIM * g + d
        src[_KW0 + LANE * g + d] = _SRC_KWIN + HEAD_DIM * g + d
        for half in range(2):
            src[_VS0 + LANE * g + HEAD_DIM * half + d] = _SRC_VSEL + HEAD_DIM * g + d
            src[_VW0 + LANE * g + HEAD_DIM * half + d] = _SRC_VWIN + HEAD_DIM * g + d
        gates = np.arange(NSA_GROUP * N_BRANCH)
        src[_SM0 + LANE * g + gates] = _SRC_GATE + NSA_GROUP * N_BRANCH * g + gates
        src[_SM0 + LANE * g + _FF_LANE + np.arange(N_FOX_HEADS)] = _SRC_FF + np.arange(N_FOX_HEADS)
    fox = np.arange(N_FOX_HEADS * HEAD_DIM)
    src[_FQ0 + fox] = _SRC_FQ + fox
    src[_FK0 + fox] = _SRC_FK + fox
    src[_FV0 + fox] = _SRC_FV + fox
    return src


def _permute_columns(w, src):
    pieces, i = [], 0
    while i < len(src):
        j = i + 1
        if src[i] < 0:
            while j < len(src) and src[j] < 0:
                j += 1
            pieces.append(jnp.zeros(w.shape[:-1] + (j - i,), w.dtype))
        else:
            while j < len(src) and src[j] == src[j - 1] + 1:
                j += 1
            pieces.append(w[..., int(src[i]):int(src[i]) + (j - i)])
        i = j
    return jnp.concatenate(pieces, axis=-1)


def _rms(x, g):
    ms = jnp.mean(x * x, axis=-1, keepdims=True)
    return x * lax.rsqrt(ms + NORM_EPS) * g


def _dot(a, b):
    return jnp.dot(a, b, preferred_element_type=F32)


def _dot_nt(a, b):
    return lax.dot_general(a, b, (((1,), (1,)), ((), ())), preferred_element_type=F32)


def _split3(x):
    hi = x.astype(MXU_DTYPE)
    r1 = x - hi.astype(F32)
    mid = r1.astype(MXU_DTYPE)
    lo = (r1 - mid.astype(F32)).astype(MXU_DTYPE)
    return hi, mid, lo


def _rows(shape):
    return lax.broadcasted_iota(jnp.int32, shape, 0)


def _lanes(shape):
    return lax.broadcasted_iota(jnp.int32, shape, 1)


def _inproj_body(x_ref, g_ref, w_ref, rc_ref, ra_ref, rb_ref,
                 qn_ref, qr_ref, kvc_ref, ksa_ref, vs_ref, kw_ref, vw_ref,
                 fq_ref, fk_ref, fv_ref, small_ref, *, tm, n_s_tiles):
    h = _rms(x_ref[...], g_ref[...]).astype(MXU_DTYPE)
    rc, ra, rb = rc_ref[...], ra_ref[...], rb_ref[...]

    def mm(lo, n):
        return _dot(h, w_ref[:, lo:lo + n])

    def rope(v):
        return v * rc + pltpu.roll(v, LANE - ROPE_DIMS // 2, 1) * ra + pltpu.roll(v, ROPE_DIMS // 2, 1) * rb

    q = mm(_Q0, N_NSA_HEADS * LANE) * Q_SCALE
    qn_ref[...] = q.astype(MXU_DTYPE)
    for hh in range(N_NSA_HEADS):
        sl = slice(LANE * hh, LANE * (hh + 1))
        qr_ref[:, sl] = rope(q[:, sl]).astype(MXU_DTYPE)

    kvc = mm(_KVC0, 4 * LANE)
    for i in range(4):
        kvc_ref[i] = kvc[:, LANE * i:LANE * i + HEAD_DIM]

    pos = (pl.program_id(0) % n_s_tiles) * tm + _rows((tm, LANE))
    onehot = ((_lanes((tm, LANE)) - HEAD_DIM) == (pos >> SEL_SHIFT)).astype(F32)
    ks = mm(_KS0, N_NSA_KV * LANE)
    kw = mm(_KW0, N_NSA_KV * LANE)
    for g in range(N_NSA_KV):
        sl = slice(LANE * g, LANE * (g + 1))
        ksa_ref[:, sl] = (rope(ks[:, sl]) + onehot).astype(MXU_DTYPE)
        kw_ref[:, sl] = rope(kw[:, sl]).astype(MXU_DTYPE)
    vs_ref[...] = mm(_VS0, N_NSA_KV * LANE).astype(MXU_DTYPE)
    vw_ref[...] = mm(_VW0, N_NSA_KV * LANE).astype(MXU_DTYPE)

    nf = N_FOX_HEADS * HEAD_DIM
    fq_ref[...] = (mm(_FQ0, nf) * Q_SCALE).astype(MXU_DTYPE)
    fk_ref[...] = mm(_FK0, nf).astype(MXU_DTYPE)
    fv_ref[...] = mm(_FV0, nf).astype(MXU_DTYPE)
    small_ref[...] = mm(_SM0, N_NSA_KV * LANE)


def _inproj(xf, g, wp, rope_c, rope_a, rope_b, *, layer, seq, tm):
    m, d = xf.shape
    n_s_tiles = seq // tm
    nf = N_FOX_HEADS * HEAD_DIM
    row = lambda i: (i, 0)
    fixed = lambda i: (0, 0)
    tab = lambda i: (i % n_s_tiles, 0)
    out_shape = (
        jax.ShapeDtypeStruct((m, N_NSA_HEADS * LANE), MXU_DTYPE),
        jax.ShapeDtypeStruct((m, N_NSA_HEADS * LANE), MXU_DTYPE),
        jax.ShapeDtypeStruct((4, m, HEAD_DIM), F32),
        jax.ShapeDtypeStruct((m, N_NSA_KV * LANE), MXU_DTYPE),
        jax.ShapeDtypeStruct((m, N_NSA_KV * LANE), MXU_DTYPE),
        jax.ShapeDtypeStruct((m, N_NSA_KV * LANE), MXU_DTYPE),
        jax.ShapeDtypeStruct((m, N_NSA_KV * LANE), MXU_DTYPE),
        jax.ShapeDtypeStruct((m, nf), MXU_DTYPE),
        jax.ShapeDtypeStruct((m, nf), MXU_DTYPE),
        jax.ShapeDtypeStruct((m, nf), MXU_DTYPE),
        jax.ShapeDtypeStruct((m, N_NSA_KV * LANE), F32),
    )
    out_specs = (
        pl.BlockSpec((tm, N_NSA_HEADS * LANE), row),
        pl.BlockSpec((tm, N_NSA_HEADS * LANE), row),
        pl.BlockSpec((4, tm, HEAD_DIM), lambda i: (0, i, 0)),
        pl.BlockSpec((tm, N_NSA_KV * LANE), row),
        pl.BlockSpec((tm, N_NSA_KV * LANE), row),
        pl.BlockSpec((tm, N_NSA_KV * LANE), row),
        pl.BlockSpec((tm, N_NSA_KV * LANE), row),
        pl.BlockSpec((tm, nf), row),
        pl.BlockSpec((tm, nf), row),
        pl.BlockSpec((tm, nf), row),
        pl.BlockSpec((tm, N_NSA_KV * LANE), row),
    )
    return pl.pallas_call(
        functools.partial(_inproj_body, tm=tm, n_s_tiles=n_s_tiles),
        grid=(m // tm,),
        in_specs=[
            pl.BlockSpec((tm, d), row),
            pl.BlockSpec((1, d), fixed),
            pl.BlockSpec((None, d, _NP), lambda i: (layer, 0, 0)),
            pl.BlockSpec((tm, LANE), tab),
            pl.BlockSpec((tm, LANE), tab),
            pl.BlockSpec((tm, LANE), tab),
        ],
        out_specs=out_specs,
        out_shape=out_shape,
        compiler_params=pltpu.CompilerParams(
            dimension_semantics=("arbitrary",), vmem_limit_bytes=VMEM_LIMIT),
        name="inproj",
    )(xf, g, wp, rope_c, rope_a, rope_b)


def _cumsum_body(sm_ref, bias_ref, tril_ref, out_ref, *, seq, chunk):
    carry = jnp.zeros((1, LANE), F32)
    tril = tril_ref[...]
    for c in range(seq // chunk):
        sl = slice(c * chunk, (c + 1) * chunk)
        z = sm_ref[sl, :] + bias_ref[...]
        logf = jnp.minimum(z, 0.0) - jnp.log1p(jnp.exp(-jnp.abs(z)))
        hi, mid, lo = _split3(logf)
        cs = (_dot(tril, hi) + _dot(tril, mid)) + _dot(tril, lo) + carry
        out_ref[0, :, sl] = (cs * LOG2E).T[_FF_LANE:_FF_LANE + N_FOX_HEADS, :]
        carry = cs[chunk - 1:chunk, :]


def _forget_cumsum(small, fbias, tril, *, batch, seq):
    chunk = tril.shape[0]
    return pl.pallas_call(
        functools.partial(_cumsum_body, seq=seq, chunk=chunk),
        grid=(batch,),
        in_specs=[
            pl.BlockSpec((seq, LANE), lambda b: (b, 0)),
            pl.BlockSpec((1, LANE), lambda b: (0, 0)),
            pl.BlockSpec((chunk, chunk), lambda b: (0, 0)),
        ],
        out_specs=pl.BlockSpec((1, N_FOX_HEADS, seq), lambda b: (b, 0, 0)),
        out_shape=jax.ShapeDtypeStruct((batch, N_FOX_HEADS, seq), F32),
        compiler_params=pltpu.CompilerParams(
            dimension_semantics=("arbitrary",), vmem_limit_bytes=VMEM_LIMIT),
        name="forget_cumsum",
    )(small, fbias, tril)


def _compress_body(x_ref, pos_ref, w1_ref, b1_ref, w2_ref, b2_ref, o_ref, *, n_chunk):
    half = CMP_STRIDE * HEAD_DIM
    x = jnp.concatenate(
        [x_ref[0, pl.ds(t, n_chunk, stride=CMP_STRIDE), :] for t in range(CMP_STRIDE)], axis=1)
    first = _dot((x + pos_ref[0, 0:1, :]).astype(MXU_DTYPE), w1_ref[0, 0:half, :])
    second = _dot((x + pos_ref[0, 1:2, :]).astype(MXU_DTYPE), w1_ref[0, half:2 * half, :])
    hid = first + pltpu.roll(second, n_chunk - 1, 0) + b1_ref[0]
    act = jax.nn.gelu(hid)
    o_ref[0, 0] = (_dot(act.astype(MXU_DTYPE), w2_ref[0]) + b2_ref[0]).astype(MXU_DTYPE)


def _compress(kvc, pos, w1, b1, w2p, b2p, *, batch, n_chunk):
    half = CMP_STRIDE * HEAD_DIM
    kv = lambda i, b: (i // N_NSA_KV, 0, 0)
    return pl.pallas_call(
        functools.partial(_compress_body, n_chunk=n_chunk),
        grid=(2 * N_NSA_KV, batch),
        in_specs=[
            pl.BlockSpec((1, n_chunk * CMP_STRIDE, HEAD_DIM), lambda i, b: (i, b, 0)),
            pl.BlockSpec((1, 2, half), kv),
            pl.BlockSpec((1, 2 * half, CMP_HIDDEN), kv),
            pl.BlockSpec((1, 1, CMP_HIDDEN), kv),
            pl.BlockSpec((1, CMP_HIDDEN, LANE), kv),
            pl.BlockSpec((1, 1, LANE), kv),
        ],
        out_specs=pl.BlockSpec((1, 1, n_chunk, LANE), lambda i, b: (i, b, 0, 0)),
        out_shape=jax.ShapeDtypeStruct((2 * N_NSA_KV, batch, n_chunk, LANE), MXU_DTYPE),
        compiler_params=pltpu.CompilerParams(
            dimension_semantics=("arbitrary", "arbitrary"), vmem_limit_bytes=VMEM_LIMIT),
        name="compress",
    )(kvc, pos, w1, b1, w2p, b2p)


def _lane_max(s):
    out = s[:, 0:LANE]
    for c in range(1, s.shape[1] // LANE):
        out = jnp.maximum(out, s[:, LANE * c:LANE * (c + 1)])
    return out


def _row_max_lanes(mrun):
    return jnp.broadcast_to(jnp.max(mrun, axis=-1, keepdims=True), mrun.shape)


def _exp_tiles(s, mb):
    n = s.shape[1] // LANE
    return jnp.concatenate(
        [jnp.exp2(s[:, LANE * c:LANE * (c + 1)] - mb).astype(MXU_DTYPE) for c in range(n)], axis=1)


def _values_with_ones(v, ones_low):
    low = _lanes(v.shape) < HEAD_DIM
    ones = jnp.ones_like(v)
    return jnp.where(low, ones, v) if ones_low else jnp.where(low, v, ones)


def _normalize(acc):
    return acc / pltpu.roll(acc, HEAD_DIM, 1)


def _spans(n_tiles, groups, fn):
    start = 0
    for size in tuple(groups) + (1,):
        n_groups = lax.div(n_tiles - start, size)

        def step(t, carry, size=size, start=start):
            fn(start + t * size, size)
            return carry

        lax.fori_loop(0, n_groups, step, 0)
        start = start + n_groups * size


def _nsa_body(qn_ref, qr_ref, kc_ref, vc_ref, ksa_ref, vs_ref, kw_ref, vw_ref, sm_ref, gb_ref, impt_ref,
              o_ref, s_ref, mrun_ref, acc_ref, *, tq, tk, group, n_sel, top_k):
    q0 = pl.program_id(2) * tq
    r = NSA_GROUP * tq
    n_cmp = kc_ref.shape[2]

    def window_scores(hq):
        n_wc = WINDOW // LANE + 1
        upper = _lanes((NSA_GROUP * LANE, LANE)) > (_rows((NSA_GROUP * LANE, LANE)) & (LANE - 1))
        qs = q0 + hq * LANE
        qh = jnp.concatenate(
            [qr_ref[hq * LANE:(hq + 1) * LANE, LANE * h:LANE * (h + 1)] for h in range(NSA_GROUP)], axis=0)
        starts = [pl.multiple_of(jnp.maximum(qs - WINDOW + LANE * c, 0), LANE) for c in range(n_wc)]
        flag_lane = _lanes((LANE, LANE)) == HEAD_DIM
        k_chunks = []
        for c, st in enumerate(starts):
            kc = kw_ref[pl.ds(st, LANE), :]
            if c < n_wc - 1:
                before_start = jnp.logical_and(flag_lane, qs - WINDOW + LANE * c < 0)
                kc = jnp.where(before_start, jnp.ones_like(kc), kc)
            k_chunks.append(kc)
        kk = jnp.concatenate(k_chunks, axis=0)
        vv = jnp.concatenate([vw_ref[pl.ds(st, LANE), :] for st in starts], axis=0)
        q_flag = _lanes(qh.shape) == HEAD_DIM
        sw = _dot_nt(jnp.where(q_flag, jnp.full_like(qh, NEG_INF), qh), kk)
        chunks = [sw[:, LANE * c:LANE * (c + 1)] for c in range(n_wc)]
        chunks[0] = jnp.where(upper, chunks[0], NEG_INF)
        chunks[-1] = jnp.where(upper, NEG_INF, chunks[-1])
        return chunks, vv

    def window_probs(chunks):
        mb = _row_max_lanes(functools.reduce(jnp.maximum, chunks))
        return jnp.concatenate([jnp.exp2(ch - mb).astype(MXU_DTYPE) for ch in chunks], axis=1)

    def window_out(pw, vv):
        return _normalize(_dot(pw, _values_with_ones(vv, False)))

    q4n = jnp.concatenate([qn_ref[:, LANE * h:LANE * (h + 1)] for h in range(NSA_GROUP)], axis=0)
    sc = _dot_nt(q4n, kc_ref[0, 0])
    win_scores = [window_scores(0), window_scores(1)]
    last_vis = (q0 + _rows((tq, LANE)) - (CMP_LEN - 1)) >> CMP_SHIFT
    last_vis = jnp.concatenate([last_vis] * NSA_GROUP, axis=0)
    sc_chunks = [
        jnp.where(_lanes((r, LANE)) + LANE * c <= last_vis, sc[:, LANE * c:LANE * (c + 1)], NEG_INF)
        for c in range(n_cmp // LANE)]
    mb = jnp.maximum(_row_max_lanes(functools.reduce(jnp.maximum, sc_chunks)), 0.1 * NEG_INF)
    e_chunks = [jnp.exp2(ch - mb) for ch in sc_chunks]
    den = jnp.sum(functools.reduce(jnp.add, e_chunks), axis=-1, keepdims=True)
    inv = 1.0 / jnp.maximum(jnp.broadcast_to(den, (r, LANE)), 1e-30)
    pc = jnp.concatenate([ech * inv for ech in e_chunks], axis=1)
    o_cmp = _dot(pc.astype(MXU_DTYPE), vc_ref[0, 0])

    pcs = (pc[0:tq] + pc[tq:2 * tq]) + (pc[2 * tq:3 * tq] + pc[3 * tq:4 * tq])
    impt_w = impt_ref[...]
    hi, mid, lo = _split3(pcs)
    imp_t = (_dot_nt(impt_w, hi) + _dot_nt(impt_w, mid)) + _dot_nt(impt_w, lo)
    jrow = _rows((n_sel, tq))
    qpos = q0 + _lanes((n_sel, tq))
    cur = qpos >> SEL_SHIFT
    valid = jrow * SEL_BLOCK <= qpos
    forced = (jrow == 0) | (jrow == cur) | (jrow == cur - 1)
    score = jnp.where(forced, FORCED_SCORE, jnp.where(valid, imp_t, -1.0))

    sub = SUBLANE
    chunks = [score[sub * c:sub * (c + 1), :] for c in range(n_sel // sub)]
    ranks = [jnp.zeros((sub, tq), F32) for _ in chunks]
    later = _rows((sub, tq))

    win_probs = [window_probs(win_scores[0][0])]
    win_blocks = []
    for jp in range(n_sel):
        if jp == n_sel // 2:
            win_blocks.append(window_out(win_probs[0], win_scores[0][1]))
            win_probs.append(window_probs(win_scores[1][0]))
        cj, rj = divmod(jp, sub)
        rowv = chunks[cj][rj:rj + 1, :]
        for c, sc_c in enumerate(chunks):
            if c > cj:
                beats = rowv >= sc_c
            elif c < cj:
                beats = rowv > sc_c
            else:
                beats = (rowv > sc_c) | ((rowv == sc_c) & (later > rj))
            ranks[c] = ranks[c] + jnp.where(beats, 1.0, 0.0)
    sel_t = jnp.where(jnp.concatenate(ranks, axis=0) < float(top_k), 1.0, 0.0)
    sel_pad = jnp.concatenate([sel_t, jnp.zeros((LANE - n_sel, tq), F32)], axis=0)
    sel_q = pltpu.roll(sel_pad.T, HEAD_DIM, 1)
    sel_bias = ((sel_q - 1.0) * (-NEG_INF)).astype(MXU_DTYPE)

    high = _lanes((tq, LANE)) >= HEAD_DIM
    q4r = jnp.concatenate(
        [jnp.where(high, sel_bias, qr_ref[:, LANE * h:LANE * (h + 1)]) for h in range(NSA_GROUP)], axis=0)

    n_full = lax.div(q0, tk)
    assert tq == tk

    def score_span(kb, n_tiles, masked):
        k0 = pl.multiple_of(kb * tk, tk)
        s = _dot_nt(q4r, ksa_ref[pl.ds(k0, n_tiles * tk), :])
        if masked:
            s = jnp.where(_lanes((r, tk)) <= (_rows((r, tk)) & (tq - 1)), s, NEG_INF)
        for j in range(n_tiles):
            s_ref[kb + j] = s[:, j * tk:(j + 1) * tk]
        mrun_ref[...] = jnp.maximum(mrun_ref[...], _lane_max(s))

    def value_span(kb, n_tiles):
        k0 = pl.multiple_of(kb * tk, tk)
        mb = mrun_ref[...]
        p = jnp.concatenate([_exp_tiles(s_ref[kb + j], mb) for j in range(n_tiles)], axis=1)
        acc_ref[...] += _dot(p, _values_with_ones(vs_ref[pl.ds(k0, n_tiles * tk), :], False))

    win_blocks.append(window_out(win_probs[1], win_scores[1][1]))
    o_win = jnp.concatenate([blk[h * LANE:(h + 1) * LANE] for h in range(NSA_GROUP) for blk in win_blocks], axis=0)

    gate = jax.nn.sigmoid(sm_ref[...] + gb_ref[...])
    partial, gate_sel = [], []
    for h in range(NSA_GROUP):
        rs = slice(h * tq, (h + 1) * tq)
        c = N_BRANCH * h
        partial.append(gate[:, c:c + 1] * o_cmp[rs] + gate[:, c + 2:c + 3] * o_win[rs])
        gate_sel.append(jnp.broadcast_to(gate[:, c + 1:c + 2], (tq, LANE)))

    mrun_ref[...] = jnp.full((r, LANE), NEG_INF, F32)
    score_span(n_full, 1, True)
    _spans(n_full, group, lambda kb, n: score_span(kb, n, False))
    mrun_ref[...] = _row_max_lanes(mrun_ref[...])
    acc_ref[...] = jnp.zeros((r, LANE), F32)
    _spans(n_full + 1, group, value_span)
    o_sel = _normalize(acc_ref[...])

    heads = [partial[h] + gate_sel[h] * o_sel[h * tq:(h + 1) * tq] for h in range(NSA_GROUP)]
    low = _lanes((tq, LANE)) < HEAD_DIM
    for p in range(NSA_GROUP // 2):
        pair = jnp.where(low, heads[2 * p], pltpu.roll(heads[2 * p + 1], HEAD_DIM, 1))
        o_ref[:, LANE * p:LANE * (p + 1)] = pair.astype(MXU_DTYPE)


def _nsa(qn, qr, cmp_kv, ksa, vs, kw, vw, small, gbias, imp_t, *, batch, seq, tq, tk, group):
    assert tq == 2 * LANE
    n_q = seq // tq
    n_cmp = cmp_kv.shape[2]
    n_sel = seq // SEL_BLOCK
    top_k = min(SEL_TOPK, n_sel)
    gw = NSA_GROUP * LANE
    qmap = lambda b, g, i: (b * n_q + i, g)
    kvmap = lambda b, g, i: (b, g)
    r = NSA_GROUP * tq
    return pl.pallas_call(
        functools.partial(_nsa_body, tq=tq, tk=tk, group=group, n_sel=n_sel, top_k=top_k),
        grid=(batch, N_NSA_KV, n_q),
        in_specs=[
            pl.BlockSpec((tq, gw), qmap),
            pl.BlockSpec((tq, gw), qmap),
            pl.BlockSpec((1, 1, n_cmp, LANE), lambda b, g, i: (g, b, 0, 0)),
            pl.BlockSpec((1, 1, n_cmp, LANE), lambda b, g, i: (N_NSA_KV + g, b, 0, 0)),
            pl.BlockSpec((seq, LANE), kvmap),
            pl.BlockSpec((seq, LANE), kvmap),
            pl.BlockSpec((seq, LANE), kvmap),
            pl.BlockSpec((seq, LANE), kvmap),
            pl.BlockSpec((tq, LANE), qmap),
            pl.BlockSpec((1, LANE), lambda b, g, i: (0, g)),
            pl.BlockSpec((n_sel, n_cmp), lambda b, g, i: (0, 0)),
        ],
        out_specs=pl.BlockSpec((tq, NSA_GROUP * HEAD_DIM), qmap),
        out_shape=jax.ShapeDtypeStruct((batch * seq, N_NSA_HEADS * HEAD_DIM), MXU_DTYPE),
        scratch_shapes=[
            pltpu.VMEM((seq // tk, r, tk), F32),
            pltpu.VMEM((r, LANE), F32),
            pltpu.VMEM((r, LANE), F32),
        ],
        compiler_params=pltpu.CompilerParams(
            dimension_semantics=("arbitrary", "arbitrary", "arbitrary"), vmem_limit_bytes=VMEM_LIMIT),
        name="nsa_attention",
    )(qn, qr, cmp_kv, cmp_kv, ksa, vs, kw, vw, small, gbias, imp_t)


def _fox_body(q_ref, k_ref, v_ref, cum_ref, o_ref, s_ref, mrun_ref, acc_ref, *, tq, tk, group):
    i = pl.program_id(2)
    q0 = pl.multiple_of(i * tq, tq)
    qp = q_ref[...]
    low = _lanes((tq, LANE)) < HEAD_DIM
    n_full = lax.div(q0, tk)
    assert tq == tk
    qs, cqs, cum_rows = [], [], []
    for e in range(2):
        qs.append(jnp.where(low if e == 0 else jnp.logical_not(low), qp, jnp.zeros_like(qp)))
        cum_rows.append(2 * pl.program_id(1) + e)
        cq = jnp.broadcast_to(cum_ref[0, pl.ds(cum_rows[e], 1), pl.ds(q0, tq)], (LANE, tq)).T
        cqs.append(jnp.concatenate([cq] * (tk // LANE), axis=1))

    def score_span(kb, n_tiles, masked):
        k0 = pl.multiple_of(kb * tk, tk)
        k = k_ref[pl.ds(k0, n_tiles * tk), :]
        for e in range(2):
            s = _dot_nt(qs[e], k)
            for j in range(n_tiles):
                ck = cum_ref[0, pl.ds(cum_rows[e], 1), pl.ds(k0 + j * tk, tk)]
                sj = s[:, j * tk:(j + 1) * tk] + cqs[e] - ck
                if masked:
                    sj = jnp.where(_lanes((tq, tk)) <= _rows((tq, tk)), sj, NEG_INF)
                s_ref[e, kb + j] = sj
                mrun_ref[e] = jnp.maximum(mrun_ref[e], _lane_max(sj))

    def value_span(kb, n_tiles):
        k0 = pl.multiple_of(kb * tk, tk)
        v = v_ref[pl.ds(k0, n_tiles * tk), :]
        for e in range(2):
            mb = mrun_ref[e]
            p = jnp.concatenate([_exp_tiles(s_ref[e, kb + j], mb) for j in range(n_tiles)], axis=1)
            acc_ref[e] += _dot(p, _values_with_ones(v, e == 1))

    mrun_ref[...] = jnp.full((2, tq, LANE), NEG_INF, F32)
    score_span(n_full, 1, True)
    _spans(n_full, group, lambda kb, n: score_span(kb, n, False))
    for e in range(2):
        mrun_ref[e] = _row_max_lanes(mrun_ref[e])
    acc_ref[...] = jnp.zeros((2, tq, LANE), F32)
    _spans(n_full + 1, group, value_span)
    o_ref[...] = jnp.where(low, _normalize(acc_ref[0]), _normalize(acc_ref[1])).astype(MXU_DTYPE)


def _fox(fq, fk, fv, cum, *, batch, seq, tq, tk, group):
    n_q = seq // tq
    n_pairs = N_FOX_HEADS // 2
    qmap = lambda b, p, i: (b * n_q + i, p)
    kvmap = lambda b, p, i: (b, p)
    return pl.pallas_call(
        functools.partial(_fox_body, tq=tq, tk=tk, group=group),
        grid=(batch, n_pairs, n_q),
        in_specs=[
            pl.BlockSpec((tq, LANE), qmap),
            pl.BlockSpec((seq, LANE), kvmap),
            pl.BlockSpec((seq, LANE), kvmap),
            pl.BlockSpec((1, N_FOX_HEADS, seq), lambda b, p, i: (b, 0, 0)),
        ],
        out_specs=pl.BlockSpec((tq, LANE), qmap),
        out_shape=jax.ShapeDtypeStruct((batch * seq, N_FOX_HEADS * HEAD_DIM), MXU_DTYPE),
        scratch_shapes=[
            pltpu.VMEM((2, seq // tk, tq, tk), F32),
            pltpu.VMEM((2, tq, LANE), F32),
            pltpu.VMEM((2, tq, LANE), F32),
        ],
        compiler_params=pltpu.CompilerParams(
            dimension_semantics=("arbitrary", "arbitrary", "arbitrary"), vmem_limit_bytes=VMEM_LIMIT),
        name="fox_attention",
    )(fq, fk, fv, cum)


def _post_body(on_ref, of_ref, x_ref, wo_ref, gpm_ref, gpl_ref, wu_ref, wd_ref, gpo_ref, o_ref, *, ff_chunk):
    n_nsa = on_ref.shape[1]
    d_ff = wu_ref.shape[1]
    n_ff = d_ff // ff_chunk
    tm = x_ref.shape[0]
    halves = [slice(0, tm // 2), slice(tm // 2, tm)]

    def out_proj(rows):
        return _dot(on_ref[rows, :], wo_ref[0:n_nsa, :]) + _dot(of_ref[rows, :], wo_ref[n_nsa:, :])

    def norms(rows, mix):
        x1 = x_ref[rows, :] + _rms(mix, gpm_ref[...])
        return x1, _rms(x1, gpl_ref[...]).astype(MXU_DTYPE)

    def mlp_chunk(h, acc, c):
        sl = slice(c * ff_chunk, (c + 1) * ff_chunk)
        a = jnp.square(jnp.maximum(_dot(h, wu_ref[:, sl]), 0.0))
        return acc + _dot(a.astype(MXU_DTYPE), wd_ref[sl, :])

    mix_a, mix_b = out_proj(halves[0]), out_proj(halves[1])
    x1_a, h_a = norms(halves[0], mix_a)
    acc_a = mlp_chunk(h_a, jnp.zeros(x1_a.shape, F32), 0)
    x1_b, h_b = norms(halves[1], mix_b)
    for c in range(1, n_ff):
        acc_a = mlp_chunk(h_a, acc_a, c)
    acc_b = mlp_chunk(h_b, jnp.zeros(x1_b.shape, F32), 0)
    o_ref[halves[0], :] = x1_a + _rms(acc_a, gpo_ref[...])
    for c in range(1, n_ff):
        acc_b = mlp_chunk(h_b, acc_b, c)
    o_ref[halves[1], :] = x1_b + _rms(acc_b, gpo_ref[...])


def _post(o_nsa, o_fox, xf, wo, g_post_mix, g_pre_mlp, wu, wd, g_post_mlp, *, layer, tm, ff_chunk):
    m, d = xf.shape
    d_ff = wu.shape[2]
    row = lambda i: (i, 0)
    fixed = lambda i: (0, 0)
    resident = lambda rows, cols: pl.BlockSpec(
        (None, rows, cols), lambda i: (layer, 0, 0), pipeline_mode=pl.Buffered(1))
    return pl.pallas_call(
        functools.partial(_post_body, ff_chunk=ff_chunk),
        grid=(m // tm,),
        in_specs=[
            pl.BlockSpec((tm, o_nsa.shape[1]), row),
            pl.BlockSpec((tm, o_fox.shape[1]), row),
            pl.BlockSpec((tm, d), row),
            resident(wo.shape[1], d),
            pl.BlockSpec((1, d), fixed),
            pl.BlockSpec((1, d), fixed),
            resident(d, d_ff),
            resident(d_ff, d),
            pl.BlockSpec((1, d), fixed),
        ],
        out_specs=pl.BlockSpec((tm, d), row),
        out_shape=jax.ShapeDtypeStruct((m, d), F32),
        compiler_params=pltpu.CompilerParams(
            dimension_semantics=("arbitrary",), vmem_limit_bytes=VMEM_LIMIT),
        name="outproj_mlp",
    )(o_nsa, o_fox, xf, wo, g_post_mix, g_pre_mlp, wu, wd, g_post_mlp)


def _rope_tables(seq):
    half = ROPE_DIMS // 2
    inv_freq = ROPE_THETA ** (-jnp.arange(half, dtype=F32) * 2.0 / ROPE_DIMS)
    ang = jnp.arange(seq).astype(F32)[:, None] * inv_freq[None, :]
    cos, sin = jnp.cos(ang), jnp.sin(ang)
    pad = jnp.zeros((seq, LANE - ROPE_DIMS), F32)
    zero = jnp.zeros((seq, half), F32)
    c = jnp.concatenate([cos, cos, pad + 1.0], axis=-1)
    a = jnp.concatenate([-sin, zero, pad], axis=-1)
    b = jnp.concatenate([zero, sin, pad], axis=-1)
    return c, a, b


def _importance_map_t(seq, n_chunk):
    n_cmp = (seq - CMP_LEN) // CMP_STRIDE + 1
    n_sel = seq // SEL_BLOCK
    cs = np.arange(n_cmp)[None, :] * CMP_STRIDE
    ss = np.arange(n_sel)[:, None] * SEL_BLOCK
    overlap = np.clip(np.minimum(cs + CMP_LEN, ss + SEL_BLOCK) - np.maximum(cs, ss), 0, None) / CMP_LEN
    out = np.zeros((n_sel, n_chunk), np.float32)
    out[:, :n_cmp] = overlap
    return jnp.asarray(out, dtype=MXU_DTYPE)


def kernel(x, w_in, b_nsa_gate, b_forget, cmp_pos_k, cmp_w1_k, cmp_b1_k, cmp_w2_k, cmp_b2_k, cmp_pos_v, cmp_w1_v, cmp_b1_v, cmp_w2_v, cmp_b2_v, w_out, w_up, w_down, g_pre_mix, g_post_mix, g_pre_mlp, g_post_mlp):
    batch, seq, d_model = x.shape
    depth = w_in.shape[0]
    m = batch * seq
    n_chunk = seq // CMP_STRIDE
    n_sel = seq // SEL_BLOCK
    tm = min(512, seq)
    nsa_tq, nsa_tk = 2 * LANE, min(256, seq)
    fox_t = min(512, seq)
    assert seq % tm == 0 and seq % fox_t == 0 and seq % nsa_tk == 0
    assert n_sel <= HEAD_DIM and n_sel % 8 == 0 and seq >= WINDOW + nsa_tq
    assert x.dtype == F32

    rope_c, rope_a, rope_b = _rope_tables(seq)
    imp_t = _importance_map_t(seq, n_chunk)
    cs_chunk = min(256, seq)
    tril = jnp.asarray(np.tril(np.ones((cs_chunk, cs_chunk), np.float32)), dtype=MXU_DTYPE)

    wp = _permute_columns(w_in.astype(MXU_DTYPE), _projection_columns())

    gate_bias = jnp.zeros((depth, 1, N_NSA_KV * LANE), F32)
    fbias = jnp.zeros((depth, 1, LANE), F32)
    n_gate = NSA_GROUP * N_BRANCH
    for g in range(N_NSA_KV):
        gate_bias = gate_bias.at[:, 0, LANE * g:LANE * g + n_gate].set(b_nsa_gate[:, n_gate * g:n_gate * (g + 1)])
    fbias = fbias.at[:, 0, _FF_LANE:_FF_LANE + N_FOX_HEADS].set(b_forget)

    half = CMP_STRIDE * HEAD_DIM
    cpos = jnp.stack([cmp_pos_k.reshape(depth, 2, half), cmp_pos_v.reshape(depth, 2, half)], axis=1)
    cw1 = jnp.stack([cmp_w1_k, cmp_w1_v], axis=1).astype(MXU_DTYPE)
    cb1 = jnp.stack([cmp_b1_k, cmp_b1_v], axis=1)[:, :, None, :]
    zw = jnp.zeros_like(cmp_w2_k)
    cw2 = jnp.stack([jnp.concatenate([cmp_w2_k, zw], axis=-1),
                     jnp.concatenate([cmp_w2_v, cmp_w2_v], axis=-1)], axis=1).astype(MXU_DTYPE)
    zb = jnp.zeros_like(cmp_b2_k)
    cb2 = jnp.stack([jnp.concatenate([cmp_b2_k, zb], axis=-1),
                     jnp.concatenate([cmp_b2_v, cmp_b2_v], axis=-1)], axis=1)[:, :, None, :]

    wo = w_out.astype(MXU_DTYPE)
    wu = w_up.astype(MXU_DTYPE)
    wd = w_down.astype(MXU_DTYPE)

    xf = x.reshape(m, d_model)
    for l in range(depth):
        qn, qr, kvc, ksa, vs, kw, vw, fq, fk, fv, small = _inproj(
            xf, g_pre_mix[l][None, :], wp, rope_c, rope_a, rope_b, layer=l, seq=seq, tm=tm)

        cum = _forget_cumsum(small, fbias[l], tril, batch=batch, seq=seq)

        cmp_kv = _compress(kvc, cpos[l], cw1[l], cb1[l], cw2[l], cb2[l],
                           batch=batch, n_chunk=n_chunk)
        o_nsa = _nsa(qn, qr, cmp_kv, ksa, vs, kw, vw, small, gate_bias[l], imp_t,
                     batch=batch, seq=seq, tq=nsa_tq, tk=nsa_tk, group=(8, 4, 2))
        o_fox = _fox(fq, fk, fv, cum, batch=batch, seq=seq, tq=fox_t, tk=fox_t, group=(4, 2))
        xf = _post(o_nsa, o_fox, xf, wo, g_post_mix[l][None, :], g_pre_mlp[l][None, :],
                   wu, wd, g_post_mlp[l][None, :], layer=l, tm=tm, ff_chunk=1024)
    return xf.reshape(batch, seq, d_model)
```

```python
import functools

import numpy as np
import jax
import jax.numpy as jnp
from jax import lax
from jax.experimental import pallas as pl
from jax.experimental.pallas import tpu as pltpu

HEAD_DIM = 64
N_NSA_HEADS = 8
N_NSA_KV = 2
NSA_GROUP = N_NSA_HEADS // N_NSA_KV
N_FOX_HEADS = 8
N_BRANCH = 3
CMP_LEN = 32
CMP_STRIDE = 16
CMP_SHIFT = 4
CMP_HIDDEN = 256
SEL_BLOCK = 64
SEL_SHIFT = 6
SEL_TOPK = 16
WINDOW = 512
ROPE_THETA = 500000.0
ROPE_DIMS = HEAD_DIM // 4
NORM_EPS = 1e-6
NEG_INF = -1e30
FORCED_SCORE = 1e4
SCALE = HEAD_DIM ** -0.5
LOG2E = 1.4426950408889634
Q_SCALE = SCALE * LOG2E

LANE = 128
SUBLANE = 8
MXU_DTYPE = jnp.bfloat16
F32 = jnp.float32
V7X_VMEM_BYTES = 64 * 1024 * 1024
VMEM_LIMIT = V7X_VMEM_BYTES * 7 // 8

_SRC_NQ, _SRC_KC, _SRC_VC, _SRC_KSEL, _SRC_VSEL, _SRC_KWIN, _SRC_VWIN = 0, 512, 640, 768, 896, 1024, 1152
_SRC_GATE, _SRC_FQ, _SRC_FK, _SRC_FV, _SRC_FF = 1280, 1304, 1816, 2328, 2840
_Q0, _KVC0, _KS0, _VS0, _KW0, _VW0, _FQ0, _FK0, _FV0, _SM0, _NP = (
    0, 1024, 1536, 1792, 2048, 2304, 2560, 3072, 3584, 4096, 4352)
_FF_LANE = 16


def _projection_columns():
    src = np.full((_NP,), -1, np.int32)
    d = np.arange(HEAD_DIM)
    for h in range(N_NSA_HEADS):
        src[_Q0 + LANE * h + d] = _SRC_NQ + HEAD_DIM * h + d
    for i in range(4):
        src[_KVC0 + LANE * i + d] = _SRC_KC + HEAD_DIM * i + d
    for g in range(N_NSA_KV):
        src[_KS0 + LANE * g + d] = _SRC_KSEL + HEAD_DIM * g + d
        src[_KW0 + LANE * g + d] = _SRC_KWIN + HEAD_DIM * g + d
        for half in range(2):
            src[_VS0 + LANE * g + HEAD_DIM * half + d] = _SRC_VSEL + HEAD_DIM * g + d
            src[_VW0 + LANE * g + HEAD_DIM * half + d] = _SRC_VWIN + HEAD_DIM * g + d
        gates = np.arange(NSA_GROUP * N_BRANCH)
        src[_SM0 + LANE * g + gates] = _SRC_GATE + NSA_GROUP * N_BRANCH * g + gates
        src[_SM0 + LANE * g + _FF_LANE + np.arange(N_FOX_HEADS)] = _SRC_FF + np.arange(N_FOX_HEADS)
    fox = np.arange(N_FOX_HEADS * HEAD_DIM)
    src[_FQ0 + fox] = _SRC_FQ + fox
    src[_FK0 + fox] = _SRC_FK + fox
    src[_FV0 + fox] = _SRC_FV + fox
    return src


def _permute_columns(w, src):
    pieces, i = [], 0
    while i < len(src):
        j = i + 1
        if src[i] < 0:
            while j < len(src) and src[j] < 0:
                j += 1
            pieces.append(jnp.zeros(w.shape[:-1] + (j - i,), w.dtype))
        else:
            while j < len(src) and src[j] == src[j - 1] + 1:
                j += 1
            pieces.append(w[..., int(src[i]):int(src[i]) + (j - i)])
        i = j
    return jnp.concatenate(pieces, axis=-1)


def _rms(x, g):
    ms = jnp.mean(x * x, axis=-1, keepdims=True)
    return x * lax.rsqrt(ms + NORM_EPS) * g


def _dot(a, b):
    return jnp.dot(a, b, preferred_element_type=F32)


def _dot_nt(a, b):
    return lax.dot_general(a, b, (((1,), (1,)), ((), ())), preferred_element_type=F32)


def _split3(x):
    hi = x.astype(MXU_DTYPE)
    r1 = x - hi.astype(F32)
    mid = r1.astype(MXU_DTYPE)
    lo = (r1 - mid.astype(F32)).astype(MXU_DTYPE)
    return hi, mid, lo


def _rows(shape):
    return lax.broadcasted_iota(jnp.int32, shape, 0)


def _lanes(shape):
    return lax.broadcasted_iota(jnp.int32, shape, 1)


def _inproj_body(x_ref, g_ref, w_ref, rc_ref, ra_ref, rb_ref,
                 qn_ref, qr_ref, kvc_ref, ksa_ref, vs_ref, kw_ref, vw_ref,
                 fq_ref, fk_ref, fv_ref, small_ref, *, tm, n_s_tiles):
    h = _rms(x_ref[...], g_ref[...]).astype(MXU_DTYPE)
    rc, ra, rb = rc_ref[...], ra_ref[...], rb_ref[...]

    def mm(lo, n):
        return _dot(h, w_ref[:, lo:lo + n])

    def rope(v):
        return v * rc + pltpu.roll(v, LANE - ROPE_DIMS // 2, 1) * ra + pltpu.roll(v, ROPE_DIMS // 2, 1) * rb

    q = mm(_Q0, N_NSA_HEADS * LANE) * Q_SCALE
    qn_ref[...] = q.astype(MXU_DTYPE)
    for hh in range(N_NSA_HEADS):
        sl = slice(LANE * hh, LANE * (hh + 1))
        qr_ref[:, sl] = rope(q[:, sl]).astype(MXU_DTYPE)

    kvc = mm(_KVC0, 4 * LANE)
    for i in range(4):
        kvc_ref[i] = kvc[:, LANE * i:LANE * i + HEAD_DIM]

    pos = (pl.program_id(0) % n_s_tiles) * tm + _rows((tm, LANE))
    onehot = ((_lanes((tm, LANE)) - HEAD_DIM) == (pos >> SEL_SHIFT)).astype(F32)
    ks = mm(_KS0, N_NSA_KV * LANE)
    kw = mm(_KW0, N_NSA_KV * LANE)
    for g in range(N_NSA_KV):
        sl = slice(LANE * g, LANE * (g + 1))
        ksa_ref[:, sl] = (rope(ks[:, sl]) + onehot).astype(MXU_DTYPE)
        kw_ref[:, sl] = rope(kw[:, sl]).astype(MXU_DTYPE)
    vs_ref[...] = mm(_VS0, N_NSA_KV * LANE).astype(MXU_DTYPE)
    vw_ref[...] = mm(_VW0, N_NSA_KV * LANE).astype(MXU_DTYPE)

    nf = N_FOX_HEADS * HEAD_DIM
    fq_ref[...] = (mm(_FQ0, nf) * Q_SCALE).astype(MXU_DTYPE)
    fk_ref[...] = mm(_FK0, nf).astype(MXU_DTYPE)
    fv_ref[...] = mm(_FV0, nf).astype(MXU_DTYPE)
    small_ref[...] = mm(_SM0, N_NSA_KV * LANE)


def _inproj(xf, g, wp, rope_c, rope_a, rope_b, *, layer, seq, tm):
    m, d = xf.shape
    n_s_tiles = seq // tm
    nf = N_FOX_HEADS * HEAD_DIM
    row = lambda i: (i, 0)
    fixed = lambda i: (0, 0)
    tab = lambda i: (i % n_s_tiles, 0)
    out_shape = (
        jax.ShapeDtypeStruct((m, N_NSA_HEADS * LANE), MXU_DTYPE),
        jax.ShapeDtypeStruct((m, N_NSA_HEADS * LANE), MXU_DTYPE),
        jax.ShapeDtypeStruct((4, m, HEAD_DIM), F32),
        jax.ShapeDtypeStruct((m, N_NSA_KV * LANE), MXU_DTYPE),
        jax.ShapeDtypeStruct((m, N_NSA_KV * LANE), MXU_DTYPE),
        jax.ShapeDtypeStruct((m, N_NSA_KV * LANE), MXU_DTYPE),
        jax.ShapeDtypeStruct((m, N_NSA_KV * LANE), MXU_DTYPE),
        jax.ShapeDtypeStruct((m, nf), MXU_DTYPE),
        jax.ShapeDtypeStruct((m, nf), MXU_DTYPE),
        jax.ShapeDtypeStruct((m, nf), MXU_DTYPE),
        jax.ShapeDtypeStruct((m, N_NSA_KV * LANE), F32),
    )
    out_specs = (
        pl.BlockSpec((tm, N_NSA_HEADS * LANE), row),
        pl.BlockSpec((tm, N_NSA_HEADS * LANE), row),
        pl.BlockSpec((4, tm, HEAD_DIM), lambda i: (0, i, 0)),
        pl.BlockSpec((tm, N_NSA_KV * LANE), row),
        pl.BlockSpec((tm, N_NSA_KV * LANE), row),
        pl.BlockSpec((tm, N_NSA_KV * LANE), row),
        pl.BlockSpec((tm, N_NSA_KV * LANE), row),
        pl.BlockSpec((tm, nf), row),
        pl.BlockSpec((tm, nf), row),
        pl.BlockSpec((tm, nf), row),
        pl.BlockSpec((tm, N_NSA_KV * LANE), row),
    )
    return pl.pallas_call(
        functools.partial(_inproj_body, tm=tm, n_s_tiles=n_s_tiles),
        grid=(m // tm,),
        in_specs=[
            pl.BlockSpec((tm, d), row),
            pl.BlockSpec((1, d), fixed),
            pl.BlockSpec((None, d, _NP), lambda i: (layer, 0, 0)),
            pl.BlockSpec((tm, LANE), tab),
            pl.BlockSpec((tm, LANE), tab),
            pl.BlockSpec((tm, LANE), tab),
        ],
        out_specs=out_specs,
        out_shape=out_shape,
        compiler_params=pltpu.CompilerParams(
            dimension_semantics=("arbitrary",), vmem_limit_bytes=VMEM_LIMIT),
        name="inproj",
    )(xf, g, wp, rope_c, rope_a, rope_b)


def _cumsum_body(sm_ref, bias_ref, tril_ref, out_ref, *, seq, chunk):
    carry = jnp.zeros((1, LANE), F32)
    tril = tril_ref[...]
    for c in range(seq // chunk):
        sl = slice(c * chunk, (c + 1) * chunk)
        z = sm_ref[sl, :] + bias_ref[...]
        logf = jnp.minimum(z, 0.0) - jnp.log1p(jnp.exp(-jnp.abs(z)))
        hi, mid, lo = _split3(logf)
        cs = (_dot(tril, hi) + _dot(tril, mid)) + _dot(tril, lo) + carry
        out_ref[0, :, sl] = (cs * LOG2E).T[_FF_LANE:_FF_LANE + N_FOX_HEADS, :]
        carry = cs[chunk - 1:chunk, :]


def _forget_cumsum(small, fbias, tril, *, batch, seq):
    chunk = tril.shape[0]
    return pl.pallas_call(
        functools.partial(_cumsum_body, seq=seq, chunk=chunk),
        grid=(batch,),
        in_specs=[
            pl.BlockSpec((seq, LANE), lambda b: (b, 0)),
            pl.BlockSpec((1, LANE), lambda b: (0, 0)),
            pl.BlockSpec((chunk, chunk), lambda b: (0, 0)),
        ],
        out_specs=pl.BlockSpec((1, N_FOX_HEADS, seq), lambda b: (b, 0, 0)),
        out_shape=jax.ShapeDtypeStruct((batch, N_FOX_HEADS, seq), F32),
        compiler_params=pltpu.CompilerParams(
            dimension_semantics=("arbitrary",), vmem_limit_bytes=VMEM_LIMIT),
        name="forget_cumsum",
    )(small, fbias, tril)


def _compress_body(x_ref, pos_ref, w1_ref, b1_ref, w2_ref, b2_ref, o_ref, *, n_chunk):
    half = CMP_STRIDE * HEAD_DIM
    x = jnp.concatenate(
        [x_ref[0, pl.ds(t, n_chunk, stride=CMP_STRIDE), :] for t in range(CMP_STRIDE)], axis=1)
    first = _dot((x + pos_ref[0, 0:1, :]).astype(MXU_DTYPE), w1_ref[0, 0:half, :])
    second = _dot((x + pos_ref[0, 1:2, :]).astype(MXU_DTYPE), w1_ref[0, half:2 * half, :])
    hid = first + pltpu.roll(second, n_chunk - 1, 0) + b1_ref[0]
    act = jax.nn.gelu(hid)
    o_ref[0, 0] = (_dot(act.astype(MXU_DTYPE), w2_ref[0]) + b2_ref[0]).astype(MXU_DTYPE)


def _compress(kvc, pos, w1, b1, w2p, b2p, *, batch, n_chunk):
    half = CMP_STRIDE * HEAD_DIM
    kv = lambda i, b: (i // N_NSA_KV, 0, 0)
    return pl.pallas_call(
        functools.partial(_compress_body, n_chunk=n_chunk),
        grid=(2 * N_NSA_KV, batch),
        in_specs=[
            pl.BlockSpec((1, n_chunk * CMP_STRIDE, HEAD_DIM), lambda i, b: (i, b, 0)),
            pl.BlockSpec((1, 2, half), kv),
            pl.BlockSpec((1, 2 * half, CMP_HIDDEN), kv),
            pl.BlockSpec((1, 1, CMP_HIDDEN), kv),
            pl.BlockSpec((1, CMP_HIDDEN, LANE), kv),
            pl.BlockSpec((1, 1, LANE), kv),
        ],
        out_specs=pl.BlockSpec((1, 1, n_chunk, LANE), lambda i, b: (i, b, 0, 0)),
        out_shape=jax.ShapeDtypeStruct((2 * N_NSA_KV, batch, n_chunk, LANE), MXU_DTYPE),
        compiler_params=pltpu.CompilerParams(
            dimension_semantics=("arbitrary", "arbitrary"), vmem_limit_bytes=VMEM_LIMIT),
        name="compress",
    )(kvc, pos, w1, b1, w2p, b2p)


def _lane_max(s):
    out = s[:, 0:LANE]
    for c in range(1, s.shape[1] // LANE):
        out = jnp.maximum(out, s[:, LANE * c:LANE * (c + 1)])
    return out


def _row_max_lanes(mrun):
    return jnp.broadcast_to(jnp.max(mrun, axis=-1, keepdims=True), mrun.shape)


def _exp_tiles(s, mb):
    n = s.shape[1] // LANE
    return jnp.concatenate(
        [jnp.exp2(s[:, LANE * c:LANE * (c + 1)] - mb).astype(MXU_DTYPE) for c in range(n)], axis=1)


def _values_with_ones(v, ones_low):
    low = _lanes(v.shape) < HEAD_DIM
    ones = jnp.ones_like(v)
    return jnp.where(low, ones, v) if ones_low else jnp.where(low, v, ones)


def _normalize(acc):
    return acc / pltpu.roll(acc, HEAD_DIM, 1)


def _spans(n_tiles, groups, fn):
    start = 0
    for size in tuple(groups) + (1,):
        n_groups = lax.div(n_tiles - start, size)

        def step(t, carry, size=size, start=start):
            fn(start + t * size, size)
            return carry

        lax.fori_loop(0, n_groups, step, 0)
        start = start + n_groups * size


def _nsa_body(qn_ref, qr_ref, kc_ref, vc_ref, ksa_ref, vs_ref, kw_ref, vw_ref, sm_ref, gb_ref, impt_ref,
              o_ref, s_ref, mrun_ref, acc_ref, *, tq, tk, group, n_sel, top_k):
    q0 = pl.program_id(2) * tq
    r = NSA_GROUP * tq
    n_cmp = kc_ref.shape[2]

    def window_scores(hq):
        n_wc = WINDOW // LANE + 1
        upper = _lanes((NSA_GROUP * LANE, LANE)) > (_rows((NSA_GROUP * LANE, LANE)) & (LANE - 1))
        qs = q0 + hq * LANE
        qh = jnp.concatenate(
            [qr_ref[hq * LANE:(hq + 1) * LANE, LANE * h:LANE * (h + 1)] for h in range(NSA_GROUP)], axis=0)
        starts = [pl.multiple_of(jnp.maximum(qs - WINDOW + LANE * c, 0), LANE) for c in range(n_wc)]
        flag_lane = _lanes((LANE, LANE)) == HEAD_DIM
        k_chunks = []
        for c, st in enumerate(starts):
            kc = kw_ref[pl.ds(st, LANE), :]
            if c < n_wc - 1:
                before_start = jnp.logical_and(flag_lane, qs - WINDOW + LANE * c < 0)
                kc = jnp.where(before_start, jnp.ones_like(kc), kc)
            k_chunks.append(kc)
        kk = jnp.concatenate(k_chunks, axis=0)
        vv = jnp.concatenate([vw_ref[pl.ds(st, LANE), :] for st in starts], axis=0)
        q_flag = _lanes(qh.shape) == HEAD_DIM
        sw = _dot_nt(jnp.where(q_flag, jnp.full_like(qh, NEG_INF), qh), kk)
        chunks = [sw[:, LANE * c:LANE * (c + 1)] for c in range(n_wc)]
        chunks[0] = jnp.where(upper, chunks[0], NEG_INF)
        chunks[-1] = jnp.where(upper, NEG_INF, chunks[-1])
        return chunks, vv

    def window_probs(chunks):
        mb = _row_max_lanes(functools.reduce(jnp.maximum, chunks))
        return jnp.concatenate([jnp.exp2(ch - mb).astype(MXU_DTYPE) for ch in chunks], axis=1)

    def window_out(pw, vv):
        return _normalize(_dot(pw, _values_with_ones(vv, False)))

    q4n = jnp.concatenate([qn_ref[:, LANE * h:LANE * (h + 1)] for h in range(NSA_GROUP)], axis=0)
    sc = _dot_nt(q4n, kc_ref[0, 0])
    win_scores = [window_scores(0), window_scores(1)]
    last_vis = (q0 + _rows((tq, LANE)) - (CMP_LEN - 1)) >> CMP_SHIFT
    last_vis = jnp.concatenate([last_vis] * NSA_GROUP, axis=0)
    sc_chunks = [
        jnp.where(_lanes((r, LANE)) + LANE * c <= last_vis, sc[:, LANE * c:LANE * (c + 1)], NEG_INF)
        for c in range(n_cmp // LANE)]
    mb = jnp.maximum(_row_max_lanes(functools.reduce(jnp.maximum, sc_chunks)), 0.1 * NEG_INF)
    e_chunks = [jnp.exp2(ch - mb) for ch in sc_chunks]
    den = jnp.sum(functools.reduce(jnp.add, e_chunks), axis=-1, keepdims=True)
    inv = 1.0 / jnp.maximum(jnp.broadcast_to(den, (r, LANE)), 1e-30)
    pc = jnp.concatenate([ech * inv for ech in e_chunks], axis=1)
    o_cmp = _dot(pc.astype(MXU_DTYPE), vc_ref[0, 0])

    pcs = (pc[0:tq] + pc[tq:2 * tq]) + (pc[2 * tq:3 * tq] + pc[3 * tq:4 * tq])
    impt_w = impt_ref[...]
    hi, mid, lo = _split3(pcs)
    imp_t = (_dot_nt(impt_w, hi) + _dot_nt(impt_w, mid)) + _dot_nt(impt_w, lo)
    jrow = _rows((n_sel, tq))
    qpos = q0 + _lanes((n_sel, tq))
    cur = qpos >> SEL_SHIFT
    valid = jrow * SEL_BLOCK <= qpos
    forced = (jrow == 0) | (jrow == cur) | (jrow == cur - 1)
    score = jnp.where(forced, FORCED_SCORE, jnp.where(valid, imp_t, -1.0))

    sub = SUBLANE
    chunks = [score[sub * c:sub * (c + 1), :] for c in range(n_sel // sub)]
    ranks = [jnp.zeros((sub, tq), F32) for _ in chunks]
    later = _rows((sub, tq))

    win_probs = [window_probs(win_scores[0][0])]
    win_blocks = []
    for jp in range(n_sel):
        if jp == n_sel // 2:
            win_blocks.append(window_out(win_probs[0], win_scores[0][1]))
            win_probs.append(window_probs(win_scores[1][0]))
        cj, rj = divmod(jp, sub)
        rowv = chunks[cj][rj:rj + 1, :]
        for c, sc_c in enumerate(chunks):
            if c > cj:
                beats = rowv >= sc_c
            elif c < cj:
                beats = rowv > sc_c
            else:
                beats = (rowv > sc_c) | ((rowv == sc_c) & (later > rj))
            ranks[c] = ranks[c] + jnp.where(beats, 1.0, 0.0)
    sel_t = jnp.where(jnp.concatenate(ranks, axis=0) < float(top_k), 1.0, 0.0)
    sel_pad = jnp.concatenate([sel_t, jnp.zeros((LANE - n_sel, tq), F32)], axis=0)
    sel_q = pltpu.roll(sel_pad.T, HEAD_DIM, 1)
    sel_bias = ((sel_q - 1.0) * (-NEG_INF)).astype(MXU_DTYPE)

    high = _lanes((tq, LANE)) >= HEAD_DIM
    q4r = jnp.concatenate(
        [jnp.where(high, sel_bias, qr_ref[:, LANE * h:LANE * (h + 1)]) for h in range(NSA_GROUP)], axis=0)

    n_full = lax.div(q0, tk)
    assert tq == tk

    def score_span(kb, n_tiles, masked):
        k0 = pl.multiple_of(kb * tk, tk)
        s = _dot_nt(q4r, ksa_ref[pl.ds(k0, n_tiles * tk), :])
        if masked:
            s = jnp.where(_lanes((r, tk)) <= (_rows((r, tk)) & (tq - 1)), s, NEG_INF)
        for j in range(n_tiles):
            s_ref[kb + j] = s[:, j * tk:(j + 1) * tk]
        mrun_ref[...] = jnp.maximum(mrun_ref[...], _lane_max(s))

    def value_span(kb, n_tiles):
        k0 = pl.multiple_of(kb * tk, tk)
        mb = mrun_ref[...]
        p = jnp.concatenate([_exp_tiles(s_ref[kb + j], mb) for j in range(n_tiles)], axis=1)
        acc_ref[...] += _dot(p, _values_with_ones(vs_ref[pl.ds(k0, n_tiles * tk), :], False))

    win_blocks.append(window_out(win_probs[1], win_scores[1][1]))
    o_win = jnp.concatenate([blk[h * LANE:(h + 1) * LANE] for h in range(NSA_GROUP) for blk in win_blocks], axis=0)

    gate = jax.nn.sigmoid(sm_ref[...] + gb_ref[...])
    partial, gate_sel = [], []
    for h in range(NSA_GROUP):
        rs = slice(h * tq, (h + 1) * tq)
        c = N_BRANCH * h
        partial.append(gate[:, c:c + 1] * o_cmp[rs] + gate[:, c + 2:c + 3] * o_win[rs])
        gate_sel.append(jnp.broadcast_to(gate[:, c + 1:c + 2], (tq, LANE)))

    mrun_ref[...] = jnp.full((r, LANE), NEG_INF, F32)
    score_span(n_full, 1, True)
    _spans(n_full, group, lambda kb, n: score_span(kb, n, False))
    mrun_ref[...] = _row_max_lanes(mrun_ref[...])
    acc_ref[...] = jnp.zeros((r, LANE), F32)
    _spans(n_full + 1, group, value_span)
    o_sel = _normalize(acc_ref[...])

    heads = [partial[h] + gate_sel[h] * o_sel[h * tq:(h + 1) * tq] for h in range(NSA_GROUP)]
    low = _lanes((tq, LANE)) < HEAD_DIM
    for p in range(NSA_GROUP // 2):
        pair = jnp.where(low, heads[2 * p], pltpu.roll(heads[2 * p + 1], HEAD_DIM, 1))
        o_ref[:, LANE * p:LANE * (p + 1)] = pair.astype(MXU_DTYPE)


def _nsa(qn, qr, cmp_kv, ksa, vs, kw, vw, small, gbias, imp_t, *, batch, seq, tq, tk, group):
    assert tq == 2 * LANE
    n_q = seq // tq
    n_cmp = cmp_kv.shape[2]
    n_sel = seq // SEL_BLOCK
    top_k = min(SEL_TOPK, n_sel)
    gw = NSA_GROUP * LANE
    qmap = lambda b, g, i: (b * n_q + i, g)
    kvmap = lambda b, g, i: (b, g)
    r = NSA_GROUP * tq
    return pl.pallas_call(
        functools.partial(_nsa_body, tq=tq, tk=tk, group=group, n_sel=n_sel, top_k=top_k),
        grid=(batch, N_NSA_KV, n_q),
        in_specs=[
            pl.BlockSpec((tq, gw), qmap),
            pl.BlockSpec((tq, gw), qmap),
            pl.BlockSpec((1, 1, n_cmp, LANE), lambda b, g, i: (g, b, 0, 0)),
            pl.BlockSpec((1, 1, n_cmp, LANE), lambda b, g, i: (N_NSA_KV + g, b, 0, 0)),
            pl.BlockSpec((seq, LANE), kvmap),
            pl.BlockSpec((seq, LANE), kvmap),
            pl.BlockSpec((seq, LANE), kvmap),
            pl.BlockSpec((seq, LANE), kvmap),
            pl.BlockSpec((tq, LANE), qmap),
            pl.BlockSpec((1, LANE), lambda b, g, i: (0, g)),
            pl.BlockSpec((n_sel, n_cmp), lambda b, g, i: (0, 0)),
        ],
        out_specs=pl.BlockSpec((tq, NSA_GROUP * HEAD_DIM), qmap),
        out_shape=jax.ShapeDtypeStruct((batch * seq, N_NSA_HEADS * HEAD_DIM), MXU_DTYPE),
        scratch_shapes=[
            pltpu.VMEM((seq // tk, r, tk), F32),
            pltpu.VMEM((r, LANE), F32),
            pltpu.VMEM((r, LANE), F32),
        ],
        compiler_params=pltpu.CompilerParams(
            dimension_semantics=("arbitrary", "arbitrary", "arbitrary"), vmem_limit_bytes=VMEM_LIMIT),
        name="nsa_attention",
    )(qn, qr, cmp_kv, cmp_kv, ksa, vs, kw, vw, small, gbias, imp_t)


def _fox_body(q_ref, k_ref, v_ref, cum_ref, o_ref, s_ref, mrun_ref, acc_ref, *, tq, tk, group, n_q):
    for n in range(n_q):
        pl.when(pl.program_id(2) == n)(functools.partial(
            _fox_block, q_ref, k_ref, v_ref, cum_ref, o_ref, s_ref, mrun_ref, acc_ref,
            tq=tq, tk=tk, group=group, n_full=n))


def _static_spans(n_tiles, group, fn):
    kb = 0
    while kb < n_tiles:
        size = min(group, n_tiles - kb)
        fn(kb, size)
        kb += size


def _fox_block(q_ref, k_ref, v_ref, cum_ref, o_ref, s_ref, mrun_ref, acc_ref, *, tq, tk, group, n_full):
    q0 = n_full * tq
    qp = q_ref[...]
    low = _lanes((tq, LANE)) < HEAD_DIM
    assert tq == tk
    qs, cqs, cum_rows = [], [], []
    for e in range(2):
        qs.append(jnp.where(low if e == 0 else jnp.logical_not(low), qp, jnp.zeros_like(qp)))
        cum_rows.append(2 * pl.program_id(1) + e)
        cq = jnp.broadcast_to(cum_ref[0, pl.ds(cum_rows[e], 1), q0:q0 + tq], (LANE, tq)).T
        cqs.append(jnp.concatenate([cq] * (tk // LANE), axis=1))

    def score_span(kb, n_tiles, masked):
        k0 = kb * tk
        k = k_ref[k0:k0 + n_tiles * tk, :]
        for e in range(2):
            s = _dot_nt(qs[e], k)
            for j in range(n_tiles):
                ck = cum_ref[0, pl.ds(cum_rows[e], 1), k0 + j * tk:k0 + (j + 1) * tk]
                sj = s[:, j * tk:(j + 1) * tk] + cqs[e] - ck
                if masked:
                    sj = jnp.where(_lanes((tq, tk)) <= _rows((tq, tk)), sj, NEG_INF)
                s_ref[e, kb + j] = sj
                mrun_ref[e] = jnp.maximum(mrun_ref[e], _lane_max(sj))

    def value_span(kb, n_tiles):
        k0 = kb * tk
        v = v_ref[k0:k0 + n_tiles * tk, :]
        for e in range(2):
            mb = mrun_ref[e]
            p = jnp.concatenate([_exp_tiles(s_ref[e, kb + j], mb) for j in range(n_tiles)], axis=1)
            acc_ref[e] += _dot(p, _values_with_ones(v, e == 1))

    mrun_ref[...] = jnp.full((2, tq, LANE), NEG_INF, F32)
    score_span(n_full, 1, True)
    _static_spans(n_full, group, lambda kb, n: score_span(kb, n, False))
    for e in range(2):
        mrun_ref[e] = _row_max_lanes(mrun_ref[e])
    acc_ref[...] = jnp.zeros((2, tq, LANE), F32)
    _static_spans(n_full + 1, group, value_span)
    o_ref[...] = jnp.where(low, _normalize(acc_ref[0]), _normalize(acc_ref[1])).astype(MXU_DTYPE)


def _fox(fq, fk, fv, cum, *, batch, seq, tq, tk, group):
    n_q = seq // tq
    n_pairs = N_FOX_HEADS // 2
    qmap = lambda b, p, i: (b * n_q + i, p)
    kvmap = lambda b, p, i: (b, p)
    return pl.pallas_call(
        functools.partial(_fox_body, tq=tq, tk=tk, group=group, n_q=n_q),
        grid=(batch, n_pairs, n_q),
        in_specs=[
            pl.BlockSpec((tq, LANE), qmap),
            pl.BlockSpec((seq, LANE), kvmap),
            pl.BlockSpec((seq, LANE), kvmap),
            pl.BlockSpec((1, N_FOX_HEADS, seq), lambda b, p, i: (b, 0, 0)),
        ],
        out_specs=pl.BlockSpec((tq, LANE), qmap),
        out_shape=jax.ShapeDtypeStruct((batch * seq, N_FOX_HEADS * HEAD_DIM), MXU_DTYPE),
        scratch_shapes=[
            pltpu.VMEM((2, seq // tk, tq, tk), F32),
            pltpu.VMEM((2, tq, LANE), F32),
            pltpu.VMEM((2, tq, LANE), F32),
        ],
        compiler_params=pltpu.CompilerParams(
            dimension_semantics=("arbitrary", "arbitrary", "arbitrary"), vmem_limit_bytes=VMEM_LIMIT),
        name="fox_attention",
    )(fq, fk, fv, cum)


def _post_body(on_ref, of_ref, x_ref, wo_ref, gpm_ref, gpl_ref, wu_ref, wd_ref, gpo_ref, o_ref, *, ff_chunk):
    n_nsa = on_ref.shape[1]
    d_ff = wu_ref.shape[1]
    n_ff = d_ff // ff_chunk
    tm = x_ref.shape[0]
    halves = [slice(0, tm // 2), slice(tm // 2, tm)]

    def out_proj(rows):
        return _dot(on_ref[rows, :], wo_ref[0:n_nsa, :]) + _dot(of_ref[rows, :], wo_ref[n_nsa:, :])

    def norms(rows, mix):
        x1 = x_ref[rows, :] + _rms(mix, gpm_ref[...])
        return x1, _rms(x1, gpl_ref[...]).astype(MXU_DTYPE)

    def mlp_chunk(h, acc, c):
        sl = slice(c * ff_chunk, (c + 1) * ff_chunk)
        a = jnp.square(jnp.maximum(_dot(h, wu_ref[:, sl]), 0.0))
        return acc + _dot(a.astype(MXU_DTYPE), wd_ref[sl, :])

    mix_a, mix_b = out_proj(halves[0]), out_proj(halves[1])
    x1_a, h_a = norms(halves[0], mix_a)
    acc_a = mlp_chunk(h_a, jnp.zeros(x1_a.shape, F32), 0)
    x1_b, h_b = norms(halves[1], mix_b)
    for c in range(1, n_ff):
        acc_a = mlp_chunk(h_a, acc_a, c)
    acc_b = mlp_chunk(h_b, jnp.zeros(x1_b.shape, F32), 0)
    o_ref[halves[0], :] = x1_a + _rms(acc_a, gpo_ref[...])
    for c in range(1, n_ff):
        acc_b = mlp_chunk(h_b, acc_b, c)
    o_ref[halves[1], :] = x1_b + _rms(acc_b, gpo_ref[...])


def _post(o_nsa, o_fox, xf, wo, g_post_mix, g_pre_mlp, wu, wd, g_post_mlp, *, layer, tm, ff_chunk):
    m, d = xf.shape
    d_ff = wu.shape[2]
    row = lambda i: (i, 0)
    fixed = lambda i: (0, 0)
    resident = lambda rows, cols: pl.BlockSpec(
        (None, rows, cols), lambda i: (layer, 0, 0), pipeline_mode=pl.Buffered(1))
    return pl.pallas_call(
        functools.partial(_post_body, ff_chunk=ff_chunk),
        grid=(m // tm,),
        in_specs=[
            pl.BlockSpec((tm, o_nsa.shape[1]), row),
            pl.BlockSpec((tm, o_fox.shape[1]), row),
            pl.BlockSpec((tm, d), row),
            resident(wo.shape[1], d),
            pl.BlockSpec((1, d), fixed),
            pl.BlockSpec((1, d), fixed),
            resident(d, d_ff),
            resident(d_ff, d),
            pl.BlockSpec((1, d), fixed),
        ],
        out_specs=pl.BlockSpec((tm, d), row),
        out_shape=jax.ShapeDtypeStruct((m, d), F32),
        compiler_params=pltpu.CompilerParams(
            dimension_semantics=("arbitrary",), vmem_limit_bytes=VMEM_LIMIT),
        name="outproj_mlp",
    )(o_nsa, o_fox, xf, wo, g_post_mix, g_pre_mlp, wu, wd, g_post_mlp)


def _rope_tables(seq):
    half = ROPE_DIMS // 2
    inv_freq = ROPE_THETA ** (-jnp.arange(half, dtype=F32) * 2.0 / ROPE_DIMS)
    ang = jnp.arange(seq).astype(F32)[:, None] * inv_freq[None, :]
    cos, sin = jnp.cos(ang), jnp.sin(ang)
    pad = jnp.zeros((seq, LANE - ROPE_DIMS), F32)
    zero = jnp.zeros((seq, half), F32)
    c = jnp.concatenate([cos, cos, pad + 1.0], axis=-1)
    a = jnp.concatenate([-sin, zero, pad], axis=-1)
    b = jnp.concatenate([zero, sin, pad], axis=-1)
    return c, a, b


def _importance_map_t(seq, n_chunk):
    n_cmp = (seq - CMP_LEN) // CMP_STRIDE + 1
    n_sel = seq // SEL_BLOCK
    cs = np.arange(n_cmp)[None, :] * CMP_STRIDE
    ss = np.arange(n_sel)[:, None] * SEL_BLOCK
    overlap = np.clip(np.minimum(cs + CMP_LEN, ss + SEL_BLOCK) - np.maximum(cs, ss), 0, None) / CMP_LEN
    out = np.zeros((n_sel, n_chunk), np.float32)
    out[:, :n_cmp] = overlap
    return jnp.asarray(out, dtype=MXU_DTYPE)


def kernel(x, w_in, b_nsa_gate, b_forget, cmp_pos_k, cmp_w1_k, cmp_b1_k, cmp_w2_k, cmp_b2_k, cmp_pos_v, cmp_w1_v, cmp_b1_v, cmp_w2_v, cmp_b2_v, w_out, w_up, w_down, g_pre_mix, g_post_mix, g_pre_mlp, g_post_mlp):
    batch, seq, d_model = x.shape
    depth = w_in.shape[0]
    m = batch * seq
    n_chunk = seq // CMP_STRIDE
    n_sel = seq // SEL_BLOCK
    tm = min(512, seq)
    nsa_tq, nsa_tk = 2 * LANE, min(256, seq)
    fox_t = min(512, seq)
    assert seq % tm == 0 and seq % fox_t == 0 and seq % nsa_tk == 0
    assert n_sel <= HEAD_DIM and n_sel % 8 == 0 and seq >= WINDOW + nsa_tq
    assert x.dtype == F32

    rope_c, rope_a, rope_b = _rope_tables(seq)
    imp_t = _importance_map_t(seq, n_chunk)
    cs_chunk = min(256, seq)
    tril = jnp.asarray(np.tril(np.ones((cs_chunk, cs_chunk), np.float32)), dtype=MXU_DTYPE)

    wp = _permute_columns(w_in.astype(MXU_DTYPE), _projection_columns())

    gate_bias = jnp.zeros((depth, 1, N_NSA_KV * LANE), F32)
    fbias = jnp.zeros((depth, 1, LANE), F32)
    n_gate = NSA_GROUP * N_BRANCH
    for g in range(N_NSA_KV):
        gate_bias = gate_bias.at[:, 0, LANE * g:LANE * g + n_gate].set(b_nsa_gate[:, n_gate * g:n_gate * (g + 1)])
    fbias = fbias.at[:, 0, _FF_LANE:_FF_LANE + N_FOX_HEADS].set(b_forget)

    half = CMP_STRIDE * HEAD_DIM
    cpos = jnp.stack([cmp_pos_k.reshape(depth, 2, half), cmp_pos_v.reshape(depth, 2, half)], axis=1)
    cw1 = jnp.stack([cmp_w1_k, cmp_w1_v], axis=1).astype(MXU_DTYPE)
    cb1 = jnp.stack([cmp_b1_k, cmp_b1_v], axis=1)[:, :, None, :]
    zw = jnp.zeros_like(cmp_w2_k)
    cw2 = jnp.stack([jnp.concatenate([cmp_w2_k, zw], axis=-1),
                     jnp.concatenate([cmp_w2_v, cmp_w2_v], axis=-1)], axis=1).astype(MXU_DTYPE)
    zb = jnp.zeros_like(cmp_b2_k)
    cb2 = jnp.stack([jnp.concatenate([cmp_b2_k, zb], axis=-1),
                     jnp.concatenate([cmp_b2_v, cmp_b2_v], axis=-1)], axis=1)[:, :, None, :]

    wo = w_out.astype(MXU_DTYPE)
    wu = w_up.astype(MXU_DTYPE)
    wd = w_down.astype(MXU_DTYPE)

    xf = x.reshape(m, d_model)
    for l in range(depth):
        qn, qr, kvc, ksa, vs, kw, vw, fq, fk, fv, small = _inproj(
            xf, g_pre_mix[l][None, :], wp, rope_c, rope_a, rope_b, layer=l, seq=seq, tm=tm)

        cum = _forget_cumsum(small, fbias[l], tril, batch=batch, seq=seq)

        cmp_kv = _compress(kvc, cpos[l], cw1[l], cb1[l], cw2[l], cb2[l],
                           batch=batch, n_chunk=n_chunk)
        o_nsa = _nsa(qn, qr, cmp_kv, ksa, vs, kw, vw, small, gate_bias[l], imp_t,
                     batch=batch, seq=seq, tq=nsa_tq, tk=nsa_tk, group=(8, 4, 2))
        o_fox = _fox(fq, fk, fv, cum, batch=batch, seq=seq, tq=fox_t, tk=fox_t, group=4)
        xf = _post(o_nsa, o_fox, xf, wo, g_post_mix[l][None, :], g_pre_mlp[l][None, :],
                   wu, wd, g_post_mlp[l][None, :], layer=l, tm=tm, ff_chunk=1024)
    return xf.reshape(batch, seq, d_model)
```

```python
import functools

import numpy as np
import jax
import jax.numpy as jnp
from jax import lax
from jax.experimental import pallas as pl
from jax.experimental.pallas import tpu as pltpu

HEAD_DIM = 64
N_NSA_HEADS = 8
N_NSA_KV = 2
NSA_GROUP = N_NSA_HEADS // N_NSA_KV
N_FOX_HEADS = 8
N_BRANCH = 3
CMP_LEN = 32
CMP_STRIDE = 16
CMP_SHIFT = 4
CMP_HIDDEN = 256
SEL_BLOCK = 64
SEL_SHIFT = 6
SEL_TOPK = 16
WINDOW = 512
ROPE_THETA = 500000.0
ROPE_DIMS = HEAD_DIM // 4
NORM_EPS = 1e-6
NEG_INF = -1e30
FORCED_SCORE = 1e4
SCALE = HEAD_DIM ** -0.5
LOG2E = 1.4426950408889634
Q_SCALE = SCALE * LOG2E

LANE = 128
SUBLANE = 8
MXU_DTYPE = jnp.bfloat16
F32 = jnp.float32
V7X_VMEM_BYTES = 64 * 1024 * 1024
VMEM_LIMIT = V7X_VMEM_BYTES * 7 // 8

_SRC_NQ, _SRC_KC, _SRC_VC, _SRC_KSEL, _SRC_VSEL, _SRC_KWIN, _SRC_VWIN = 0, 512, 640, 768, 896, 1024, 1152
_SRC_GATE, _SRC_FQ, _SRC_FK, _SRC_FV, _SRC_FF = 1280, 1304, 1816, 2328, 2840
_Q0, _KVC0, _KS0, _VS0, _KW0, _VW0, _FQ0, _FK0, _FV0, _SM0, _NP = (
    0, 1024, 1536, 1792, 2048, 2304, 2560, 3072, 3584, 4096, 4352)
_FF_LANE = 16


def _projection_columns():
    src = np.full((_NP,), -1, np.int32)
    d = np.arange(HEAD_DIM)
    for h in range(N_NSA_HEADS):
        src[_Q0 + LANE * h + d] = _SRC_NQ + HEAD_DIM * h + d
    for i in range(4):
        src[_KVC0 + LANE * i + d] = _SRC_KC + HEAD_DIM * i + d
    for g in range(N_NSA_KV):
        src[_KS0 + LANE * g + d] = _SRC_KSEL + HEAD_DIM * g + d
        src[_KW0 + LANE * g + d] = _SRC_KWIN + HEAD_DIM * g + d
        for half in range(2):
            src[_VS0 + LANE * g + HEAD_DIM * half + d] = _SRC_VSEL + HEAD_DIM * g + d
            src[_VW0 + LANE * g + HEAD_DIM * half + d] = _SRC_VWIN + HEAD_DIM * g + d
        gates = np.arange(NSA_GROUP * N_BRANCH)
        src[_SM0 + LANE * g + gates] = _SRC_GATE + NSA_GROUP * N_BRANCH * g + gates
        src[_SM0 + LANE * g + _FF_LANE + np.arange(N_FOX_HEADS)] = _SRC_FF + np.arange(N_FOX_HEADS)
    fox = np.arange(N_FOX_HEADS * HEAD_DIM)
    src[_FQ0 + fox] = _SRC_FQ + fox
    src[_FK0 + fox] = _SRC_FK + fox
    src[_FV0 + fox] = _SRC_FV + fox
    return src


def _permute_columns(w, src):
    pieces, i = [], 0
    while i < len(src):
        j = i + 1
        if src[i] < 0:
            while j < len(src) and src[j] < 0:
                j += 1
            pieces.append(jnp.zeros(w.shape[:-1] + (j - i,), w.dtype))
        else:
            while j < len(src) and src[j] == src[j - 1] + 1:
                j += 1
            pieces.append(w[..., int(src[i]):int(src[i]) + (j - i)])
        i = j
    return jnp.concatenate(pieces, axis=-1)


def _rms(x, g):
    ms = jnp.mean(x * x, axis=-1, keepdims=True)
    return x * lax.rsqrt(ms + NORM_EPS) * g


def _dot(a, b):
    return jnp.dot(a, b, preferred_element_type=F32)


def _dot_nt(a, b):
    return lax.dot_general(a, b, (((1,), (1,)), ((), ())), preferred_element_type=F32)


def _split3(x):
    hi = x.astype(MXU_DTYPE)
    r1 = x - hi.astype(F32)
    mid = r1.astype(MXU_DTYPE)
    lo = (r1 - mid.astype(F32)).astype(MXU_DTYPE)
    return hi, mid, lo


def _rows(shape):
    return lax.broadcasted_iota(jnp.int32, shape, 0)


def _lanes(shape):
    return lax.broadcasted_iota(jnp.int32, shape, 1)


def _inproj_body(x_ref, g_ref, w_ref, rc_ref, ra_ref, rb_ref,
                 qn_ref, qr_ref, kvc_ref, ksa_ref, vs_ref, kw_ref, vw_ref,
                 fq_ref, fk_ref, fv_ref, small_ref, *, tm, n_s_tiles):
    h = _rms(x_ref[...], g_ref[...]).astype(MXU_DTYPE)
    rc, ra, rb = rc_ref[...], ra_ref[...], rb_ref[...]

    def mm(lo, n):
        return _dot(h, w_ref[:, lo:lo + n])

    def rope(v):
        return v * rc + pltpu.roll(v, LANE - ROPE_DIMS // 2, 1) * ra + pltpu.roll(v, ROPE_DIMS // 2, 1) * rb

    q = mm(_Q0, N_NSA_HEADS * LANE) * Q_SCALE
    qn_ref[...] = q.astype(MXU_DTYPE)
    for hh in range(N_NSA_HEADS):
        sl = slice(LANE * hh, LANE * (hh + 1))
        qr_ref[:, sl] = rope(q[:, sl]).astype(MXU_DTYPE)

    kvc = mm(_KVC0, 4 * LANE)
    for i in range(4):
        kvc_ref[i] = kvc[:, LANE * i:LANE * i + HEAD_DIM]

    pos = (pl.program_id(0) % n_s_tiles) * tm + _rows((tm, LANE))
    onehot = ((_lanes((tm, LANE)) - HEAD_DIM) == (pos >> SEL_SHIFT)).astype(F32)
    ks = mm(_KS0, N_NSA_KV * LANE)
    kw = mm(_KW0, N_NSA_KV * LANE)
    for g in range(N_NSA_KV):
        sl = slice(LANE * g, LANE * (g + 1))
        ksa_ref[:, sl] = (rope(ks[:, sl]) + onehot).astype(MXU_DTYPE)
        kw_ref[:, sl] = rope(kw[:, sl]).astype(MXU_DTYPE)
    vs_ref[...] = mm(_VS0, N_NSA_KV * LANE).astype(MXU_DTYPE)
    vw_ref[...] = mm(_VW0, N_NSA_KV * LANE).astype(MXU_DTYPE)

    nf = N_FOX_HEADS * HEAD_DIM
    fq_ref[...] = (mm(_FQ0, nf) * Q_SCALE).astype(MXU_DTYPE)
    fk_ref[...] = mm(_FK0, nf).astype(MXU_DTYPE)
    fv_ref[...] = mm(_FV0, nf).astype(MXU_DTYPE)
    small_ref[...] = mm(_SM0, N_NSA_KV * LANE)


def _inproj(xf, g, wp, rope_c, rope_a, rope_b, *, layer, seq, tm):
    m, d = xf.shape
    n_s_tiles = seq // tm
    nf = N_FOX_HEADS * HEAD_DIM
    row = lambda i: (i, 0)
    fixed = lambda i: (0, 0)
    tab = lambda i: (i % n_s_tiles, 0)
    out_shape = (
        jax.ShapeDtypeStruct((m, N_NSA_HEADS * LANE), MXU_DTYPE),
        jax.ShapeDtypeStruct((m, N_NSA_HEADS * LANE), MXU_DTYPE),
        jax.ShapeDtypeStruct((4, m, HEAD_DIM), F32),
        jax.ShapeDtypeStruct((m, N_NSA_KV * LANE), MXU_DTYPE),
        jax.ShapeDtypeStruct((m, N_NSA_KV * LANE), MXU_DTYPE),
        jax.ShapeDtypeStruct((m, N_NSA_KV * LANE), MXU_DTYPE),
        jax.ShapeDtypeStruct((m, N_NSA_KV * LANE), MXU_DTYPE),
        jax.ShapeDtypeStruct((m, nf), MXU_DTYPE),
        jax.ShapeDtypeStruct((m, nf), MXU_DTYPE),
        jax.ShapeDtypeStruct((m, nf), MXU_DTYPE),
        jax.ShapeDtypeStruct((m, N_NSA_KV * LANE), F32),
    )
    out_specs = (
        pl.BlockSpec((tm, N_NSA_HEADS * LANE), row),
        pl.BlockSpec((tm, N_NSA_HEADS * LANE), row),
        pl.BlockSpec((4, tm, HEAD_DIM), lambda i: (0, i, 0)),
        pl.BlockSpec((tm, N_NSA_KV * LANE), row),
        pl.BlockSpec((tm, N_NSA_KV * LANE), row),
        pl.BlockSpec((tm, N_NSA_KV * LANE), row),
        pl.BlockSpec((tm, N_NSA_KV * LANE), row),
        pl.BlockSpec((tm, nf), row),
        pl.BlockSpec((tm, nf), row),
        pl.BlockSpec((tm, nf), row),
        pl.BlockSpec((tm, N_NSA_KV * LANE), row),
    )
    return pl.pallas_call(
        functools.partial(_inproj_body, tm=tm, n_s_tiles=n_s_tiles),
        grid=(m // tm,),
        in_specs=[
            pl.BlockSpec((tm, d), row),
            pl.BlockSpec((1, d), fixed),
            pl.BlockSpec((None, d, _NP), lambda i: (layer, 0, 0)),
            pl.BlockSpec((tm, LANE), tab),
            pl.BlockSpec((tm, LANE), tab),
            pl.BlockSpec((tm, LANE), tab),
        ],
        out_specs=out_specs,
        out_shape=out_shape,
        compiler_params=pltpu.CompilerParams(
            dimension_semantics=("arbitrary",), vmem_limit_bytes=VMEM_LIMIT),
        name="inproj",
    )(xf, g, wp, rope_c, rope_a, rope_b)


def _cumsum_body(sm_ref, bias_ref, tril_ref, out_ref, *, seq, chunk):
    carry = jnp.zeros((1, LANE), F32)
    tril = tril_ref[...]
    for c in range(seq // chunk):
        sl = slice(c * chunk, (c + 1) * chunk)
        z = sm_ref[sl, :] + bias_ref[...]
        logf = jnp.minimum(z, 0.0) - jnp.log1p(jnp.exp(-jnp.abs(z)))
        hi, mid, lo = _split3(logf)
        cs = (_dot(tril, hi) + _dot(tril, mid)) + _dot(tril, lo) + carry
        out_ref[0, :, sl] = (cs * LOG2E).T[_FF_LANE:_FF_LANE + N_FOX_HEADS, :]
        carry = cs[chunk - 1:chunk, :]


def _forget_cumsum(small, fbias, tril, *, batch, seq):
    chunk = tril.shape[0]
    return pl.pallas_call(
        functools.partial(_cumsum_body, seq=seq, chunk=chunk),
        grid=(batch,),
        in_specs=[
            pl.BlockSpec((seq, LANE), lambda b: (b, 0)),
            pl.BlockSpec((1, LANE), lambda b: (0, 0)),
            pl.BlockSpec((chunk, chunk), lambda b: (0, 0)),
        ],
        out_specs=pl.BlockSpec((1, N_FOX_HEADS, seq), lambda b: (b, 0, 0)),
        out_shape=jax.ShapeDtypeStruct((batch, N_FOX_HEADS, seq), F32),
        compiler_params=pltpu.CompilerParams(
            dimension_semantics=("arbitrary",), vmem_limit_bytes=VMEM_LIMIT),
        name="forget_cumsum",
    )(small, fbias, tril)


def _compress_body(x_ref, pos_ref, w1_ref, b1_ref, w2_ref, b2_ref, o_ref, *, n_chunk):
    half = CMP_STRIDE * HEAD_DIM
    x = jnp.concatenate(
        [x_ref[0, pl.ds(t, n_chunk, stride=CMP_STRIDE), :] for t in range(CMP_STRIDE)], axis=1)
    first = _dot((x + pos_ref[0, 0:1, :]).astype(MXU_DTYPE), w1_ref[0, 0:half, :])
    second = _dot((x + pos_ref[0, 1:2, :]).astype(MXU_DTYPE), w1_ref[0, half:2 * half, :])
    hid = first + pltpu.roll(second, n_chunk - 1, 0) + b1_ref[0]
    act = jax.nn.gelu(hid)
    o_ref[0, 0] = (_dot(act.astype(MXU_DTYPE), w2_ref[0]) + b2_ref[0]).astype(MXU_DTYPE)


def _compress(kvc, pos, w1, b1, w2p, b2p, *, batch, n_chunk):
    half = CMP_STRIDE * HEAD_DIM
    kv = lambda i, b: (i // N_NSA_KV, 0, 0)
    return pl.pallas_call(
        functools.partial(_compress_body, n_chunk=n_chunk),
        grid=(2 * N_NSA_KV, batch),
        in_specs=[
            pl.BlockSpec((1, n_chunk * CMP_STRIDE, HEAD_DIM), lambda i, b: (i, b, 0)),
            pl.BlockSpec((1, 2, half), kv),
            pl.BlockSpec((1, 2 * half, CMP_HIDDEN), kv),
            pl.BlockSpec((1, 1, CMP_HIDDEN), kv),
            pl.BlockSpec((1, CMP_HIDDEN, LANE), kv),
            pl.BlockSpec((1, 1, LANE), kv),
        ],
        out_specs=pl.BlockSpec((1, 1, n_chunk, LANE), lambda i, b: (i, b, 0, 0)),
        out_shape=jax.ShapeDtypeStruct((2 * N_NSA_KV, batch, n_chunk, LANE), MXU_DTYPE),
        compiler_params=pltpu.CompilerParams(
            dimension_semantics=("arbitrary", "arbitrary"), vmem_limit_bytes=VMEM_LIMIT),
        name="compress",
    )(kvc, pos, w1, b1, w2p, b2p)


def _lane_max(s):
    out = s[:, 0:LANE]
    for c in range(1, s.shape[1] // LANE):
        out = jnp.maximum(out, s[:, LANE * c:LANE * (c + 1)])
    return out


def _row_max_lanes(mrun):
    return jnp.broadcast_to(jnp.max(mrun, axis=-1, keepdims=True), mrun.shape)


def _exp_tiles(s, mb):
    n = s.shape[1] // LANE
    return jnp.concatenate(
        [jnp.exp2(s[:, LANE * c:LANE * (c + 1)] - mb).astype(MXU_DTYPE) for c in range(n)], axis=1)


def _values_with_ones(v, ones_low):
    low = _lanes(v.shape) < HEAD_DIM
    ones = jnp.ones_like(v)
    return jnp.where(low, ones, v) if ones_low else jnp.where(low, v, ones)


def _normalize(acc):
    return acc / pltpu.roll(acc, HEAD_DIM, 1)


def _spans(n_tiles, groups, fn):
    start = 0
    for size in tuple(groups) + (1,):
        n_groups = lax.div(n_tiles - start, size)

        def step(t, carry, size=size, start=start):
            fn(start + t * size, size)
            return carry

        lax.fori_loop(0, n_groups, step, 0)
        start = start + n_groups * size


def _nsa_body(qn_ref, qr_ref, kc_ref, vc_ref, ksa_ref, vs_ref, kw_ref, vw_ref, sm_ref, gb_ref, impt_ref,
              o_ref, s_ref, mrun_ref, acc_ref, *, tq, tk, group, n_sel, top_k):
    q0 = pl.program_id(2) * tq
    r = NSA_GROUP * tq
    n_cmp = kc_ref.shape[2]

    def window_scores(hq):
        n_wc = WINDOW // LANE + 1
        upper = _lanes((NSA_GROUP * LANE, LANE)) > (_rows((NSA_GROUP * LANE, LANE)) & (LANE - 1))
        qs = q0 + hq * LANE
        qh = jnp.concatenate(
            [qr_ref[hq * LANE:(hq + 1) * LANE, LANE * h:LANE * (h + 1)] for h in range(NSA_GROUP)], axis=0)
        starts = [pl.multiple_of(jnp.maximum(qs - WINDOW + LANE * c, 0), LANE) for c in range(n_wc)]
        flag_lane = _lanes((LANE, LANE)) == HEAD_DIM
        k_chunks = []
        for c, st in enumerate(starts):
            kc = kw_ref[pl.ds(st, LANE), :]
            if c < n_wc - 1:
                before_start = jnp.logical_and(flag_lane, qs - WINDOW + LANE * c < 0)
                kc = jnp.where(before_start, jnp.ones_like(kc), kc)
            k_chunks.append(kc)
        kk = jnp.concatenate(k_chunks, axis=0)
        vv = jnp.concatenate([vw_ref[pl.ds(st, LANE), :] for st in starts], axis=0)
        q_flag = _lanes(qh.shape) == HEAD_DIM
        sw = _dot_nt(jnp.where(q_flag, jnp.full_like(qh, NEG_INF), qh), kk)
        chunks = [sw[:, LANE * c:LANE * (c + 1)] for c in range(n_wc)]
        chunks[0] = jnp.where(upper, chunks[0], NEG_INF)
        chunks[-1] = jnp.where(upper, NEG_INF, chunks[-1])
        return chunks, vv

    def window_probs(chunks):
        mb = _row_max_lanes(functools.reduce(jnp.maximum, chunks))
        return jnp.concatenate([jnp.exp2(ch - mb).astype(MXU_DTYPE) for ch in chunks], axis=1)

    def window_out(pw, vv):
        return _normalize(_dot(pw, _values_with_ones(vv, False)))

    q4n = jnp.concatenate([qn_ref[:, LANE * h:LANE * (h + 1)] for h in range(NSA_GROUP)], axis=0)
    sc = _dot_nt(q4n, kc_ref[0, 0])
    win_scores = [window_scores(0), window_scores(1)]
    last_vis = (q0 + _rows((tq, LANE)) - (CMP_LEN - 1)) >> CMP_SHIFT
    last_vis = jnp.concatenate([last_vis] * NSA_GROUP, axis=0)
    sc_chunks = [
        jnp.where(_lanes((r, LANE)) + LANE * c <= last_vis, sc[:, LANE * c:LANE * (c + 1)], NEG_INF)
        for c in range(n_cmp // LANE)]
    mb = jnp.maximum(_row_max_lanes(functools.reduce(jnp.maximum, sc_chunks)), 0.1 * NEG_INF)
    e_chunks = [jnp.exp2(ch - mb) for ch in sc_chunks]
    den = jnp.sum(functools.reduce(jnp.add, e_chunks), axis=-1, keepdims=True)
    inv = 1.0 / jnp.maximum(jnp.broadcast_to(den, (r, LANE)), 1e-30)
    pc = jnp.concatenate([ech * inv for ech in e_chunks], axis=1)
    o_cmp = _dot(pc.astype(MXU_DTYPE), vc_ref[0, 0])

    pcs = (pc[0:tq] + pc[tq:2 * tq]) + (pc[2 * tq:3 * tq] + pc[3 * tq:4 * tq])
    impt_w = impt_ref[...]
    hi, mid, lo = _split3(pcs)
    imp_t = (_dot_nt(impt_w, hi) + _dot_nt(impt_w, mid)) + _dot_nt(impt_w, lo)
    jrow = _rows((n_sel, tq))
    qpos = q0 + _lanes((n_sel, tq))
    cur = qpos >> SEL_SHIFT
    valid = jrow * SEL_BLOCK <= qpos
    forced = (jrow == 0) | (jrow == cur) | (jrow == cur - 1)
    score = jnp.where(forced, FORCED_SCORE, jnp.where(valid, imp_t, -1.0))

    sub = SUBLANE
    chunks = [score[sub * c:sub * (c + 1), :] for c in range(n_sel // sub)]
    ranks = [jnp.zeros((sub, tq), F32) for _ in chunks]
    later = _rows((sub, tq))

    win_probs = [window_probs(win_scores[0][0])]
    win_blocks = []
    for jp in range(n_sel):
        if jp == n_sel // 2:
            win_blocks.append(window_out(win_probs[0], win_scores[0][1]))
            win_probs.append(window_probs(win_scores[1][0]))
        cj, rj = divmod(jp, sub)
        rowv = chunks[cj][rj:rj + 1, :]
        for c, sc_c in enumerate(chunks):
            if c > cj:
                beats = rowv >= sc_c
            elif c < cj:
                beats = rowv > sc_c
            else:
                beats = (rowv > sc_c) | ((rowv == sc_c) & (later > rj))
            ranks[c] = ranks[c] + jnp.where(beats, 1.0, 0.0)
    sel_t = jnp.where(jnp.concatenate(ranks, axis=0) < float(top_k), 1.0, 0.0)
    sel_pad = jnp.concatenate([sel_t, jnp.zeros((LANE - n_sel, tq), F32)], axis=0)
    sel_q = pltpu.roll(sel_pad.T, HEAD_DIM, 1)
    sel_bias = ((sel_q - 1.0) * (-NEG_INF)).astype(MXU_DTYPE)

    high = _lanes((tq, LANE)) >= HEAD_DIM
    q4r = jnp.concatenate(
        [jnp.where(high, sel_bias, qr_ref[:, LANE * h:LANE * (h + 1)]) for h in range(NSA_GROUP)], axis=0)

    n_full = lax.div(q0, tk)
    assert tq == tk

    def score_span(kb, n_tiles, masked):
        k0 = pl.multiple_of(kb * tk, tk)
        s = _dot_nt(q4r, ksa_ref[pl.ds(k0, n_tiles * tk), :])
        if masked:
            s = jnp.where(_lanes((r, tk)) <= (_rows((r, tk)) & (tq - 1)), s, NEG_INF)
        for j in range(n_tiles):
            s_ref[kb + j] = s[:, j * tk:(j + 1) * tk]
        mrun_ref[...] = jnp.maximum(mrun_ref[...], _lane_max(s))

    def value_span(kb, n_tiles):
        k0 = pl.multiple_of(kb * tk, tk)
        mb = mrun_ref[...]
        p = jnp.concatenate([_exp_tiles(s_ref[kb + j], mb) for j in range(n_tiles)], axis=1)
        acc_ref[...] += _dot(p, _values_with_ones(vs_ref[pl.ds(k0, n_tiles * tk), :], False))

    win_blocks.append(window_out(win_probs[1], win_scores[1][1]))
    o_win = jnp.concatenate([blk[h * LANE:(h + 1) * LANE] for h in range(NSA_GROUP) for blk in win_blocks], axis=0)

    gate = jax.nn.sigmoid(sm_ref[...] + gb_ref[...])
    partial, gate_sel = [], []
    for h in range(NSA_GROUP):
        rs = slice(h * tq, (h + 1) * tq)
        c = N_BRANCH * h
        partial.append(gate[:, c:c + 1] * o_cmp[rs] + gate[:, c + 2:c + 3] * o_win[rs])
        gate_sel.append(jnp.broadcast_to(gate[:, c + 1:c + 2], (tq, LANE)))

    half_r = r // 2
    row_halves = [slice(0, half_r), slice(half_r, r)]

    def back(n):
        def sspan(kb, n_tiles, masked):
            k0 = kb * tk
            k = ksa_ref[k0:k0 + n_tiles * tk, :]
            for rows in row_halves:
                s = _dot_nt(q4r[rows], k)
                if masked:
                    s = jnp.where(_lanes((half_r, tk)) <= (_rows((half_r, tk)) & (tq - 1)), s, NEG_INF)
                for j in range(n_tiles):
                    s_ref[kb + j, rows, :] = s[:, j * tk:(j + 1) * tk]
                mrun_ref[rows, :] = jnp.maximum(mrun_ref[rows, :], _lane_max(s))

        def vspan(kb, n_tiles):
            k0 = kb * tk
            v = _values_with_ones(vs_ref[k0:k0 + n_tiles * tk, :], False)
            for rows in row_halves:
                mb = mrun_ref[rows, :]
                p = jnp.concatenate([_exp_tiles(s_ref[kb + j, rows, :], mb) for j in range(n_tiles)], axis=1)
                acc_ref[rows, :] += _dot(p, v)

        mrun_ref[...] = jnp.full((r, LANE), NEG_INF, F32)
        sspan(n, 1, True)
        _static_spans(n, group, lambda kb, nt: sspan(kb, nt, False))
        mrun_ref[...] = _row_max_lanes(mrun_ref[...])
        acc_ref[...] = jnp.zeros((r, LANE), F32)
        _static_spans(n + 1, group, vspan)
        o_sel = _normalize(acc_ref[...])

        heads = [partial[h] + gate_sel[h] * o_sel[h * tq:(h + 1) * tq] for h in range(NSA_GROUP)]
        low = _lanes((tq, LANE)) < HEAD_DIM
        for p in range(NSA_GROUP // 2):
            pair = jnp.where(low, heads[2 * p], pltpu.roll(heads[2 * p + 1], HEAD_DIM, 1))
            o_ref[:, LANE * p:LANE * (p + 1)] = pair.astype(MXU_DTYPE)

    for n in range(s_ref.shape[0]):
        pl.when(pl.program_id(2) == n)(functools.partial(back, n))


def _nsa(qn, qr, cmp_kv, ksa, vs, kw, vw, small, gbias, imp_t, *, batch, seq, tq, tk, group):
    assert tq == 2 * LANE
    n_q = seq // tq
    n_cmp = cmp_kv.shape[2]
    n_sel = seq // SEL_BLOCK
    top_k = min(SEL_TOPK, n_sel)
    gw = NSA_GROUP * LANE
    qmap = lambda b, g, i: (b * n_q + i, g)
    kvmap = lambda b, g, i: (b, g)
    r = NSA_GROUP * tq
    return pl.pallas_call(
        functools.partial(_nsa_body, tq=tq, tk=tk, group=group, n_sel=n_sel, top_k=top_k),
        grid=(batch, N_NSA_KV, n_q),
        in_specs=[
            pl.BlockSpec((tq, gw), qmap),
            pl.BlockSpec((tq, gw), qmap),
            pl.BlockSpec((1, 1, n_cmp, LANE), lambda b, g, i: (g, b, 0, 0)),
            pl.BlockSpec((1, 1, n_cmp, LANE), lambda b, g, i: (N_NSA_KV + g, b, 0, 0)),
            pl.BlockSpec((seq, LANE), kvmap),
            pl.BlockSpec((seq, LANE), kvmap),
            pl.BlockSpec((seq, LANE), kvmap),
            pl.BlockSpec((seq, LANE), kvmap),
            pl.BlockSpec((tq, LANE), qmap),
            pl.BlockSpec((1, LANE), lambda b, g, i: (0, g)),
            pl.BlockSpec((n_sel, n_cmp), lambda b, g, i: (0, 0)),
        ],
        out_specs=pl.BlockSpec((tq, NSA_GROUP * HEAD_DIM), qmap),
        out_shape=jax.ShapeDtypeStruct((batch * seq, N_NSA_HEADS * HEAD_DIM), MXU_DTYPE),
        scratch_shapes=[
            pltpu.VMEM((seq // tk, r, tk), F32),
            pltpu.VMEM((r, LANE), F32),
            pltpu.VMEM((r, LANE), F32),
        ],
        compiler_params=pltpu.CompilerParams(
            dimension_semantics=("arbitrary", "arbitrary", "arbitrary"), vmem_limit_bytes=VMEM_LIMIT),
        name="nsa_attention",
    )(qn, qr, cmp_kv, cmp_kv, ksa, vs, kw, vw, small, gbias, imp_t)


def _fox_body(q_ref, k_ref, v_ref, cum_ref, o_ref, s_ref, mrun_ref, acc_ref, *, tq, tk, group, n_q):
    for n in range(n_q):
        pl.when(pl.program_id(2) == n)(functools.partial(
            _fox_block, q_ref, k_ref, v_ref, cum_ref, o_ref, s_ref, mrun_ref, acc_ref,
            tq=tq, tk=tk, group=group, n_full=n))


def _static_spans(n_tiles, group, fn):
    kb = 0
    while kb < n_tiles:
        size = min(group, n_tiles - kb)
        fn(kb, size)
        kb += size


def _fox_block(q_ref, k_ref, v_ref, cum_ref, o_ref, s_ref, mrun_ref, acc_ref, *, tq, tk, group, n_full):
    q0 = n_full * tq
    qp = q_ref[...]
    low = _lanes((tq, LANE)) < HEAD_DIM
    assert tq == tk
    qs, cqs, cum_rows = [], [], []
    for e in range(2):
        qs.append(jnp.where(low if e == 0 else jnp.logical_not(low), qp, jnp.zeros_like(qp)))
        cum_rows.append(2 * pl.program_id(1) + e)
        cq = jnp.broadcast_to(cum_ref[0, pl.ds(cum_rows[e], 1), q0:q0 + tq], (LANE, tq)).T
        cqs.append(jnp.concatenate([cq] * (tk // LANE), axis=1))

    def score_span(kb, n_tiles, masked):
        k0 = kb * tk
        k = k_ref[k0:k0 + n_tiles * tk, :]
        for e in range(2):
            s = _dot_nt(qs[e], k)
            for j in range(n_tiles):
                ck = cum_ref[0, pl.ds(cum_rows[e], 1), k0 + j * tk:k0 + (j + 1) * tk]
                sj = s[:, j * tk:(j + 1) * tk] + cqs[e] - ck
                if masked:
                    sj = jnp.where(_lanes((tq, tk)) <= _rows((tq, tk)), sj, NEG_INF)
                s_ref[e, kb + j] = sj
                mrun_ref[e] = jnp.maximum(mrun_ref[e], _lane_max(sj))

    def value_span(kb, n_tiles):
        k0 = kb * tk
        v = v_ref[k0:k0 + n_tiles * tk, :]
        for e in range(2):
            mb = mrun_ref[e]
            p = jnp.concatenate([_exp_tiles(s_ref[e, kb + j], mb) for j in range(n_tiles)], axis=1)
            acc_ref[e] += _dot(p, _values_with_ones(v, e == 1))

    mrun_ref[...] = jnp.full((2, tq, LANE), NEG_INF, F32)
    score_span(n_full, 1, True)
    _static_spans(n_full, group, lambda kb, n: score_span(kb, n, False))
    for e in range(2):
        mrun_ref[e] = _row_max_lanes(mrun_ref[e])
    acc_ref[...] = jnp.zeros((2, tq, LANE), F32)
    _static_spans(n_full + 1, group, value_span)
    o_ref[...] = jnp.where(low, _normalize(acc_ref[0]), _normalize(acc_ref[1])).astype(MXU_DTYPE)


def _fox(fq, fk, fv, cum, *, batch, seq, tq, tk, group):
    n_q = seq // tq
    n_pairs = N_FOX_HEADS // 2
    qmap = lambda b, p, i: (b * n_q + i, p)
    kvmap = lambda b, p, i: (b, p)
    return pl.pallas_call(
        functools.partial(_fox_body, tq=tq, tk=tk, group=group, n_q=n_q),
        grid=(batch, n_pairs, n_q),
        in_specs=[
            pl.BlockSpec((tq, LANE), qmap),
            pl.BlockSpec((seq, LANE), kvmap),
            pl.BlockSpec((seq, LANE), kvmap),
            pl.BlockSpec((1, N_FOX_HEADS, seq), lambda b, p, i: (b, 0, 0)),
        ],
        out_specs=pl.BlockSpec((tq, LANE), qmap),
        out_shape=jax.ShapeDtypeStruct((batch * seq, N_FOX_HEADS * HEAD_DIM), MXU_DTYPE),
        scratch_shapes=[
            pltpu.VMEM((2, seq // tk, tq, tk), F32),
            pltpu.VMEM((2, tq, LANE), F32),
            pltpu.VMEM((2, tq, LANE), F32),
        ],
        compiler_params=pltpu.CompilerParams(
            dimension_semantics=("arbitrary", "arbitrary", "arbitrary"), vmem_limit_bytes=VMEM_LIMIT),
        name="fox_attention",
    )(fq, fk, fv, cum)


def _post_body(on_ref, of_ref, x_ref, wo_ref, gpm_ref, gpl_ref, wu_ref, wd_ref, gpo_ref, o_ref, *, ff_chunk):
    n_nsa = on_ref.shape[1]
    d_ff = wu_ref.shape[1]
    n_ff = d_ff // ff_chunk
    tm = x_ref.shape[0]
    halves = [slice(0, tm // 2), slice(tm // 2, tm)]

    def out_proj(rows):
        return _dot(on_ref[rows, :], wo_ref[0:n_nsa, :]) + _dot(of_ref[rows, :], wo_ref[n_nsa:, :])

    def norms(rows, mix):
        x1 = x_ref[rows, :] + _rms(mix, gpm_ref[...])
        return x1, _rms(x1, gpl_ref[...]).astype(MXU_DTYPE)

    def mlp_chunk(h, acc, c):
        sl = slice(c * ff_chunk, (c + 1) * ff_chunk)
        a = jnp.square(jnp.maximum(_dot(h, wu_ref[:, sl]), 0.0))
        return acc + _dot(a.astype(MXU_DTYPE), wd_ref[sl, :])

    mix_a, mix_b = out_proj(halves[0]), out_proj(halves[1])
    x1_a, h_a = norms(halves[0], mix_a)
    acc_a = mlp_chunk(h_a, jnp.zeros(x1_a.shape, F32), 0)
    x1_b, h_b = norms(halves[1], mix_b)
    for c in range(1, n_ff):
        acc_a = mlp_chunk(h_a, acc_a, c)
    acc_b = mlp_chunk(h_b, jnp.zeros(x1_b.shape, F32), 0)
    o_ref[halves[0], :] = x1_a + _rms(acc_a, gpo_ref[...])
    for c in range(1, n_ff):
        acc_b = mlp_chunk(h_b, acc_b, c)
    o_ref[halves[1], :] = x1_b + _rms(acc_b, gpo_ref[...])


def _post(o_nsa, o_fox, xf, wo, g_post_mix, g_pre_mlp, wu, wd, g_post_mlp, *, layer, tm, ff_chunk):
    m, d = xf.shape
    d_ff = wu.shape[2]
    row = lambda i: (i, 0)
    fixed = lambda i: (0, 0)
    resident = lambda rows, cols: pl.BlockSpec(
        (None, rows, cols), lambda i: (layer, 0, 0), pipeline_mode=pl.Buffered(1))
    return pl.pallas_call(
        functools.partial(_post_body, ff_chunk=ff_chunk),
        grid=(m // tm,),
        in_specs=[
            pl.BlockSpec((tm, o_nsa.shape[1]), row),
            pl.BlockSpec((tm, o_fox.shape[1]), row),
            pl.BlockSpec((tm, d), row),
            resident(wo.shape[1], d),
            pl.BlockSpec((1, d), fixed),
            pl.BlockSpec((1, d), fixed),
            resident(d, d_ff),
            resident(d_ff, d),
            pl.BlockSpec((1, d), fixed),
        ],
        out_specs=pl.BlockSpec((tm, d), row),
        out_shape=jax.ShapeDtypeStruct((m, d), F32),
        compiler_params=pltpu.CompilerParams(
            dimension_semantics=("arbitrary",), vmem_limit_bytes=VMEM_LIMIT),
        name="outproj_mlp",
    )(o_nsa, o_fox, xf, wo, g_post_mix, g_pre_mlp, wu, wd, g_post_mlp)


def _rope_tables(seq):
    half = ROPE_DIMS // 2
    inv_freq = ROPE_THETA ** (-jnp.arange(half, dtype=F32) * 2.0 / ROPE_DIMS)
    ang = jnp.arange(seq).astype(F32)[:, None] * inv_freq[None, :]
    cos, sin = jnp.cos(ang), jnp.sin(ang)
    pad = jnp.zeros((seq, LANE - ROPE_DIMS), F32)
    zero = jnp.zeros((seq, half), F32)
    c = jnp.concatenate([cos, cos, pad + 1.0], axis=-1)
    a = jnp.concatenate([-sin, zero, pad], axis=-1)
    b = jnp.concatenate([zero, sin, pad], axis=-1)
    return c, a, b


def _importance_map_t(seq, n_chunk):
    n_cmp = (seq - CMP_LEN) // CMP_STRIDE + 1
    n_sel = seq // SEL_BLOCK
    cs = np.arange(n_cmp)[None, :] * CMP_STRIDE
    ss = np.arange(n_sel)[:, None] * SEL_BLOCK
    overlap = np.clip(np.minimum(cs + CMP_LEN, ss + SEL_BLOCK) - np.maximum(cs, ss), 0, None) / CMP_LEN
    out = np.zeros((n_sel, n_chunk), np.float32)
    out[:, :n_cmp] = overlap
    return jnp.asarray(out, dtype=MXU_DTYPE)


def kernel(x, w_in, b_nsa_gate, b_forget, cmp_pos_k, cmp_w1_k, cmp_b1_k, cmp_w2_k, cmp_b2_k, cmp_pos_v, cmp_w1_v, cmp_b1_v, cmp_w2_v, cmp_b2_v, w_out, w_up, w_down, g_pre_mix, g_post_mix, g_pre_mlp, g_post_mlp):
    batch, seq, d_model = x.shape
    depth = w_in.shape[0]
    m = batch * seq
    n_chunk = seq // CMP_STRIDE
    n_sel = seq // SEL_BLOCK
    tm = min(512, seq)
    nsa_tq, nsa_tk = 2 * LANE, min(256, seq)
    fox_t = min(512, seq)
    assert seq % tm == 0 and seq % fox_t == 0 and seq % nsa_tk == 0
    assert n_sel <= HEAD_DIM and n_sel % 8 == 0 and seq >= WINDOW + nsa_tq
    assert x.dtype == F32

    rope_c, rope_a, rope_b = _rope_tables(seq)
    imp_t = _importance_map_t(seq, n_chunk)
    cs_chunk = min(256, seq)
    tril = jnp.asarray(np.tril(np.ones((cs_chunk, cs_chunk), np.float32)), dtype=MXU_DTYPE)

    wp = _permute_columns(w_in.astype(MXU_DTYPE), _projection_columns())

    gate_bias = jnp.zeros((depth, 1, N_NSA_KV * LANE), F32)
    fbias = jnp.zeros((depth, 1, LANE), F32)
    n_gate = NSA_GROUP * N_BRANCH
    for g in range(N_NSA_KV):
        gate_bias = gate_bias.at[:, 0, LANE * g:LANE * g + n_gate].set(b_nsa_gate[:, n_gate * g:n_gate * (g + 1)])
    fbias = fbias.at[:, 0, _FF_LANE:_FF_LANE + N_FOX_HEADS].set(b_forget)

    half = CMP_STRIDE * HEAD_DIM
    cpos = jnp.stack([cmp_pos_k.reshape(depth, 2, half), cmp_pos_v.reshape(depth, 2, half)], axis=1)
    cw1 = jnp.stack([cmp_w1_k, cmp_w1_v], axis=1).astype(MXU_DTYPE)
    cb1 = jnp.stack([cmp_b1_k, cmp_b1_v], axis=1)[:, :, None, :]
    zw = jnp.zeros_like(cmp_w2_k)
    cw2 = jnp.stack([jnp.concatenate([cmp_w2_k, zw], axis=-1),
                     jnp.concatenate([cmp_w2_v, cmp_w2_v], axis=-1)], axis=1).astype(MXU_DTYPE)
    zb = jnp.zeros_like(cmp_b2_k)
    cb2 = jnp.stack([jnp.concatenate([cmp_b2_k, zb], axis=-1),
                     jnp.concatenate([cmp_b2_v, cmp_b2_v], axis=-1)], axis=1)[:, :, None, :]

    wo = w_out.astype(MXU_DTYPE)
    wu = w_up.astype(MXU_DTYPE)
    wd = w_down.astype(MXU_DTYPE)

    xf = x.reshape(m, d_model)
    for l in range(depth):
        qn, qr, kvc, ksa, vs, kw, vw, fq, fk, fv, small = _inproj(
            xf, g_pre_mix[l][None, :], wp, rope_c, rope_a, rope_b, layer=l, seq=seq, tm=tm)

        cum = _forget_cumsum(small, fbias[l], tril, batch=batch, seq=seq)

        cmp_kv = _compress(kvc, cpos[l], cw1[l], cb1[l], cw2[l], cb2[l],
                           batch=batch, n_chunk=n_chunk)
        o_nsa = _nsa(qn, qr, cmp_kv, ksa, vs, kw, vw, small, gate_bias[l], imp_t,
                     batch=batch, seq=seq, tq=nsa_tq, tk=nsa_tk, group=8)
        o_fox = _fox(fq, fk, fv, cum, batch=batch, seq=seq, tq=fox_t, tk=fox_t, group=4)
        xf = _post(o_nsa, o_fox, xf, wo, g_post_mix[l][None, :], g_pre_mlp[l][None, :],
                   wu, wd, g_post_mlp[l][None, :], layer=l, tm=tm, ff_chunk=1024)
    return xf.reshape(batch, seq, d_model)
```
